```python
import math
import jax, jax.numpy as jnp
from jax import lax
import numpy as np


D_MODEL = 2048
BATCH = 4
SEQ = 4096
DEPTH = 2

A_HEADS = 8
A_HEAD_DIM = 128
A_KV_RANK = 256
IDX_HEADS = 16
IDX_HEAD_DIM = 64
IDX_ROPE_DIM = 32
ROPE_THETA = 10000.0
DSA_TOPK = 256
Q_BLOCK = 128
B_GROUPS = ((128, 1), (512, 4), (2048, 16))
B_HEADS_PER_GROUP = 4
B_HEAD_DIM = 128
B_HEADS = B_HEADS_PER_GROUP * len(B_GROUPS)
BAND_BLOCK = 128
REL_BUCKETS = 32
REL_MAX_DISTANCE = 2048
N_BIAS_HEADS = A_HEADS + B_HEADS
D_FF = -(-8 * D_MODEL // (3 * 256)) * 256
NORM_EPS = 1e-6
NEG_INF = -1e30

A_Q_COLS = A_HEADS * A_HEAD_DIM
A_KV_COLS = A_KV_RANK
IDX_Q_COLS = IDX_HEADS * IDX_HEAD_DIM
IDX_K_COLS = IDX_HEAD_DIM
IDX_W_COLS = IDX_HEADS
B_QKV_COLS = 3 * B_HEADS * B_HEAD_DIM
GATE_COLS = 2 * D_MODEL
IN_SIZES = (A_Q_COLS, A_KV_COLS, IDX_Q_COLS, IDX_K_COLS, IDX_W_COLS, B_QKV_COLS, GATE_COLS)
N_IN = sum(IN_SIZES)
A_OUT = A_HEADS * A_HEAD_DIM
B_OUT = B_HEADS_PER_GROUP * B_HEAD_DIM

kernel_name = 'hybrid_dsa_dilated_gated_block'


def rms_norm(x, g):
    xf = x.astype(jnp.float32)
    y = xf * lax.rsqrt(jnp.mean(xf * xf, axis=-1, keepdims=True) + NORM_EPS)
    return y.astype(x.dtype) * g


def layer_norm(x, g, b):
    xf = x.astype(jnp.float32)
    mu = jnp.mean(xf, axis=-1, keepdims=True)
    var = jnp.mean(jnp.square(xf - mu), axis=-1, keepdims=True)
    return ((xf - mu) * lax.rsqrt(var + NORM_EPS)).astype(x.dtype) * g + b


def rel_bucket(dist):
    n = jnp.maximum(dist, 0)
    exact = REL_BUCKETS // 2
    nf = jnp.maximum(n, 1).astype(jnp.float32)
    large = exact + (jnp.log(nf / exact) / math.log(REL_MAX_DISTANCE / exact)
                     * (REL_BUCKETS - exact)).astype(jnp.int32)
    return jnp.where(n < exact, n, jnp.minimum(large, REL_BUCKETS - 1))


def rope_partial(x, pos):
    half = IDX_ROPE_DIM // 2
    freqs = ROPE_THETA ** (-jnp.arange(half, dtype=jnp.float32) / half)
    ang = pos.astype(jnp.float32)[:, None] * freqs[None, :]
    cos = jnp.cos(ang)[None, :, None, :]
    sin = jnp.sin(ang)[None, :, None, :]
    xr = x[..., :IDX_ROPE_DIM].astype(jnp.float32)
    x1, x2 = xr[..., :half], xr[..., half:]
    rot = jnp.concatenate([x1 * cos - x2 * sin, x1 * sin + x2 * cos], axis=-1).astype(x.dtype)
    return jnp.concatenate([rot, x[..., IDX_ROPE_DIM:]], axis=-1)


def dsa_attention(q, c_kv, q_idx, k_idx, w_idx, w_uk, w_uv, bias_a):
    B, L = q.shape[0], q.shape[1]
    topk = min(DSA_TOPK, L // 4)
    nq = L // Q_BLOCK
    q_lat = jnp.einsum('blhd,hdc->blhc', q, w_uk) * (A_HEAD_DIM ** -0.5)
    key_pos = jnp.arange(L)

    def to_blocks(a):
        return jnp.swapaxes(a.reshape(B, nq, Q_BLOCK, *a.shape[2:]), 0, 1)

    def block(args):
        ql, qi, wi, t0 = args
        qpos = t0 + jnp.arange(Q_BLOCK)
        logits = jnp.einsum('bphd,bsd->bphs', qi, k_idx)
        score = jnp.einsum('bph,bphs->bps', wi, jax.nn.relu(logits)).astype(jnp.float32)
        causal = key_pos[None, :] <= qpos[:, None]
        score = jnp.where(causal[None], score, -jnp.inf)
        _, idx = lax.top_k(score, topk)
        kv_sel = jax.vmap(lambda c, i: c[i])(c_kv, idx)
        s = jnp.einsum('bphc,bpkc->bphk', ql, kv_sel).astype(jnp.float32)
        dist = qpos[None, :, None] - idx
        s = s + jnp.moveaxis(bias_a[rel_bucket(dist)], -1, 2).astype(jnp.float32)
        s = jnp.where((dist >= 0)[:, :, None, :], s, NEG_INF)
        p = jax.nn.softmax(s, axis=-1).astype(kv_sel.dtype)
        return jnp.einsum('bphk,bpkc->bphc', p, kv_sel)

    t0s = jnp.arange(nq, dtype=jnp.int32) * Q_BLOCK
    o_lat = lax.map(block, (to_blocks(q_lat), to_blocks(q_idx), to_blocks(w_idx), t0s))
    o_lat = jnp.swapaxes(o_lat, 0, 1).reshape(B, L, A_HEADS, A_KV_RANK)
    return jnp.einsum('blhc,hcd->blhd', o_lat, w_uv).reshape(B, L, A_OUT)


def dilated_group(q, k, v, bias_g, dilation, window):
    B, L, H, dh = q.shape
    steps = window // dilation
    P = BAND_BLOCK
    M = L // dilation
    nb = -(-M // P)
    Mp = nb * P

    def strided(a):
        return a.reshape(B, M, dilation, H, dh).transpose(0, 2, 1, 3, 4)

    qs = jnp.pad(strided(q), ((0, 0), (0, 0), (0, Mp - M), (0, 0), (0, 0))).reshape(B, dilation, nb, P, H, dh)

    def key_blocks(a):
        a = jnp.pad(strided(a), ((0, 0), (0, 0), (P, Mp - M), (0, 0), (0, 0))).reshape(B, dilation, nb + 1, P, H, dh)
        return jnp.concatenate([a[:, :, :-1], a[:, :, 1:]], axis=3)

    ks, vs = key_blocks(k), key_blocks(v)
    i = jnp.arange(P)[:, None]
    u = jnp.arange(2 * P)[None, :]
    back = i + P - u
    m_key = (jnp.arange(nb) * P)[:, None, None] - P + u[None]
    valid = (back >= 0)[None] & (back <= steps)[None] & (m_key >= 0)
    bias = jnp.transpose(bias_g[rel_bucket(back * dilation)], (2, 0, 1)).astype(jnp.float32)
    s = jnp.einsum('bdnihe,bdnuhe->bdnhiu', qs, ks).astype(jnp.float32) * (dh ** -0.5) + bias
    s = jnp.where(valid[None, None, :, None], s, NEG_INF)
    mx = jnp.max(s, axis=-1, keepdims=True)
    p = jnp.exp(s - mx)
    den = jnp.sum(p, axis=-1, keepdims=True)
    o = jnp.einsum('bdnhiu,bdnuhe->bdnihe', (p / den).astype(v.dtype), vs)
    lse = (mx + jnp.log(den))[..., 0]
    o = o.reshape(B, dilation, Mp, H, dh)[:, :, :M].transpose(0, 2, 1, 3, 4).reshape(B, L, H, dh)
    lse = lse.transpose(0, 1, 2, 4, 3).reshape(B, dilation, Mp, H)[:, :, :M].transpose(0, 2, 1, 3).reshape(B, L, H)
    return o, lse


def dilated_mixture(qkv_b, bias_b):
    B, L = qkv_b.shape[0], qkv_b.shape[1]
    outs, lses = [], []
    for g, (window, dil) in enumerate(B_GROUPS):
        o, lse = dilated_group(qkv_b[:, :, 0, g], qkv_b[:, :, 1, g], qkv_b[:, :, 2, g],
                               bias_b[:, g * B_HEADS_PER_GROUP:(g + 1) * B_HEADS_PER_GROUP], dil, window)
        outs.append(o)
        lses.append(lse)
    alpha = jax.nn.softmax(jnp.stack(lses), axis=0)
    out = jnp.sum(alpha[..., None] * jnp.stack(outs).astype(jnp.float32), axis=0)
    return out.astype(qkv_b.dtype).reshape(B, L, B_OUT)


def hybrid_layer(x, c, rel_bias, w_ada, b_ada, norm1_g, w_in, kv_norm_g, idx_ln_g, idx_ln_b,
                 w_uk, w_uv, w_a_up, w_b_up, w_out, norm2_g, w_ff_gate, w_ff_up, w_ff_down):
    B, L, _ = x.shape
    mod = jax.nn.silu(c) @ w_ada + b_ada
    shift1, scale1, gate1, shift2, scale2, gate2 = [m[:, None, :] for m in jnp.split(mod, 6, axis=-1)]

    h = rms_norm(x, norm1_g) * (1 + scale1) + shift1
    z = h @ w_in
    split_pts = [int(p) for p in np.cumsum(IN_SIZES)[:-1]]
    q_a, ckv, q_i, k_i, w_i, qkv_b, gates = jnp.split(z, split_pts, axis=-1)
    pos = jnp.arange(L)
    q_a = q_a.reshape(B, L, A_HEADS, A_HEAD_DIM)
    ckv = rms_norm(ckv, kv_norm_g)
    q_i = rope_partial(q_i.reshape(B, L, IDX_HEADS, IDX_HEAD_DIM), pos)
    k_i = rope_partial(layer_norm(k_i, idx_ln_g, idx_ln_b)[:, :, None, :], pos)[:, :, 0, :]
    w_i = w_i * (IDX_HEADS ** -0.5 * IDX_HEAD_DIM ** -0.5)
    y_a = dsa_attention(q_a, ckv, q_i, k_i, w_i, w_uk, w_uv, rel_bias[:, :A_HEADS])
    y_b = dilated_mixture(qkv_b.reshape(B, L, 3, len(B_GROUPS), B_HEADS_PER_GROUP, B_HEAD_DIM),
                          rel_bias[:, A_HEADS:])
    g_a, g_b = jnp.split(jax.nn.sigmoid(gates), 2, axis=-1)
    merged = g_a * (y_a @ w_a_up) + g_b * (y_b @ w_b_up)
    x = x + gate1 * (merged @ w_out)

    h2 = rms_norm(x, norm2_g) * (1 + scale2) + shift2
    ff = (jax.nn.silu(h2 @ w_ff_gate) * (h2 @ w_ff_up)) @ w_ff_down
    return x + gate2 * ff


def setup_inputs(seed: int = 0) -> dict:
    key = jax.random.key(seed)
    ks = jax.random.split(key, 22)

    def nrm(k, shape, scale):
        return jax.random.normal(k, shape, jnp.float32) * scale

    return {
        'x': nrm(ks[0], (BATCH, SEQ, D_MODEL), 1.0),
        'c': nrm(ks[1], (BATCH, D_MODEL), 1.0),
        'rel_bias': nrm(ks[2], (REL_BUCKETS, N_BIAS_HEADS), 0.5),
        'w_ada': nrm(ks[3], (DEPTH, D_MODEL, 6 * D_MODEL), 0.5 * D_MODEL ** -0.5),
        'b_ada': nrm(ks[4], (DEPTH, 6 * D_MODEL), 0.02),
        'norm1_g': 1.0 + nrm(ks[5], (DEPTH, D_MODEL), 0.02),
        'w_in': nrm(ks[6], (DEPTH, D_MODEL, N_IN), D_MODEL ** -0.5),
        'kv_norm_g': 1.0 + nrm(ks[7], (DEPTH, A_KV_RANK), 0.02),
        'idx_ln_g': 1.0 + nrm(ks[8], (DEPTH, IDX_HEAD_DIM), 0.02),
        'idx_ln_b': nrm(ks[9], (DEPTH, IDX_HEAD_DIM), 0.02),
        'w_uk': nrm(ks[10], (DEPTH, A_HEADS, A_HEAD_DIM, A_KV_RANK), A_HEAD_DIM ** -0.5),
        'w_uv': nrm(ks[11], (DEPTH, A_HEADS, A_KV_RANK, A_HEAD_DIM), A_KV_RANK ** -0.5),
        'w_a_up': nrm(ks[12], (DEPTH, A_OUT, D_MODEL), A_OUT ** -0.5),
        'w_b_up': nrm(ks[13], (DEPTH, B_OUT, D_MODEL), B_OUT ** -0.5),
        'w_out': nrm(ks[14], (DEPTH, D_MODEL, D_MODEL), D_MODEL ** -0.5),
        'norm2_g': 1.0 + nrm(ks[15], (DEPTH, D_MODEL), 0.02),
        'w_ff_gate': nrm(ks[16], (DEPTH, D_MODEL, D_FF), D_MODEL ** -0.5),
        'w_ff_up': nrm(ks[17], (DEPTH, D_MODEL, D_FF), D_MODEL ** -0.5),
        'w_ff_down': nrm(ks[18], (DEPTH, D_FF, D_MODEL), D_FF ** -0.5),
        'final_g': 1.0 + nrm(ks[19], (D_MODEL,), 0.02),
    }


def reference(x, c, rel_bias, w_ada, b_ada, norm1_g, w_in, kv_norm_g, idx_ln_g, idx_ln_b,
              w_uk, w_uv, w_a_up, w_b_up, w_out, norm2_g, w_ff_gate, w_ff_up, w_ff_down, final_g):
    for l in range(DEPTH):
        x = hybrid_layer(x, c, rel_bias, w_ada[l], b_ada[l], norm1_g[l], w_in[l], kv_norm_g[l],
                         idx_ln_g[l], idx_ln_b[l], w_uk[l], w_uv[l], w_a_up[l], w_b_up[l], w_out[l],
                         norm2_g[l], w_ff_gate[l], w_ff_up[l], w_ff_down[l])
    return rms_norm(x, final_g)
```

```python
import functools
import math

import jax
import jax.numpy as jnp
from jax import lax
from jax.experimental import pallas as pl
from jax.experimental.pallas import tpu as pltpu

D_MODEL = 2048
A_HEADS = 8
A_HEAD_DIM = 128
A_KV_RANK = 256
IDX_HEADS = 16
IDX_HEAD_DIM = 64
IDX_ROPE_DIM = 32
ROPE_THETA = 10000.0
DSA_TOPK = 256
B_GROUPS = ((128, 1), (512, 4), (2048, 16))
B_HEADS_PER_GROUP = 4
B_HEAD_DIM = 128
B_HEADS = B_HEADS_PER_GROUP * len(B_GROUPS)
BAND_BLOCK = 128
REL_BUCKETS = 32
REL_MAX_DISTANCE = 2048
N_BIAS_HEADS = A_HEADS + B_HEADS
D_FF = -(-8 * D_MODEL // (3 * 256)) * 256
NORM_EPS = 1e-6
NEG_INF = -1e30

A_Q_COLS = A_HEADS * A_HEAD_DIM
IDX_Q_COLS = IDX_HEADS * IDX_HEAD_DIM
B_QKV_COLS = 3 * B_HEADS * B_HEAD_DIM
GATE_COLS = 2 * D_MODEL
IN_SIZES = (A_Q_COLS, A_KV_RANK, IDX_Q_COLS, IDX_HEAD_DIM, IDX_HEADS, B_QKV_COLS, GATE_COLS)
A_OUT = A_HEADS * A_HEAD_DIM
B_OUT = B_HEADS_PER_GROUP * B_HEAD_DIM

LANES = 128
SUBLANES = 8
VMEM_LIMIT_BYTES = 56 * 1024 * 1024

BF16 = jnp.bfloat16
F32 = jnp.float32
INT_MIN = -2 ** 31

FAR_DELTA = -(-(REL_MAX_DISTANCE + BAND_BLOCK - 1) // BAND_BLOCK)


def _dot(a, b):
    return jnp.dot(a, b, preferred_element_type=F32)


def _dot_nt(a, b):
    return lax.dot_general(a, b, (((1,), (1,)), ((), ())), preferred_element_type=F32)


def _params(*sem):
    return pltpu.CompilerParams(dimension_semantics=sem, vmem_limit_bytes=VMEM_LIMIT_BYTES)


def _mod_kernel(c_ref, w_ref, b_ref, o_ref):
    c = c_ref[...]
    a = (c * jax.nn.sigmoid(c)).astype(BF16)
    o_ref[0] = _dot(a, w_ref[0].astype(BF16)) + b_ref[0]


def _modulation(c_pad, w_ada, b_ada, tn=1024):
    depth, d, n6 = w_ada.shape
    rows = c_pad.shape[0]
    return pl.pallas_call(
        _mod_kernel,
        grid=(depth, n6 // tn),
        in_specs=[
            pl.BlockSpec((rows, d), lambda l, j: (0, 0)),
            pl.BlockSpec((1, d, tn), lambda l, j: (l, 0, j)),
            pl.BlockSpec((1, 1, tn), lambda l, j: (l, 0, j)),
        ],
        out_specs=pl.BlockSpec((1, rows, tn), lambda l, j: (l, 0, j)),
        out_shape=jax.ShapeDtypeStruct((depth, rows, n6), F32),
        compiler_params=_params("arbitrary", "arbitrary"),
        name="adaln_modulation",
    )(c_pad, w_ada, b_ada.reshape(depth, 1, n6))


def _bucket(dist):
    n = jnp.maximum(dist, 0)
    exact = REL_BUCKETS // 2
    nf = jnp.maximum(n, 1).astype(F32)
    large = exact + (jnp.log(nf / exact) / math.log(REL_MAX_DISTANCE / exact)
                     * (REL_BUCKETS - exact)).astype(jnp.int32)
    return jnp.where(n < exact, n, jnp.minimum(large, REL_BUCKETS - 1))


def _lookup(rb_ref, bucket, head):
    t = jnp.zeros(bucket.shape, F32)
    for k in range(REL_BUCKETS):
        t = jnp.where(bucket == k, rb_ref[k, head], t)
    return t


def _bias_a_kernel(rb_ref, o_ref):
    delta = pl.program_id(0)
    row = lax.broadcasted_iota(jnp.int32, (BAND_BLOCK, BAND_BLOCK), 0)
    col = lax.broadcasted_iota(jnp.int32, (BAND_BLOCK, BAND_BLOCK), 1)
    bucket = _bucket(delta * BAND_BLOCK + row - col)
    for h in range(A_HEADS):
        o_ref[h, 0] = _lookup(rb_ref, bucket, h)


def _bias_a_table(rel_bias, n_delta):
    return pl.pallas_call(
        _bias_a_kernel,
        grid=(n_delta,),
        in_specs=[pl.BlockSpec(memory_space=pltpu.SMEM)],
        out_specs=pl.BlockSpec((A_HEADS, 1, BAND_BLOCK, BAND_BLOCK), lambda d: (0, d, 0, 0)),
        out_shape=jax.ShapeDtypeStruct((A_HEADS, n_delta, BAND_BLOCK, BAND_BLOCK), F32),
        compiler_params=_params("arbitrary"),
        name="rel_bias_table_a",
    )(rel_bias)


def _bias_b_kernel(rb_ref, o_ref):
    g = pl.program_id(0)
    dil = jnp.where(g == 0, B_GROUPS[0][1], jnp.where(g == 1, B_GROUPS[1][1], B_GROUPS[2][1]))
    row = lax.broadcasted_iota(jnp.int32, (BAND_BLOCK, 2 * BAND_BLOCK), 0)
    u = lax.broadcasted_iota(jnp.int32, (BAND_BLOCK, 2 * BAND_BLOCK), 1)
    bucket = _bucket((row + BAND_BLOCK - u) * dil)
    for j in range(B_HEADS_PER_GROUP):
        o_ref[0, j] = _lookup(rb_ref, bucket, A_HEADS + g * B_HEADS_PER_GROUP + j)


def _bias_b_table(rel_bias):
    ng = len(B_GROUPS)
    return pl.pallas_call(
        _bias_b_kernel,
        grid=(ng,),
        in_specs=[pl.BlockSpec(memory_space=pltpu.SMEM)],
        out_specs=pl.BlockSpec((1, B_HEADS_PER_GROUP, BAND_BLOCK, 2 * BAND_BLOCK),
                               lambda g: (g, 0, 0, 0)),
        out_shape=jax.ShapeDtypeStruct((ng, B_HEADS_PER_GROUP, BAND_BLOCK, 2 * BAND_BLOCK), F32),
        compiler_params=_params("arbitrary"),
        name="rel_bias_table_b",
    )(rel_bias)


def _rms(x):
    return x * lax.rsqrt(jnp.mean(x * x, axis=-1, keepdims=True) + NORM_EPS)


def _norm_mod_kernel(x_ref, g_ref, sc_ref, sh_ref, o_ref):
    y = _rms(x_ref[...]) * g_ref[...]
    o_ref[...] = (y * (1.0 + sc_ref[0]) + sh_ref[0]).astype(o_ref.dtype)


def _norm_mod(x, g, scale, shift, seq, tm=512):
    n, d = x.shape
    per = seq // tm
    return pl.pallas_call(
        _norm_mod_kernel,
        grid=(n // tm,),
        in_specs=[
            pl.BlockSpec((tm, d), lambda i: (i, 0)),
            pl.BlockSpec((1, d), lambda i: (0, 0)),
            pl.BlockSpec((1, 1, d), lambda i: (i // per, 0, 0)),
            pl.BlockSpec((1, 1, d), lambda i: (i // per, 0, 0)),
        ],
        out_specs=pl.BlockSpec((tm, d), lambda i: (i, 0)),
        out_shape=jax.ShapeDtypeStruct((n, d), BF16),
        compiler_params=_params("arbitrary"),
        name="norm_modulate",
    )(x, g.reshape(1, d), scale, shift)


def _final_norm_kernel(x_ref, g_ref, o_ref):
    o_ref[...] = _rms(x_ref[...]) * g_ref[...]


def _final_norm(x, g, tm=512):
    n, d = x.shape
    return pl.pallas_call(
        _final_norm_kernel,
        grid=(n // tm,),
        in_specs=[pl.BlockSpec((tm, d), lambda i: (i, 0)), pl.BlockSpec((1, d), lambda i: (0, 0))],
        out_specs=pl.BlockSpec((tm, d), lambda i: (i, 0)),
        out_shape=jax.ShapeDtypeStruct((n, d), F32),
        compiler_params=_params("arbitrary"),
        name="final_norm",
    )(x, g.reshape(1, d))


def _qlat_kernel(h_ref, w_ref, wuk_ref, o_ref):
    z = _dot(h_ref[...], w_ref[...])
    for hd in range(A_HEADS):
        zh = z[:, hd * A_HEAD_DIM:(hd + 1) * A_HEAD_DIM].astype(BF16)
        o_ref[hd] = (_dot(zh, wuk_ref[hd]) * (A_HEAD_DIM ** -0.5)).astype(o_ref.dtype)


def _proj_qlat(h, w, wuk, tm=512):
    n, d = h.shape
    return pl.pallas_call(
        _qlat_kernel,
        grid=(n // tm,),
        in_specs=[
            pl.BlockSpec((tm, d), lambda i: (i, 0)),
            pl.BlockSpec((d, A_Q_COLS), lambda i: (0, 0)),
            pl.BlockSpec((A_HEADS, A_HEAD_DIM, A_KV_RANK), lambda i: (0, 0, 0)),
        ],
        out_specs=pl.BlockSpec((A_HEADS, tm, A_KV_RANK), lambda i: (0, i, 0)),
        out_shape=jax.ShapeDtypeStruct((A_HEADS, n, A_KV_RANK), BF16),
        compiler_params=_params("arbitrary"),
        name="proj_q_latent",
    )(h, w, wuk)


def _ckv_kernel(h_ref, w_ref, g_ref, o_ref):
    z = _dot(h_ref[...], w_ref[...])
    o_ref[...] = (_rms(z) * g_ref[...]).astype(o_ref.dtype)


def _proj_ckv(h, w, g, tm=1024):
    n, d = h.shape
    return pl.pallas_call(
        _ckv_kernel,
        grid=(n // tm,),
        in_specs=[
            pl.BlockSpec((tm, d), lambda i: (i, 0)),
            pl.BlockSpec((d, A_KV_RANK), lambda i: (0, 0)),
            pl.BlockSpec((1, A_KV_RANK), lambda i: (0, 0)),
        ],
        out_specs=pl.BlockSpec((tm, A_KV_RANK), lambda i: (i, 0)),
        out_shape=jax.ShapeDtypeStruct((n, A_KV_RANK), BF16),
        compiler_params=_params("arbitrary"),
        name="proj_latent_kv",
    )(h, w, g.reshape(1, A_KV_RANK))


def _rope(z, cos_t, sin_lo, sin_hi):
    half = IDX_ROPE_DIM // 2
    return (z * cos_t + pltpu.roll(z, half, 1) * sin_hi
            + pltpu.roll(z, LANES - half, 1) * sin_lo)


def _qi_kernel(h_ref, w_ref, cos_ref, slo_ref, shi_ref, o_ref):
    z = _dot(h_ref[...], w_ref[...])
    cos_t, slo, shi = cos_ref[...], slo_ref[...], shi_ref[...]
    for s in range(IDX_Q_COLS // LANES):
        zs = z[:, s * LANES:(s + 1) * LANES]
        o_ref[:, s * LANES:(s + 1) * LANES] = _rope(zs, cos_t, slo, shi).astype(o_ref.dtype)


def _proj_qi(h, w, tables, seq, tm=512):
    n, d = h.shape
    per = seq // tm
    tspec = pl.BlockSpec((tm, LANES), lambda i: (i % per, 0))
    return pl.pallas_call(
        _qi_kernel,
        grid=(n // tm,),
        in_specs=[
            pl.BlockSpec((tm, d), lambda i: (i, 0)),
            pl.BlockSpec((d, IDX_Q_COLS), lambda i: (0, 0)),
            tspec, tspec, tspec,
        ],
        out_specs=pl.BlockSpec((tm, IDX_Q_COLS), lambda i: (i, 0)),
        out_shape=jax.ShapeDtypeStruct((n, IDX_Q_COLS), BF16),
        compiler_params=_params("arbitrary"),
        name="proj_index_q",
    )(h, w, *tables)


def _kw_kernel(h_ref, w_ref, g_ref, b_ref, cos_ref, slo_ref, shi_ref, k_ref, wi_ref):
    z = _dot(h_ref[...], w_ref[...])
    zk = z[:, :LANES]
    mu = jnp.mean(zk, axis=-1, keepdims=True)
    var = jnp.mean(jnp.square(zk - mu), axis=-1, keepdims=True)
    kn = (zk - mu) * lax.rsqrt(var + NORM_EPS) * g_ref[...] + b_ref[...]
    k_ref[...] = _rope(kn, cos_ref[...], slo_ref[...], shi_ref[...]).astype(k_ref.dtype)
    wi_ref[...] = z[:, LANES:] * (IDX_HEADS ** -0.5 * IDX_HEAD_DIM ** -0.5)


def _proj_kw(h, w, g2, b2, tables, seq, tm=1024):
    n, d = h.shape
    per = seq // tm
    tspec = pl.BlockSpec((tm, LANES), lambda i: (i % per, 0))
    vspec = pl.BlockSpec((1, LANES), lambda i: (0, 0))
    ospec = pl.BlockSpec((tm, LANES), lambda i: (i, 0))
    return pl.pallas_call(
        _kw_kernel,
        grid=(n // tm,),
        in_specs=[
            pl.BlockSpec((tm, d), lambda i: (i, 0)),
            pl.BlockSpec((d, 2 * LANES), lambda i: (0, 0)),
            vspec, vspec, tspec, tspec, tspec,
        ],
        out_specs=[ospec, ospec],
        out_shape=[jax.ShapeDtypeStruct((n, LANES), BF16), jax.ShapeDtypeStruct((n, LANES), F32)],
        compiler_params=_params("arbitrary"),
        name="proj_index_kw",
    )(h, w, g2, b2, *tables)


def _mm_kernel(a_ref, w_ref, o_ref):
    o_ref[...] = _dot(a_ref[...], w_ref[...]).astype(o_ref.dtype)


def _matmul(a, w, out_dtype, tm, tn, name):
    n, k = a.shape
    cols = w.shape[1]
    return pl.pallas_call(
        _mm_kernel,
        grid=(n // tm, cols // tn),
        in_specs=[pl.BlockSpec((tm, k), lambda i, j: (i, 0)), pl.BlockSpec((k, tn), lambda i, j: (0, j))],
        out_specs=pl.BlockSpec((tm, tn), lambda i, j: (i, j)),
        out_shape=jax.ShapeDtypeStruct((n, cols), out_dtype),
        compiler_params=_params("arbitrary", "arbitrary"),
        name=name,
    )(a, w)


def _dsa_kernel(qi_ref, wi_ref, k2_ref, ql_ref, ckv_ref, ba_ref, wuv_ref, o_ref,
                key_scr, wb_scr, qm_scr, p_scr, acc_scr, m_scr, l_scr, *, tq, tk, topk, n_delta):
    i = pl.program_id(1)
    q0 = i * tq
    nck = (q0 + tq + tk - 1) // tk
    ngrp = tk // LANES
    nsub = tq // BAND_BLOCK
    row_pos = q0 + lax.broadcasted_iota(jnp.int32, (tq, tk), 0)
    col = lax.broadcasted_iota(jnp.int32, (tq, tk), 1)
    lane = lax.broadcasted_iota(jnp.int32, (tq, LANES), 1)

    wi = wi_ref[...]
    for h in range(IDX_HEADS):
        wb_scr[h] = jnp.broadcast_to(wi[:, h:h + 1], (tq, LANES))
        qs = qi_ref[:, (h // 2) * LANES:(h // 2 + 1) * LANES].astype(F32)
        keep = (lane >= IDX_HEAD_DIM) if h % 2 else (lane < IDX_HEAD_DIM)
        qm_scr[h] = jnp.where(keep, qs, 0.0).astype(BF16)

    def idx_body(c, carry):
        k2 = k2_ref[pl.ds(pl.multiple_of(c * tk, tk), tk), :]
        score = jnp.zeros((tq, tk), F32)
        for h in range(IDX_HEADS):
            logits = _dot_nt(qm_scr[h], k2)
            wb = wb_scr[h]
            wfull = wb if ngrp == 1 else jnp.concatenate([wb] * ngrp, axis=1)
            score = score + wfull * jnp.maximum(logits, 0.0)
        score = jnp.where(c * tk + col <= row_pos, score, -jnp.inf)
        bits = pltpu.bitcast(score, jnp.int32)
        key_scr[c] = bits ^ ((bits >> 31) & 0x7FFFFFFF)
        return carry

    lax.fori_loop(0, nck, idx_body, 0)

    def bit_body(b, tx):
        cand_x = tx | jnp.left_shift(jnp.int32(1), 31 - b)
        cand = jnp.broadcast_to(cand_x ^ INT_MIN, (tq, LANES))

        def cnt_body(c, part):
            blk = key_scr[c]
            for g in range(ngrp):
                part = part + (blk[:, g * LANES:(g + 1) * LANES] >= cand).astype(jnp.int32)
            return part

        part = lax.fori_loop(0, nck, cnt_body, jnp.zeros((tq, LANES), jnp.int32))
        cnt = jnp.sum(part.astype(F32), axis=1, keepdims=True)
        return jnp.where(cnt >= topk, cand_x, tx)

    tx = lax.fori_loop(0, 32, bit_body, jnp.zeros((tq, 1), jnp.int32))
    thr = jnp.broadcast_to(tx ^ INT_MIN, (tq, LANES))
    thr_full = thr if ngrp == 1 else jnp.concatenate([thr] * ngrp, axis=1)

    m_scr[...] = jnp.full(m_scr.shape, NEG_INF, F32)
    l_scr[...] = jnp.zeros(l_scr.shape, F32)
    acc_scr[...] = jnp.zeros(acc_scr.shape, F32)

    def att_body(c, carry):
        k0 = pl.multiple_of(c * tk, tk)
        kv = ckv_ref[pl.ds(k0, tk), :]
        sel = (key_scr[c] >= thr_full) & (c * tk + col <= row_pos)
        madd = jnp.where(sel, 0.0, NEG_INF)
        q_all = ql_ref[...].reshape(A_HEADS * tq, A_KV_RANK)
        s_all = _dot_nt(q_all, kv)
        base_delta = (q0 - c * tk) // BAND_BLOCK
        for h in range(A_HEADS):
            rows = []
            for a in range(nsub):
                tiles = []
                for j in range(ngrp):
                    delta = jnp.clip(base_delta + (a - j), 0, n_delta - 1)
                    tiles.append(ba_ref[h, delta])
                rows.append(tiles[0] if ngrp == 1 else jnp.concatenate(tiles, axis=1))
            bias = rows[0] if nsub == 1 else jnp.concatenate(rows, axis=0)
            s = s_all[h * tq:(h + 1) * tq] + bias + madd
            m_prev = m_scr[h]
            m_new = jnp.maximum(m_prev, jnp.max(s, axis=1, keepdims=True))
            alpha = jnp.exp(m_prev - m_new)
            p = jnp.exp(s - m_new)
            l_scr[h] = alpha * l_scr[h] + jnp.sum(p, axis=1, keepdims=True)
            m_scr[h] = m_new
            acc_scr[h] = acc_scr[h] * alpha
            p_scr[h * tq:(h + 1) * tq, :] = p.astype(BF16)
        pv = _dot(p_scr[...], kv)
        acc_scr[...] += pv.reshape(A_HEADS, tq, A_KV_RANK)
        return carry

    lax.fori_loop(0, nck, att_body, 0)

    for h in range(A_HEADS):
        o_lat = (acc_scr[h] / l_scr[h]).astype(BF16)
        o_ref[:, h * A_HEAD_DIM:(h + 1) * A_HEAD_DIM] = _dot(o_lat, wuv_ref[h]).astype(o_ref.dtype)


def _dsa_attention(qi, wi, k2, ql, ckv, bias_a, wuv, batch, seq, tq=256):
    tk = tq
    n = batch * seq
    nq = seq // tq
    topk = min(DSA_TOPK, seq // 4)
    assert topk <= tq and seq % tq == 0 and tq % BAND_BLOCK == 0
    n_delta = bias_a.shape[1]
    kern = functools.partial(_dsa_kernel, tq=tq, tk=tk, topk=topk, n_delta=n_delta)
    return pl.pallas_call(
        kern,
        grid=(batch, nq),
        in_specs=[
            pl.BlockSpec((tq, IDX_Q_COLS), lambda b, i: (b * nq + i, 0)),
            pl.BlockSpec((tq, LANES), lambda b, i: (b * nq + i, 0)),
            pl.BlockSpec((seq, LANES), lambda b, i: (b, 0)),
            pl.BlockSpec((A_HEADS, tq, A_KV_RANK), lambda b, i: (0, b * nq + i, 0)),
            pl.BlockSpec((seq, A_KV_RANK), lambda b, i: (b, 0)),
            pl.BlockSpec(bias_a.shape, lambda b, i: (0, 0, 0, 0)),
            pl.BlockSpec(wuv.shape, lambda b, i: (0, 0, 0)),
        ],
        out_specs=pl.BlockSpec((tq, A_OUT), lambda b, i: (b * nq + i, 0)),
        out_shape=jax.ShapeDtypeStruct((n, A_OUT), BF16),
        scratch_shapes=[
            pltpu.VMEM((seq // tk, tq, tk), jnp.int32),
            pltpu.VMEM((IDX_HEADS, tq, LANES), F32),
            pltpu.VMEM((IDX_HEADS, tq, LANES), BF16),
            pltpu.VMEM((A_HEADS * tq, tk), BF16),
            pltpu.VMEM((A_HEADS, tq, A_KV_RANK), F32),
            pltpu.VMEM((A_HEADS, tq, 1), F32),
            pltpu.VMEM((A_HEADS, tq, 1), F32),
        ],
        compiler_params=_params("arbitrary", "arbitrary"),
        name="dsa_attention",
    )(qi, wi, k2, ql, ckv, bias_a, wuv)


def _dil_kernel(q_ref, kc_ref, kp_ref, vc_ref, vp_ref, bias_ref, o_ref, lse_ref, *, nsub, steps):
    first = pl.program_id(2) == 0
    P = BAND_BLOCK
    dh = B_HEAD_DIM
    row = lax.broadcasted_iota(jnp.int32, (P, 2 * P), 0)
    u = lax.broadcasted_iota(jnp.int32, (P, 2 * P), 1)
    back = row + P - u
    band = (back >= 0) & (back <= steps)
    band_first = band & ((u >= P) | jnp.logical_not(first))
    for a in range(nsub):
        valid = band_first if a == 0 else band
        for j in range(B_HEADS_PER_GROUP):
            cs = slice(j * dh, (j + 1) * dh)
            rs = slice(a * P, (a + 1) * P)
            if a == 0:
                k_prev, v_prev = kp_ref[0, :, cs], vp_ref[0, :, cs]
            else:
                ps = slice((a - 1) * P, a * P)
                k_prev, v_prev = kc_ref[0, ps, cs], vc_ref[0, ps, cs]
            k_cat = jnp.concatenate([k_prev, kc_ref[0, rs, cs]], axis=0)
            v_cat = jnp.concatenate([v_prev, vc_ref[0, rs, cs]], axis=0)
            s = _dot_nt(q_ref[0, rs, cs], k_cat) * (dh ** -0.5) + bias_ref[0, j]
            s = jnp.where(valid, s, NEG_INF)
            mx = jnp.max(s, axis=-1, keepdims=True)
            p = jnp.exp(s - mx)
            den = jnp.sum(p, axis=-1, keepdims=True)
            o_ref[0, rs, cs] = _dot((p / den).astype(BF16), v_cat)
            lse_ref[0, rs, cs] = jnp.broadcast_to(mx + jnp.log(den), (P, dh))


def _dilated_group(qkv, bias_b, g, batch, seq):
    window, dil = B_GROUPS[g]
    steps = window // dil
    assert steps <= BAND_BLOCK
    m = seq // dil
    rows = min(4 * BAND_BLOCK, m)
    assert m % rows == 0 and rows % BAND_BLOCK == 0
    nsub = rows // BAND_BLOCK
    width = B_OUT
    ng = len(B_GROUPS)
    per_tok = B_QKV_COLS // width
    x3 = qkv.reshape(batch, m, dil * B_QKV_COLS)

    def cur(which):
        return pl.BlockSpec((1, rows, width), lambda b, r, i: (b, i, r * per_tok + which * ng + g))

    def prev(which):
        return pl.BlockSpec((1, BAND_BLOCK, width),
                            lambda b, r, i: (b, jnp.maximum(i * nsub - 1, 0), r * per_tok + which * ng + g))

    ospec = pl.BlockSpec((1, rows, width), lambda b, r, i: (b, i, r))
    oshape = jax.ShapeDtypeStruct((batch, m, dil * width), F32)
    o, lse = pl.pallas_call(
        functools.partial(_dil_kernel, nsub=nsub, steps=steps),
        grid=(batch, dil, m // rows),
        in_specs=[cur(0), cur(1), prev(1), cur(2), prev(2),
                  pl.BlockSpec((1, B_HEADS_PER_GROUP, BAND_BLOCK, 2 * BAND_BLOCK),
                               lambda b, r, i: (g, 0, 0, 0))],
        out_specs=[ospec, ospec],
        out_shape=[oshape, oshape],
        compiler_params=_params("arbitrary", "arbitrary", "arbitrary"),
        name=f"dilated_attention_g{g}",
    )(x3, x3, x3, x3, x3, bias_b)
    return o.reshape(batch * seq, width), lse.reshape(batch * seq, width)


def _dil_merge_kernel(o0, o1, o2, l0, l1, l2, y_ref):
    a0, a1, a2 = l0[...], l1[...], l2[...]
    mx = jnp.maximum(jnp.maximum(a0, a1), a2)
    e0, e1, e2 = jnp.exp(a0 - mx), jnp.exp(a1 - mx), jnp.exp(a2 - mx)
    den = e0 + e1 + e2
    y = (e0 / den) * o0[...] + (e1 / den) * o1[...] + (e2 / den) * o2[...]
    y_ref[...] = y.astype(y_ref.dtype)


def _dilated_merge(outs, lses, tm=1024):
    n, w = outs[0].shape
    spec = pl.BlockSpec((tm, w), lambda i: (i, 0))
    return pl.pallas_call(
        _dil_merge_kernel,
        grid=(n // tm,),
        in_specs=[spec] * 6,
        out_specs=spec,
        out_shape=jax.ShapeDtypeStruct((n, w), BF16),
        compiler_params=_params("arbitrary"),
        name="dilated_merge",
    )(*outs, *lses)


def _merge_kernel(h_ref, ya_ref, yb_ref, wga_ref, wgb_ref, wa_ref, wb_ref, o_ref):
    h = h_ref[...]
    ga = jax.nn.sigmoid(_dot(h, wga_ref[...]))
    gb = jax.nn.sigmoid(_dot(h, wgb_ref[...]))
    merged = ga * _dot(ya_ref[...], wa_ref[...]) + gb * _dot(yb_ref[...], wb_ref[...])
    o_ref[...] = merged.astype(o_ref.dtype)


def _gated_merge(h, ya, yb, wga, wgb, wa, wb, tm=512, tn=512):
    n, d = h.shape

    def rows(k):
        return pl.BlockSpec((tm, k), lambda i, j: (i, 0))

    def cols(k):
        return pl.BlockSpec((k, tn), lambda i, j: (0, j))

    return pl.pallas_call(
        _merge_kernel,
        grid=(n // tm, d // tn),
        in_specs=[rows(d), rows(A_OUT), rows(B_OUT), cols(d), cols(d), cols(A_OUT), cols(B_OUT)],
        out_specs=pl.BlockSpec((tm, tn), lambda i, j: (i, j)),
        out_shape=jax.ShapeDtypeStruct((n, d), BF16),
        compiler_params=_params("arbitrary", "arbitrary"),
        name="gated_merge",
    )(h, ya, yb, wga, wgb, wa, wb)


def _mm_res_kernel(a_ref, w_ref, x_ref, g_ref, o_ref):
    o_ref[...] = x_ref[...] + g_ref[0] * _dot(a_ref[...], w_ref[...])


def _matmul_residual(a, w, x, gate, seq, tm, tn, name):
    n, k = a.shape
    d = w.shape[1]
    per = seq // tm
    return pl.pallas_call(
        _mm_res_kernel,
        grid=(n // tm, d // tn),
        in_specs=[
            pl.BlockSpec((tm, k), lambda i, j: (i, 0)),
            pl.BlockSpec((k, tn), lambda i, j: (0, j)),
            pl.BlockSpec((tm, tn), lambda i, j: (i, j)),
            pl.BlockSpec((1, 1, tn), lambda i, j: (i // per, 0, j)),
        ],
        out_specs=pl.BlockSpec((tm, tn), lambda i, j: (i, j)),
        out_shape=jax.ShapeDtypeStruct((n, d), F32),
        compiler_params=_params("arbitrary", "arbitrary"),
        name=name,
    )(a, w, x, gate)


def _ffn_up_kernel(h_ref, wg_ref, wu_ref, o_ref):
    h = h_ref[...]
    a = _dot(h, wg_ref[...])
    o_ref[...] = (a * jax.nn.sigmoid(a) * _dot(h, wu_ref[...])).astype(o_ref.dtype)


def _ffn_up(h, wg, wu, tm=1024, tn=512):
    n, d = h.shape
    f = wg.shape[1]
    wspec = pl.BlockSpec((d, tn), lambda i, j: (0, j))
    return pl.pallas_call(
        _ffn_up_kernel,
        grid=(n // tm, f // tn),
        in_specs=[pl.BlockSpec((tm, d), lambda i, j: (i, 0)), wspec, wspec],
        out_specs=pl.BlockSpec((tm, tn), lambda i, j: (i, j)),
        out_shape=jax.ShapeDtypeStruct((n, f), BF16),
        compiler_params=_params("arbitrary", "arbitrary"),
        name="ffn_up",
    )(h, wg, wu)


def _rope_tables(seq):
    half = IDX_ROPE_DIM // 2
    freqs = ROPE_THETA ** (-jnp.arange(half, dtype=F32) / half)
    ang = jnp.arange(seq).astype(F32)[:, None] * freqs[None, :]
    cos, sin = jnp.cos(ang), jnp.sin(ang)
    rest = IDX_HEAD_DIM - IDX_ROPE_DIM
    one = jnp.ones((seq, rest), F32)
    zr = jnp.zeros((seq, rest), F32)
    zh = jnp.zeros((seq, half), F32)
    cos_t = jnp.concatenate([cos, cos, one], axis=1)
    sin_lo = jnp.concatenate([-sin, zh, zr], axis=1)
    sin_hi = jnp.concatenate([zh, sin, zr], axis=1)
    rep = LANES // IDX_HEAD_DIM
    return tuple(jnp.tile(t, (1, rep)) for t in (cos_t, sin_lo, sin_hi))


def _layer(x, h_mod, tables, bias_a, bias_b, batch, seq, w_in, kv_norm_g, idx_ln_g, idx_ln_b,
           w_uk, w_uv, w_a_up, w_b_up, w_out, norm1_g, norm2_g, w_ff_gate, w_ff_up, w_ff_down):
    shift1, scale1, gate1, shift2, scale2, gate2 = h_mod
    offs = [0]
    for s in IN_SIZES:
        offs.append(offs[-1] + s)
    seg = [w_in[:, offs[k]:offs[k + 1]] for k in range(len(IN_SIZES))]
    w_qa, w_kv, w_qi, w_ki, w_wi, w_qkvb, w_gate = seg
    d = w_in.shape[0]
    rep = LANES // IDX_HEAD_DIM
    w_kw = jnp.concatenate(
        [w_ki] * rep + [w_wi, jnp.zeros((d, LANES - IDX_HEADS), w_in.dtype)], axis=1).astype(BF16)
    ln_g2 = jnp.tile(idx_ln_g, rep).reshape(1, LANES)
    ln_b2 = jnp.tile(idx_ln_b, rep).reshape(1, LANES)

    h = _norm_mod(x, norm1_g, scale1, shift1, seq)
    ql = _proj_qlat(h, w_qa.astype(BF16), w_uk.astype(BF16))
    ckv = _proj_ckv(h, w_kv.astype(BF16), kv_norm_g)
    qi = _proj_qi(h, w_qi.astype(BF16), tables, seq)
    k2, wi = _proj_kw(h, w_kw, ln_g2, ln_b2, tables, seq)
    qkvb = _matmul(h, w_qkvb.astype(BF16), BF16, 1024, 768, "proj_dilated_qkv")

    ya = _dsa_attention(qi, wi, k2, ql, ckv, bias_a, w_uv.astype(BF16), batch, seq)
    outs, lses = zip(*[_dilated_group(qkvb, bias_b, g, batch, seq) for g in range(len(B_GROUPS))])
    yb = _dilated_merge(outs, lses)

    merged = _gated_merge(h, ya, yb, w_gate[:, :d].astype(BF16), w_gate[:, d:].astype(BF16),
                          w_a_up.astype(BF16), w_b_up.astype(BF16))
    x = _matmul_residual(merged, w_out.astype(BF16), x, gate1, seq, 1024, 512, "out_proj_residual")

    h2 = _norm_mod(x, norm2_g, scale2, shift2, seq)
    act = _ffn_up(h2, w_ff_gate.astype(BF16), w_ff_up.astype(BF16))
    return _matmul_residual(act, w_ff_down.astype(BF16), x, gate2, seq, 512, 512, "ffn_down_residual")


def kernel(x, c, rel_bias, w_ada, b_ada, norm1_g, w_in, kv_norm_g, idx_ln_g, idx_ln_b, w_uk, w_uv,
           w_a_up, w_b_up, w_out, norm2_g, w_ff_gate, w_ff_up, w_ff_down, final_g):
    batch, seq, d = x.shape
    depth = w_ada.shape[0]
    assert d == D_MODEL and seq % (B_GROUPS[-1][1] * BAND_BLOCK) == 0
    n = batch * seq
    rows = -(-batch // SUBLANES) * SUBLANES
    mod = _modulation(jnp.pad(c, ((0, rows - batch), (0, 0))), w_ada, b_ada)
    mod = mod[:, :batch].reshape(depth, batch, 6, 1, d)
    bias_a = _bias_a_table(rel_bias, min(seq // BAND_BLOCK, FAR_DELTA + 1))
    bias_b = _bias_b_table(rel_bias)
    tables = _rope_tables(seq)
    xf = x.reshape(n, d)
    for l in range(depth):
        h_mod = [mod[l, :, k] for k in range(6)]
        xf = _layer(xf, h_mod, tables, bias_a, bias_b, batch, seq, w_in[l], kv_norm_g[l], idx_ln_g[l],
                    idx_ln_b[l], w_uk[l], w_uv[l], w_a_up[l], w_b_up[l], w_out[l], norm1_g[l],
                    norm2_g[l], w_ff_gate[l], w_ff_up[l], w_ff_down[l])
    return _final_norm(xf, final_g).reshape(batch, seq, d)
```

```python
import functools
import math

import jax
import jax.numpy as jnp
from jax import lax
from jax.experimental import pallas as pl
from jax.experimental.pallas import tpu as pltpu

D_MODEL = 2048
A_HEADS = 8
A_HEAD_DIM = 128
A_KV_RANK = 256
IDX_HEADS = 16
IDX_HEAD_DIM = 64
IDX_ROPE_DIM = 32
ROPE_THETA = 10000.0
DSA_TOPK = 256
B_GROUPS = ((128, 1), (512, 4), (2048, 16))
B_HEADS_PER_GROUP = 4
B_HEAD_DIM = 128
B_HEADS = B_HEADS_PER_GROUP * len(B_GROUPS)
BAND_BLOCK = 128
REL_BUCKETS = 32
REL_MAX_DISTANCE = 2048
N_BIAS_HEADS = A_HEADS + B_HEADS
D_FF = -(-8 * D_MODEL // (3 * 256)) * 256
NORM_EPS = 1e-6
NEG_INF = -1e30

A_Q_COLS = A_HEADS * A_HEAD_DIM
IDX_Q_COLS = IDX_HEADS * IDX_HEAD_DIM
B_QKV_COLS = 3 * B_HEADS * B_HEAD_DIM
GATE_COLS = 2 * D_MODEL
IN_SIZES = (A_Q_COLS, A_KV_RANK, IDX_Q_COLS, IDX_HEAD_DIM, IDX_HEADS, B_QKV_COLS, GATE_COLS)
A_OUT = A_HEADS * A_HEAD_DIM
B_OUT = B_HEADS_PER_GROUP * B_HEAD_DIM

LANES = 128
SUBLANES = 8
VMEM_LIMIT_BYTES = 56 * 1024 * 1024

BF16 = jnp.bfloat16
F32 = jnp.float32
INT_MIN = -2 ** 31

FAR_DELTA = -(-(REL_MAX_DISTANCE + BAND_BLOCK - 1) // BAND_BLOCK)
DSA_TILE = 256


def _dot(a, b):
    return jnp.dot(a, b, preferred_element_type=F32)


def _dot_nt(a, b):
    return lax.dot_general(a, b, (((1,), (1,)), ((), ())), preferred_element_type=F32)


def _params(*sem):
    return pltpu.CompilerParams(dimension_semantics=sem, vmem_limit_bytes=VMEM_LIMIT_BYTES)


def _mod_kernel(c_ref, w_ref, b_ref, o_ref):
    c = c_ref[...]
    a = (c * jax.nn.sigmoid(c)).astype(BF16)
    o_ref[0] = _dot(a, w_ref[0].astype(BF16)) + b_ref[0]


def _modulation(c_pad, w_ada, b_ada, tn=1024):
    depth, d, n6 = w_ada.shape
    rows = c_pad.shape[0]
    return pl.pallas_call(
        _mod_kernel,
        grid=(depth, n6 // tn),
        in_specs=[
            pl.BlockSpec((rows, d), lambda l, j: (0, 0)),
            pl.BlockSpec((1, d, tn), lambda l, j: (l, 0, j)),
            pl.BlockSpec((1, 1, tn), lambda l, j: (l, 0, j)),
        ],
        out_specs=pl.BlockSpec((1, rows, tn), lambda l, j: (l, 0, j)),
        out_shape=jax.ShapeDtypeStruct((depth, rows, n6), F32),
        compiler_params=_params("arbitrary", "arbitrary"),
        name="adaln_modulation",
    )(c_pad, w_ada, b_ada.reshape(depth, 1, n6))


def _bucket(dist):
    n = jnp.maximum(dist, 0)
    exact = REL_BUCKETS // 2
    nf = jnp.maximum(n, 1).astype(F32)
    large = exact + (jnp.log(nf / exact) / math.log(REL_MAX_DISTANCE / exact)
                     * (REL_BUCKETS - exact)).astype(jnp.int32)
    return jnp.where(n < exact, n, jnp.minimum(large, REL_BUCKETS - 1))


def _lookup(rb_ref, bucket, head):
    t = jnp.zeros(bucket.shape, F32)
    for k in range(REL_BUCKETS):
        t = jnp.where(bucket == k, rb_ref[k, head], t)
    return t


def _bias_a_kernel(rb_ref, o_ref):
    delta = pl.program_id(0)
    row = lax.broadcasted_iota(jnp.int32, (BAND_BLOCK, BAND_BLOCK), 0)
    col = lax.broadcasted_iota(jnp.int32, (BAND_BLOCK, BAND_BLOCK), 1)
    bucket = _bucket(delta * BAND_BLOCK + col - row)
    for h in range(A_HEADS):
        o_ref[h, 0] = _lookup(rb_ref, bucket, h)


def _bias_a_table(rel_bias, n_delta):
    return pl.pallas_call(
        _bias_a_kernel,
        grid=(n_delta,),
        in_specs=[pl.BlockSpec(memory_space=pltpu.SMEM)],
        out_specs=pl.BlockSpec((A_HEADS, 1, BAND_BLOCK, BAND_BLOCK), lambda d: (0, d, 0, 0)),
        out_shape=jax.ShapeDtypeStruct((A_HEADS, n_delta, BAND_BLOCK, BAND_BLOCK), F32),
        compiler_params=_params("arbitrary"),
        name="rel_bias_table_a",
    )(rel_bias)


def _bias_b_kernel(rb_ref, o_ref):
    g = pl.program_id(0)
    dil = jnp.where(g == 0, B_GROUPS[0][1], jnp.where(g == 1, B_GROUPS[1][1], B_GROUPS[2][1]))
    row = lax.broadcasted_iota(jnp.int32, (BAND_BLOCK, 2 * BAND_BLOCK), 0)
    u = lax.broadcasted_iota(jnp.int32, (BAND_BLOCK, 2 * BAND_BLOCK), 1)
    bucket = _bucket((row + BAND_BLOCK - u) * dil)
    for j in range(B_HEADS_PER_GROUP):
        o_ref[0, j] = _lookup(rb_ref, bucket, A_HEADS + g * B_HEADS_PER_GROUP + j)


def _bias_b_table(rel_bias):
    ng = len(B_GROUPS)
    return pl.pallas_call(
        _bias_b_kernel,
        grid=(ng,),
        in_specs=[pl.BlockSpec(memory_space=pltpu.SMEM)],
        out_specs=pl.BlockSpec((1, B_HEADS_PER_GROUP, BAND_BLOCK, 2 * BAND_BLOCK),
                               lambda g: (g, 0, 0, 0)),
        out_shape=jax.ShapeDtypeStruct((ng, B_HEADS_PER_GROUP, BAND_BLOCK, 2 * BAND_BLOCK), F32),
        compiler_params=_params("arbitrary"),
        name="rel_bias_table_b",
    )(rel_bias)


def _rms(x):
    return x * lax.rsqrt(jnp.mean(x * x, axis=-1, keepdims=True) + NORM_EPS)


def _norm_mod_kernel(x_ref, g_ref, sc_ref, sh_ref, o_ref):
    y = _rms(x_ref[...]) * g_ref[...]
    o_ref[...] = (y * (1.0 + sc_ref[0]) + sh_ref[0]).astype(o_ref.dtype)


def _norm_mod(x, g, scale, shift, seq, tm=512):
    n, d = x.shape
    per = seq // tm
    return pl.pallas_call(
        _norm_mod_kernel,
        grid=(n // tm,),
        in_specs=[
            pl.BlockSpec((tm, d), lambda i: (i, 0)),
            pl.BlockSpec((1, d), lambda i: (0, 0)),
            pl.BlockSpec((1, 1, d), lambda i: (i // per, 0, 0)),
            pl.BlockSpec((1, 1, d), lambda i: (i // per, 0, 0)),
        ],
        out_specs=pl.BlockSpec((tm, d), lambda i: (i, 0)),
        out_shape=jax.ShapeDtypeStruct((n, d), BF16),
        compiler_params=_params("arbitrary"),
        name="norm_modulate",
    )(x, g.reshape(1, d), scale, shift)


def _final_norm_kernel(x_ref, g_ref, o_ref):
    o_ref[...] = _rms(x_ref[...]) * g_ref[...]


def _final_norm(x, g, tm=512):
    n, d = x.shape
    return pl.pallas_call(
        _final_norm_kernel,
        grid=(n // tm,),
        in_specs=[pl.BlockSpec((tm, d), lambda i: (i, 0)), pl.BlockSpec((1, d), lambda i: (0, 0))],
        out_specs=pl.BlockSpec((tm, d), lambda i: (i, 0)),
        out_shape=jax.ShapeDtypeStruct((n, d), F32),
        compiler_params=_params("arbitrary"),
        name="final_norm",
    )(x, g.reshape(1, d))


def _qlat_kernel(h_ref, w_ref, wuk_ref, o_ref):
    z = _dot(h_ref[...], w_ref[...])
    for hd in range(A_HEADS):
        zh = z[:, hd * A_HEAD_DIM:(hd + 1) * A_HEAD_DIM].astype(BF16)
        o_ref[hd] = (_dot(zh, wuk_ref[hd]) * (A_HEAD_DIM ** -0.5)).astype(o_ref.dtype)


def _proj_qlat(h, w, wuk, tm=512):
    n, d = h.shape
    return pl.pallas_call(
        _qlat_kernel,
        grid=(n // tm,),
        in_specs=[
            pl.BlockSpec((tm, d), lambda i: (i, 0)),
            pl.BlockSpec((d, A_Q_COLS), lambda i: (0, 0)),
            pl.BlockSpec((A_HEADS, A_HEAD_DIM, A_KV_RANK), lambda i: (0, 0, 0)),
        ],
        out_specs=pl.BlockSpec((A_HEADS, tm, A_KV_RANK), lambda i: (0, i, 0)),
        out_shape=jax.ShapeDtypeStruct((A_HEADS, n, A_KV_RANK), BF16),
        compiler_params=_params("arbitrary"),
        name="proj_q_latent",
    )(h, w, wuk)


def _ckv_kernel(h_ref, w_ref, g_ref, o_ref, ot_ref, *, tk):
    z = _dot(h_ref[...], w_ref[...])
    ckv = _rms(z) * g_ref[...]
    o_ref[...] = ckv.astype(o_ref.dtype)
    for s in range(ot_ref.shape[0]):
        ot_ref[s] = ckv[s * tk:(s + 1) * tk, :].T.astype(ot_ref.dtype)


def _proj_ckv(h, w, g, tk, tm=1024):
    n, d = h.shape
    return pl.pallas_call(
        functools.partial(_ckv_kernel, tk=tk),
        grid=(n // tm,),
        in_specs=[
            pl.BlockSpec((tm, d), lambda i: (i, 0)),
            pl.BlockSpec((d, A_KV_RANK), lambda i: (0, 0)),
            pl.BlockSpec((1, A_KV_RANK), lambda i: (0, 0)),
        ],
        out_specs=[pl.BlockSpec((tm, A_KV_RANK), lambda i: (i, 0)),
                   pl.BlockSpec((tm // tk, A_KV_RANK, tk), lambda i: (i, 0, 0))],
        out_shape=[jax.ShapeDtypeStruct((n, A_KV_RANK), BF16),
                   jax.ShapeDtypeStruct((n // tk, A_KV_RANK, tk), BF16)],
        compiler_params=_params("arbitrary"),
        name="proj_latent_kv",
    )(h, w, g.reshape(1, A_KV_RANK))


def _rope(z, cos_t, sin_lo, sin_hi):
    half = IDX_ROPE_DIM // 2
    return (z * cos_t + pltpu.roll(z, half, 1) * sin_hi
            + pltpu.roll(z, LANES - half, 1) * sin_lo)


def _qi_kernel(h_ref, w_ref, cos_ref, slo_ref, shi_ref, o_ref):
    z = _dot(h_ref[...], w_ref[...])
    cos_t, slo, shi = cos_ref[...], slo_ref[...], shi_ref[...]
    for s in range(IDX_Q_COLS // LANES):
        zs = z[:, s * LANES:(s + 1) * LANES]
        o_ref[:, s * LANES:(s + 1) * LANES] = _rope(zs, cos_t, slo, shi).astype(o_ref.dtype)


def _proj_qi(h, w, tables, seq, tm=512):
    n, d = h.shape
    per = seq // tm
    tspec = pl.BlockSpec((tm, LANES), lambda i: (i % per, 0))
    return pl.pallas_call(
        _qi_kernel,
        grid=(n // tm,),
        in_specs=[
            pl.BlockSpec((tm, d), lambda i: (i, 0)),
            pl.BlockSpec((d, IDX_Q_COLS), lambda i: (0, 0)),
            tspec, tspec, tspec,
        ],
        out_specs=pl.BlockSpec((tm, IDX_Q_COLS), lambda i: (i, 0)),
        out_shape=jax.ShapeDtypeStruct((n, IDX_Q_COLS), BF16),
        compiler_params=_params("arbitrary"),
        name="proj_index_q",
    )(h, w, *tables)


def _kw_kernel(h_ref, w_ref, g_ref, b_ref, cos_ref, slo_ref, shi_ref, k_ref, wi_ref):
    z = _dot(h_ref[...], w_ref[...])
    zk = z[:, :LANES]
    mu = jnp.mean(zk, axis=-1, keepdims=True)
    var = jnp.mean(jnp.square(zk - mu), axis=-1, keepdims=True)
    kn = (zk - mu) * lax.rsqrt(var + NORM_EPS) * g_ref[...] + b_ref[...]
    k_ref[...] = _rope(kn, cos_ref[...], slo_ref[...], shi_ref[...]).astype(k_ref.dtype)
    wi_ref[...] = (z[:, LANES:] * (IDX_HEADS ** -0.5 * IDX_HEAD_DIM ** -0.5)).T


def _proj_kw(h, w, g2, b2, tables, seq, tm=1024):
    n, d = h.shape
    per = seq // tm
    tspec = pl.BlockSpec((tm, LANES), lambda i: (i % per, 0))
    vspec = pl.BlockSpec((1, LANES), lambda i: (0, 0))
    ospec = pl.BlockSpec((tm, LANES), lambda i: (i, 0))
    return pl.pallas_call(
        _kw_kernel,
        grid=(n // tm,),
        in_specs=[
            pl.BlockSpec((tm, d), lambda i: (i, 0)),
            pl.BlockSpec((d, 2 * LANES), lambda i: (0, 0)),
            vspec, vspec, tspec, tspec, tspec,
        ],
        out_specs=[ospec, pl.BlockSpec((LANES, tm), lambda i: (0, i))],
        out_shape=[jax.ShapeDtypeStruct((n, LANES), BF16), jax.ShapeDtypeStruct((LANES, n), F32)],
        compiler_params=_params("arbitrary"),
        name="proj_index_kw",
    )(h, w, g2, b2, *tables)


def _mm_kernel(a_ref, w_ref, o_ref):
    o_ref[...] = _dot(a_ref[...], w_ref[...]).astype(o_ref.dtype)


def _matmul(a, w, out_dtype, tm, tn, name):
    n, k = a.shape
    cols = w.shape[1]
    return pl.pallas_call(
        _mm_kernel,
        grid=(n // tm, cols // tn),
        in_specs=[pl.BlockSpec((tm, k), lambda i, j: (i, 0)), pl.BlockSpec((k, tn), lambda i, j: (0, j))],
        out_specs=pl.BlockSpec((tm, tn), lambda i, j: (i, j)),
        out_shape=jax.ShapeDtypeStruct((n, cols), out_dtype),
        compiler_params=_params("arbitrary", "arbitrary"),
        name=name,
    )(a, w)


def _dsa_kernel(qi_ref, wi_ref, k2_ref, ql_ref, ckv_ref, ckvt_ref, ba_ref, wuvt_ref, o_ref,
                key_scr, qm_scr, acc_scr, m_scr, l_scr, *, tq, tk, topk, n_delta):
    i = pl.program_id(1)
    q0 = i * tq
    nck = (q0 + tq + tk - 1) // tk
    nsub_k = tk // BAND_BLOCK
    nsub_q = tq // BAND_BLOCK
    key_row = lax.broadcasted_iota(jnp.int32, (tk, tq), 0)
    q_pos = q0 + lax.broadcasted_iota(jnp.int32, (tk, tq), 1)
    lane = lax.broadcasted_iota(jnp.int32, (tq, LANES), 1)

    for h in range(IDX_HEADS):
        qs = qi_ref[:, (h // 2) * LANES:(h // 2 + 1) * LANES].astype(F32)
        keep = (lane >= IDX_HEAD_DIM) if h % 2 else (lane < IDX_HEAD_DIM)
        qm_scr[h] = jnp.where(keep, qs, 0.0).astype(BF16)

    def idx_body(c, carry):
        k2 = k2_ref[pl.ds(pl.multiple_of(c * tk, tk), tk), :]
        score = jnp.zeros((tk, tq), F32)
        for h in range(IDX_HEADS):
            logits = _dot_nt(k2, qm_scr[h])
            score = score + wi_ref[h:h + 1, :] * jnp.maximum(logits, 0.0)
        score = jnp.where(c * tk + key_row <= q_pos, score, -jnp.inf)
        bits = pltpu.bitcast(score, jnp.int32)
        key_scr[c] = bits ^ ((bits >> 31) & 0x7FFFFFFF)
        return carry

    lax.fori_loop(0, nck, idx_body, 0)

    def bit_body(b, tx):
        cand_x = tx | jnp.left_shift(jnp.int32(1), 31 - b)
        cand = cand_x ^ INT_MIN

        def cnt_body(c, part):
            ge = (key_scr[c] >= cand).astype(jnp.int32)
            return part + jnp.sum(ge.reshape(tk // SUBLANES, SUBLANES, tq), axis=0)

        part = lax.fori_loop(0, nck, cnt_body, jnp.zeros((SUBLANES, tq), jnp.int32))
        cnt = jnp.sum(part.astype(F32), axis=0, keepdims=True)
        return jnp.where(cnt >= topk, cand_x, tx)

    tx = lax.fori_loop(0, 32, bit_body, jnp.zeros((1, tq), jnp.int32))
    thr = tx ^ INT_MIN

    m_scr[...] = jnp.full(m_scr.shape, NEG_INF, F32)
    l_scr[...] = jnp.zeros(l_scr.shape, F32)
    acc_scr[...] = jnp.zeros(acc_scr.shape, F32)

    def att_body(c, carry):
        kv = ckv_ref[pl.ds(pl.multiple_of(c * tk, tk), tk), :]
        kvt = ckvt_ref[c]
        sel = (key_scr[c] >= thr) & (c * tk + key_row <= q_pos)
        madd = jnp.where(sel, 0.0, NEG_INF)
        base_delta = (q0 - c * tk) // BAND_BLOCK
        for h in range(A_HEADS):
            rows = []
            for j in range(nsub_k):
                tiles = []
                for a in range(nsub_q):
                    delta = jnp.clip(base_delta + (a - j), 0, n_delta - 1)
                    tiles.append(ba_ref[h, delta])
                rows.append(tiles[0] if nsub_q == 1 else jnp.concatenate(tiles, axis=1))
            bias = rows[0] if nsub_k == 1 else jnp.concatenate(rows, axis=0)
            s = _dot_nt(kv, ql_ref[h]) + bias + madd
            m_prev = m_scr[h:h + 1, :]
            m_new = jnp.maximum(m_prev, jnp.max(s, axis=0, keepdims=True))
            alpha = jnp.exp(m_prev - m_new)
            p = jnp.exp(s - m_new)
            l_scr[h:h + 1, :] = alpha * l_scr[h:h + 1, :] + jnp.sum(p, axis=0, keepdims=True)
            m_scr[h:h + 1, :] = m_new
            acc_scr[h] = acc_scr[h] * alpha + _dot(kvt, p.astype(BF16))
        return carry

    lax.fori_loop(0, nck, att_body, 0)

    for h in range(A_HEADS):
        o_lat_t = (acc_scr[h] / l_scr[h:h + 1, :]).astype(BF16)
        y_t = _dot(wuvt_ref[h], o_lat_t)
        o_ref[:, h * A_HEAD_DIM:(h + 1) * A_HEAD_DIM] = y_t.T.astype(o_ref.dtype)


def _dsa_attention(qi, wi_t, k2, ql, ckv, ckv_t, bias_a, wuv_t, batch, seq, tq):
    tk = tq
    n = batch * seq
    nq = seq // tq
    nck = seq // tk
    topk = min(DSA_TOPK, seq // 4)
    assert topk <= tk and seq % tq == 0 and tq % BAND_BLOCK == 0
    n_delta = bias_a.shape[1]
    kern = functools.partial(_dsa_kernel, tq=tq, tk=tk, topk=topk, n_delta=n_delta)
    return pl.pallas_call(
        kern,
        grid=(batch, nq),
        in_specs=[
            pl.BlockSpec((tq, IDX_Q_COLS), lambda b, i: (b * nq + i, 0)),
            pl.BlockSpec((LANES, tq), lambda b, i: (0, b * nq + i)),
            pl.BlockSpec((seq, LANES), lambda b, i: (b, 0)),
            pl.BlockSpec((A_HEADS, tq, A_KV_RANK), lambda b, i: (0, b * nq + i, 0)),
            pl.BlockSpec((seq, A_KV_RANK), lambda b, i: (b, 0)),
            pl.BlockSpec((nck, A_KV_RANK, tk), lambda b, i: (b, 0, 0)),
            pl.BlockSpec(bias_a.shape, lambda b, i: (0, 0, 0, 0)),
            pl.BlockSpec(wuv_t.shape, lambda b, i: (0, 0, 0)),
        ],
        out_specs=pl.BlockSpec((tq, A_OUT), lambda b, i: (b * nq + i, 0)),
        out_shape=jax.ShapeDtypeStruct((n, A_OUT), BF16),
        scratch_shapes=[
            pltpu.VMEM((nck, tk, tq), jnp.int32),
            pltpu.VMEM((IDX_HEADS, tq, LANES), BF16),
            pltpu.VMEM((A_HEADS, A_KV_RANK, tq), F32),
            pltpu.VMEM((A_HEADS, tq), F32),
            pltpu.VMEM((A_HEADS, tq), F32),
        ],
        compiler_params=_params("arbitrary", "arbitrary"),
        name="dsa_attention",
    )(qi, wi_t, k2, ql, ckv, ckv_t, bias_a, wuv_t)


def _dil_kernel(q_ref, kc_ref, kp_ref, vc_ref, vp_ref, bias_ref, o_ref, lse_ref, *, nsub, steps):
    first = pl.program_id(2) == 0
    P = BAND_BLOCK
    dh = B_HEAD_DIM
    row = lax.broadcasted_iota(jnp.int32, (P, 2 * P), 0)
    u = lax.broadcasted_iota(jnp.int32, (P, 2 * P), 1)
    back = row + P - u
    band = (back >= 0) & (back <= steps)
    band_first = band & ((u >= P) | jnp.logical_not(first))
    for a in range(nsub):
        valid = band_first if a == 0 else band
        for j in range(B_HEADS_PER_GROUP):
            cs = slice(j * dh, (j + 1) * dh)
            rs = slice(a * P, (a + 1) * P)
            if a == 0:
                k_prev, v_prev = kp_ref[0, :, cs], vp_ref[0, :, cs]
            else:
                ps = slice((a - 1) * P, a * P)
                k_prev, v_prev = kc_ref[0, ps, cs], vc_ref[0, ps, cs]
            k_cat = jnp.concatenate([k_prev, kc_ref[0, rs, cs]], axis=0)
            v_cat = jnp.concatenate([v_prev, vc_ref[0, rs, cs]], axis=0)
            s = _dot_nt(q_ref[0, rs, cs], k_cat) * (dh ** -0.5) + bias_ref[0, j]
            s = jnp.where(valid, s, NEG_INF)
            mx = jnp.max(s, axis=-1, keepdims=True)
            p = jnp.exp(s - mx)
            den = jnp.sum(p, axis=-1, keepdims=True)
            o_ref[0, rs, cs] = _dot((p / den).astype(BF16), v_cat)
            lse_ref[0, rs, cs] = jnp.broadcast_to(mx + jnp.log(den), (P, dh))


def _dilated_group(qkv, bias_b, g, batch, seq):
    window, dil = B_GROUPS[g]
    steps = window // dil
    assert steps <= BAND_BLOCK
    m = seq // dil
    rows = min(4 * BAND_BLOCK, m)
    assert m % rows == 0 and rows % BAND_BLOCK == 0
    nsub = rows // BAND_BLOCK
    width = B_OUT
    ng = len(B_GROUPS)
    per_tok = B_QKV_COLS // width
    x3 = qkv.reshape(batch, m, dil * B_QKV_COLS)

    def cur(which):
        return pl.BlockSpec((1, rows, width), lambda b, r, i: (b, i, r * per_tok + which * ng + g))

    def prev(which):
        return pl.BlockSpec((1, BAND_BLOCK, width),
                            lambda b, r, i: (b, jnp.maximum(i * nsub - 1, 0), r * per_tok + which * ng + g))

    ospec = pl.BlockSpec((1, rows, width), lambda b, r, i: (b, i, r))
    oshape = jax.ShapeDtypeStruct((batch, m, dil * width), F32)
    o, lse = pl.pallas_call(
        functools.partial(_dil_kernel, nsub=nsub, steps=steps),
        grid=(batch, dil, m // rows),
        in_specs=[cur(0), cur(1), prev(1), cur(2), prev(2),
                  pl.BlockSpec((1, B_HEADS_PER_GROUP, BAND_BLOCK, 2 * BAND_BLOCK),
                               lambda b, r, i: (g, 0, 0, 0))],
        out_specs=[ospec, ospec],
        out_shape=[oshape, oshape],
        compiler_params=_params("arbitrary", "arbitrary", "arbitrary"),
        name=f"dilated_attention_g{g}",
    )(x3, x3, x3, x3, x3, bias_b)
    return o.reshape(batch * seq, width), lse.reshape(batch * seq, width)


def _dil_merge_kernel(o0, o1, o2, l0, l1, l2, y_ref):
    a0, a1, a2 = l0[...], l1[...], l2[...]
    mx = jnp.maximum(jnp.maximum(a0, a1), a2)
    e0, e1, e2 = jnp.exp(a0 - mx), jnp.exp(a1 - mx), jnp.exp(a2 - mx)
    den = e0 + e1 + e2
    y = (e0 / den) * o0[...] + (e1 / den) * o1[...] + (e2 / den) * o2[...]
    y_ref[...] = y.astype(y_ref.dtype)


def _dilated_merge(outs, lses, tm=1024):
    n, w = outs[0].shape
    spec = pl.BlockSpec((tm, w), lambda i: (i, 0))
    return pl.pallas_call(
        _dil_merge_kernel,
        grid=(n // tm,),
        in_specs=[spec] * 6,
        out_specs=spec,
        out_shape=jax.ShapeDtypeStruct((n, w), BF16),
        compiler_params=_params("arbitrary"),
        name="dilated_merge",
    )(*outs, *lses)


def _merge_kernel(h_ref, ya_ref, yb_ref, wga_ref, wgb_ref, wa_ref, wb_ref, o_ref):
    h = h_ref[...]
    ga = jax.nn.sigmoid(_dot(h, wga_ref[...]))
    gb = jax.nn.sigmoid(_dot(h, wgb_ref[...]))
    merged = ga * _dot(ya_ref[...], wa_ref[...]) + gb * _dot(yb_ref[...], wb_ref[...])
    o_ref[...] = merged.astype(o_ref.dtype)


def _gated_merge(h, ya, yb, wga, wgb, wa, wb, tm=512, tn=512):
    n, d = h.shape

    def rows(k):
        return pl.BlockSpec((tm, k), lambda i, j: (i, 0))

    def cols(k):
        return pl.BlockSpec((k, tn), lambda i, j: (0, j))

    return pl.pallas_call(
        _merge_kernel,
        grid=(n // tm, d // tn),
        in_specs=[rows(d), rows(A_OUT), rows(B_OUT), cols(d), cols(d), cols(A_OUT), cols(B_OUT)],
        out_specs=pl.BlockSpec((tm, tn), lambda i, j: (i, j)),
        out_shape=jax.ShapeDtypeStruct((n, d), BF16),
        compiler_params=_params("arbitrary", "arbitrary"),
        name="gated_merge",
    )(h, ya, yb, wga, wgb, wa, wb)


def _mm_res_kernel(a_ref, w_ref, x_ref, g_ref, o_ref):
    o_ref[...] = x_ref[...] + g_ref[0] * _dot(a_ref[...], w_ref[...])


def _matmul_residual(a, w, x, gate, seq, tm, tn, name):
    n, k = a.shape
    d = w.shape[1]
    per = seq // tm
    return pl.pallas_call(
        _mm_res_kernel,
        grid=(n // tm, d // tn),
        in_specs=[
            pl.BlockSpec((tm, k), lambda i, j: (i, 0)),
            pl.BlockSpec((k, tn), lambda i, j: (0, j)),
            pl.BlockSpec((tm, tn), lambda i, j: (i, j)),
            pl.BlockSpec((1, 1, tn), lambda i, j: (i // per, 0, j)),
        ],
        out_specs=pl.BlockSpec((tm, tn), lambda i, j: (i, j)),
        out_shape=jax.ShapeDtypeStruct((n, d), F32),
        compiler_params=_params("arbitrary", "arbitrary"),
        name=name,
    )(a, w, x, gate)


def _ffn_up_kernel(h_ref, wg_ref, wu_ref, o_ref):
    h = h_ref[...]
    a = _dot(h, wg_ref[...])
    o_ref[...] = (a * jax.nn.sigmoid(a) * _dot(h, wu_ref[...])).astype(o_ref.dtype)


def _ffn_up(h, wg, wu, tm=1024, tn=512):
    n, d = h.shape
    f = wg.shape[1]
    wspec = pl.BlockSpec((d, tn), lambda i, j: (0, j))
    return pl.pallas_call(
        _ffn_up_kernel,
        grid=(n // tm, f // tn),
        in_specs=[pl.BlockSpec((tm, d), lambda i, j: (i, 0)), wspec, wspec],
        out_specs=pl.BlockSpec((tm, tn), lambda i, j: (i, j)),
        out_shape=jax.ShapeDtypeStruct((n, f), BF16),
        compiler_params=_params("arbitrary", "arbitrary"),
        name="ffn_up",
    )(h, wg, wu)


def _rope_tables(seq):
    half = IDX_ROPE_DIM // 2
    freqs = ROPE_THETA ** (-jnp.arange(half, dtype=F32) / half)
    ang = jnp.arange(seq).astype(F32)[:, None] * freqs[None, :]
    cos, sin = jnp.cos(ang), jnp.sin(ang)
    rest = IDX_HEAD_DIM - IDX_ROPE_DIM
    one = jnp.ones((seq, rest), F32)
    zr = jnp.zeros((seq, rest), F32)
    zh = jnp.zeros((seq, half), F32)
    cos_t = jnp.concatenate([cos, cos, one], axis=1)
    sin_lo = jnp.concatenate([-sin, zh, zr], axis=1)
    sin_hi = jnp.concatenate([zh, sin, zr], axis=1)
    rep = LANES // IDX_HEAD_DIM
    return tuple(jnp.tile(t, (1, rep)) for t in (cos_t, sin_lo, sin_hi))


def _layer(x, h_mod, tables, bias_a, bias_b, batch, seq, w_in, kv_norm_g, idx_ln_g, idx_ln_b,
           w_uk, w_uv, w_a_up, w_b_up, w_out, norm1_g, norm2_g, w_ff_gate, w_ff_up, w_ff_down):
    shift1, scale1, gate1, shift2, scale2, gate2 = h_mod
    offs = [0]
    for s in IN_SIZES:
        offs.append(offs[-1] + s)
    seg = [w_in[:, offs[k]:offs[k + 1]] for k in range(len(IN_SIZES))]
    w_qa, w_kv, w_qi, w_ki, w_wi, w_qkvb, w_gate = seg
    d = w_in.shape[0]
    rep = LANES // IDX_HEAD_DIM
    w_kw = jnp.concatenate(
        [w_ki] * rep + [w_wi, jnp.zeros((d, LANES - IDX_HEADS), w_in.dtype)], axis=1).astype(BF16)
    ln_g2 = jnp.tile(idx_ln_g, rep).reshape(1, LANES)
    ln_b2 = jnp.tile(idx_ln_b, rep).reshape(1, LANES)

    h = _norm_mod(x, norm1_g, scale1, shift1, seq)
    ql = _proj_qlat(h, w_qa.astype(BF16), w_uk.astype(BF16))
    ckv, ckv_t = _proj_ckv(h, w_kv.astype(BF16), kv_norm_g, DSA_TILE)
    qi = _proj_qi(h, w_qi.astype(BF16), tables, seq)
    k2, wi_t = _proj_kw(h, w_kw, ln_g2, ln_b2, tables, seq)
    qkvb = _matmul(h, w_qkvb.astype(BF16), BF16, 1024, 768, "proj_dilated_qkv")

    wuv_t = jnp.swapaxes(w_uv, 1, 2).astype(BF16)
    ya = _dsa_attention(qi, wi_t, k2, ql, ckv, ckv_t, bias_a, wuv_t, batch, seq, DSA_TILE)
    outs, lses = zip(*[_dilated_group(qkvb, bias_b, g, batch, seq) for g in range(len(B_GROUPS))])
    yb = _dilated_merge(outs, lses)

    merged = _gated_merge(h, ya, yb, w_gate[:, :d].astype(BF16), w_gate[:, d:].astype(BF16),
                          w_a_up.astype(BF16), w_b_up.astype(BF16))
    x = _matmul_residual(merged, w_out.astype(BF16), x, gate1, seq, 1024, 512, "out_proj_residual")

    h2 = _norm_mod(x, norm2_g, scale2, shift2, seq)
    act = _ffn_up(h2, w_ff_gate.astype(BF16), w_ff_up.astype(BF16))
    return _matmul_residual(act, w_ff_down.astype(BF16), x, gate2, seq, 512, 512, "ffn_down_residual")


def kernel(x, c, rel_bias, w_ada, b_ada, norm1_g, w_in, kv_norm_g, idx_ln_g, idx_ln_b, w_uk, w_uv,
           w_a_up, w_b_up, w_out, norm2_g, w_ff_gate, w_ff_up, w_ff_down, final_g):
    batch, seq, d = x.shape
    depth = w_ada.shape[0]
    assert d == D_MODEL and seq % (B_GROUPS[-1][1] * BAND_BLOCK) == 0
    n = batch * seq
    rows = -(-batch // SUBLANES) * SUBLANES
    mod = _modulation(jnp.pad(c, ((0, rows - batch), (0, 0))), w_ada, b_ada)
    mod = mod[:, :batch].reshape(depth, batch, 6, 1, d)
    bias_a = _bias_a_table(rel_bias, min(seq // BAND_BLOCK, FAR_DELTA + 1))
    bias_b = _bias_b_table(rel_bias)
    tables = _rope_tables(seq)
    xf = x.reshape(n, d)
    for l in range(depth):
        h_mod = [mod[l, :, k] for k in range(6)]
        xf = _layer(xf, h_mod, tables, bias_a, bias_b, batch, seq, w_in[l], kv_norm_g[l], idx_ln_g[l],
                    idx_ln_b[l], w_uk[l], w_uv[l], w_a_up[l], w_b_up[l], w_out[l], norm1_g[l],
                    norm2_g[l], w_ff_gate[l], w_ff_up[l], w_ff_down[l])
    return _final_norm(xf, final_g).reshape(batch, seq, d)
```

```python
import functools
import math

import jax
import jax.numpy as jnp
from jax import lax
from jax.experimental import pallas as pl
from jax.experimental.pallas import tpu as pltpu

D_MODEL = 2048
A_HEADS = 8
A_HEAD_DIM = 128
A_KV_RANK = 256
IDX_HEADS = 16
IDX_HEAD_DIM = 64
IDX_ROPE_DIM = 32
ROPE_THETA = 10000.0
DSA_TOPK = 256
B_GROUPS = ((128, 1), (512, 4), (2048, 16))
B_HEADS_PER_GROUP = 4
B_HEAD_DIM = 128
B_HEADS = B_HEADS_PER_GROUP * len(B_GROUPS)
BAND_BLOCK = 128
REL_BUCKETS = 32
REL_MAX_DISTANCE = 2048
N_BIAS_HEADS = A_HEADS + B_HEADS
D_FF = -(-8 * D_MODEL // (3 * 256)) * 256
NORM_EPS = 1e-6
NEG_INF = -1e30

A_Q_COLS = A_HEADS * A_HEAD_DIM
IDX_Q_COLS = IDX_HEADS * IDX_HEAD_DIM
B_QKV_COLS = 3 * B_HEADS * B_HEAD_DIM
GATE_COLS = 2 * D_MODEL
IN_SIZES = (A_Q_COLS, A_KV_RANK, IDX_Q_COLS, IDX_HEAD_DIM, IDX_HEADS, B_QKV_COLS, GATE_COLS)
A_OUT = A_HEADS * A_HEAD_DIM
B_OUT = B_HEADS_PER_GROUP * B_HEAD_DIM

LANES = 128
SUBLANES = 8
VMEM_LIMIT_BYTES = 56 * 1024 * 1024

BF16 = jnp.bfloat16
F32 = jnp.float32
INT_MIN = -2 ** 31

FAR_DELTA = -(-(REL_MAX_DISTANCE + BAND_BLOCK - 1) // BAND_BLOCK)
DSA_TILE = 512


def _dot(a, b):
    return jnp.dot(a, b, preferred_element_type=F32)


def _dot_nt(a, b):
    return lax.dot_general(a, b, (((1,), (1,)), ((), ())), preferred_element_type=F32)


def _params(*sem):
    return pltpu.CompilerParams(dimension_semantics=sem, vmem_limit_bytes=VMEM_LIMIT_BYTES)


def _mod_kernel(c_ref, w_ref, b_ref, o_ref):
    c = c_ref[...]
    a = (c * jax.nn.sigmoid(c)).astype(BF16)
    o_ref[0] = _dot(a, w_ref[0].astype(BF16)) + b_ref[0]


def _modulation(c_pad, w_ada, b_ada, tn=1024):
    depth, d, n6 = w_ada.shape
    rows = c_pad.shape[0]
    return pl.pallas_call(
        _mod_kernel,
        grid=(depth, n6 // tn),
        in_specs=[
            pl.BlockSpec((rows, d), lambda l, j: (0, 0)),
            pl.BlockSpec((1, d, tn), lambda l, j: (l, 0, j)),
            pl.BlockSpec((1, 1, tn), lambda l, j: (l, 0, j)),
        ],
        out_specs=pl.BlockSpec((1, rows, tn), lambda l, j: (l, 0, j)),
        out_shape=jax.ShapeDtypeStruct((depth, rows, n6), F32),
        compiler_params=_params("arbitrary", "arbitrary"),
        name="adaln_modulation",
    )(c_pad, w_ada, b_ada.reshape(depth, 1, n6))


def _bucket(dist):
    n = jnp.maximum(dist, 0)
    exact = REL_BUCKETS // 2
    nf = jnp.maximum(n, 1).astype(F32)
    large = exact + (jnp.log(nf / exact) / math.log(REL_MAX_DISTANCE / exact)
                     * (REL_BUCKETS - exact)).astype(jnp.int32)
    return jnp.where(n < exact, n, jnp.minimum(large, REL_BUCKETS - 1))


def _lookup(rb_ref, bucket, head):
    t = jnp.zeros(bucket.shape, F32)
    for k in range(REL_BUCKETS):
        t = jnp.where(bucket == k, rb_ref[k, head], t)
    return t


def _bias_a_kernel(rb_ref, o_ref):
    delta = pl.program_id(0)
    row = lax.broadcasted_iota(jnp.int32, (BAND_BLOCK, BAND_BLOCK), 0)
    col = lax.broadcasted_iota(jnp.int32, (BAND_BLOCK, BAND_BLOCK), 1)
    bucket = _bucket(delta * BAND_BLOCK + col - row)
    for h in range(A_HEADS):
        o_ref[h, 0] = _lookup(rb_ref, bucket, h)


def _bias_a_table(rel_bias, n_delta):
    return pl.pallas_call(
        _bias_a_kernel,
        grid=(n_delta,),
        in_specs=[pl.BlockSpec(memory_space=pltpu.SMEM)],
        out_specs=pl.BlockSpec((A_HEADS, 1, BAND_BLOCK, BAND_BLOCK), lambda d: (0, d, 0, 0)),
        out_shape=jax.ShapeDtypeStruct((A_HEADS, n_delta, BAND_BLOCK, BAND_BLOCK), F32),
        compiler_params=_params("arbitrary"),
        name="rel_bias_table_a",
    )(rel_bias)


def _bias_b_kernel(rb_ref, o_ref):
    g = pl.program_id(0)
    dil = jnp.where(g == 0, B_GROUPS[0][1], jnp.where(g == 1, B_GROUPS[1][1], B_GROUPS[2][1]))
    row = lax.broadcasted_iota(jnp.int32, (BAND_BLOCK, 2 * BAND_BLOCK), 0)
    u = lax.broadcasted_iota(jnp.int32, (BAND_BLOCK, 2 * BAND_BLOCK), 1)
    bucket = _bucket((row + BAND_BLOCK - u) * dil)
    for j in range(B_HEADS_PER_GROUP):
        o_ref[0, j] = _lookup(rb_ref, bucket, A_HEADS + g * B_HEADS_PER_GROUP + j)


def _bias_b_table(rel_bias):
    ng = len(B_GROUPS)
    return pl.pallas_call(
        _bias_b_kernel,
        grid=(ng,),
        in_specs=[pl.BlockSpec(memory_space=pltpu.SMEM)],
        out_specs=pl.BlockSpec((1, B_HEADS_PER_GROUP, BAND_BLOCK, 2 * BAND_BLOCK),
                               lambda g: (g, 0, 0, 0)),
        out_shape=jax.ShapeDtypeStruct((ng, B_HEADS_PER_GROUP, BAND_BLOCK, 2 * BAND_BLOCK), F32),
        compiler_params=_params("arbitrary"),
        name="rel_bias_table_b",
    )(rel_bias)


def _rms(x):
    return x * lax.rsqrt(jnp.mean(x * x, axis=-1, keepdims=True) + NORM_EPS)


def _norm_mod_kernel(x_ref, g_ref, sc_ref, sh_ref, o_ref):
    y = _rms(x_ref[...]) * g_ref[...]
    o_ref[...] = (y * (1.0 + sc_ref[0]) + sh_ref[0]).astype(o_ref.dtype)


def _norm_mod(x, g, scale, shift, seq, tm=512):
    n, d = x.shape
    per = seq // tm
    return pl.pallas_call(
        _norm_mod_kernel,
        grid=(n // tm,),
        in_specs=[
            pl.BlockSpec((tm, d), lambda i: (i, 0)),
            pl.BlockSpec((1, d), lambda i: (0, 0)),
            pl.BlockSpec((1, 1, d), lambda i: (i // per, 0, 0)),
            pl.BlockSpec((1, 1, d), lambda i: (i // per, 0, 0)),
        ],
        out_specs=pl.BlockSpec((tm, d), lambda i: (i, 0)),
        out_shape=jax.ShapeDtypeStruct((n, d), BF16),
        compiler_params=_params("arbitrary"),
        name="norm_modulate",
    )(x, g.reshape(1, d), scale, shift)


def _final_norm_kernel(x_ref, g_ref, o_ref):
    o_ref[...] = _rms(x_ref[...]) * g_ref[...]


def _final_norm(x, g, tm=512):
    n, d = x.shape
    return pl.pallas_call(
        _final_norm_kernel,
        grid=(n // tm,),
        in_specs=[pl.BlockSpec((tm, d), lambda i: (i, 0)), pl.BlockSpec((1, d), lambda i: (0, 0))],
        out_specs=pl.BlockSpec((tm, d), lambda i: (i, 0)),
        out_shape=jax.ShapeDtypeStruct((n, d), F32),
        compiler_params=_params("arbitrary"),
        name="final_norm",
    )(x, g.reshape(1, d))


def _qlat_kernel(h_ref, w_ref, wuk_ref, o_ref):
    z = _dot(h_ref[...], w_ref[...])
    for hd in range(A_HEADS):
        zh = z[:, hd * A_HEAD_DIM:(hd + 1) * A_HEAD_DIM].astype(BF16)
        o_ref[hd] = (_dot(zh, wuk_ref[hd]) * (A_HEAD_DIM ** -0.5)).astype(o_ref.dtype)


def _proj_qlat(h, w, wuk, tm=512):
    n, d = h.shape
    return pl.pallas_call(
        _qlat_kernel,
        grid=(n // tm,),
        in_specs=[
            pl.BlockSpec((tm, d), lambda i: (i, 0)),
            pl.BlockSpec((d, A_Q_COLS), lambda i: (0, 0)),
            pl.BlockSpec((A_HEADS, A_HEAD_DIM, A_KV_RANK), lambda i: (0, 0, 0)),
        ],
        out_specs=pl.BlockSpec((A_HEADS, tm, A_KV_RANK), lambda i: (0, i, 0)),
        out_shape=jax.ShapeDtypeStruct((A_HEADS, n, A_KV_RANK), BF16),
        compiler_params=_params("arbitrary"),
        name="proj_q_latent",
    )(h, w, wuk)


def _ckv_kernel(h_ref, w_ref, g_ref, o_ref, ot_ref, *, tk):
    z = _dot(h_ref[...], w_ref[...])
    ckv = _rms(z) * g_ref[...]
    o_ref[...] = ckv.astype(o_ref.dtype)
    for s in range(ot_ref.shape[0]):
        ot_ref[s] = ckv[s * tk:(s + 1) * tk, :].T.astype(ot_ref.dtype)


def _proj_ckv(h, w, g, tk, tm=1024):
    n, d = h.shape
    return pl.pallas_call(
        functools.partial(_ckv_kernel, tk=tk),
        grid=(n // tm,),
        in_specs=[
            pl.BlockSpec((tm, d), lambda i: (i, 0)),
            pl.BlockSpec((d, A_KV_RANK), lambda i: (0, 0)),
            pl.BlockSpec((1, A_KV_RANK), lambda i: (0, 0)),
        ],
        out_specs=[pl.BlockSpec((tm, A_KV_RANK), lambda i: (i, 0)),
                   pl.BlockSpec((tm // tk, A_KV_RANK, tk), lambda i: (i, 0, 0))],
        out_shape=[jax.ShapeDtypeStruct((n, A_KV_RANK), BF16),
                   jax.ShapeDtypeStruct((n // tk, A_KV_RANK, tk), BF16)],
        compiler_params=_params("arbitrary"),
        name="proj_latent_kv",
    )(h, w, g.reshape(1, A_KV_RANK))


def _rope(z, cos_t, sin_lo, sin_hi):
    half = IDX_ROPE_DIM // 2
    return (z * cos_t + pltpu.roll(z, half, 1) * sin_hi
            + pltpu.roll(z, LANES - half, 1) * sin_lo)


def _qi_kernel(h_ref, w_ref, cos_ref, slo_ref, shi_ref, o_ref):
    z = _dot(h_ref[...], w_ref[...])
    cos_t, slo, shi = cos_ref[...], slo_ref[...], shi_ref[...]
    for s in range(IDX_Q_COLS // LANES):
        zs = z[:, s * LANES:(s + 1) * LANES]
        o_ref[:, s * LANES:(s + 1) * LANES] = _rope(zs, cos_t, slo, shi).astype(o_ref.dtype)


def _proj_qi(h, w, tables, seq, tm=512):
    n, d = h.shape
    per = seq // tm
    tspec = pl.BlockSpec((tm, LANES), lambda i: (i % per, 0))
    return pl.pallas_call(
        _qi_kernel,
        grid=(n // tm,),
        in_specs=[
            pl.BlockSpec((tm, d), lambda i: (i, 0)),
            pl.BlockSpec((d, IDX_Q_COLS), lambda i: (0, 0)),
            tspec, tspec, tspec,
        ],
        out_specs=pl.BlockSpec((tm, IDX_Q_COLS), lambda i: (i, 0)),
        out_shape=jax.ShapeDtypeStruct((n, IDX_Q_COLS), BF16),
        compiler_params=_params("arbitrary"),
        name="proj_index_q",
    )(h, w, *tables)


def _kw_kernel(h_ref, w_ref, g_ref, b_ref, cos_ref, slo_ref, shi_ref, k_ref, wi_ref):
    z = _dot(h_ref[...], w_ref[...])
    zk = z[:, :LANES]
    mu = jnp.mean(zk, axis=-1, keepdims=True)
    var = jnp.mean(jnp.square(zk - mu), axis=-1, keepdims=True)
    kn = (zk - mu) * lax.rsqrt(var + NORM_EPS) * g_ref[...] + b_ref[...]
    k_ref[...] = _rope(kn, cos_ref[...], slo_ref[...], shi_ref[...]).astype(k_ref.dtype)
    wi_ref[...] = (z[:, LANES:] * (IDX_HEADS ** -0.5 * IDX_HEAD_DIM ** -0.5)).T


def _proj_kw(h, w, g2, b2, tables, seq, tm=1024):
    n, d = h.shape
    per = seq // tm
    tspec = pl.BlockSpec((tm, LANES), lambda i: (i % per, 0))
    vspec = pl.BlockSpec((1, LANES), lambda i: (0, 0))
    ospec = pl.BlockSpec((tm, LANES), lambda i: (i, 0))
    return pl.pallas_call(
        _kw_kernel,
        grid=(n // tm,),
        in_specs=[
            pl.BlockSpec((tm, d), lambda i: (i, 0)),
            pl.BlockSpec((d, 2 * LANES), lambda i: (0, 0)),
            vspec, vspec, tspec, tspec, tspec,
        ],
        out_specs=[ospec, pl.BlockSpec((LANES, tm), lambda i: (0, i))],
        out_shape=[jax.ShapeDtypeStruct((n, LANES), BF16), jax.ShapeDtypeStruct((LANES, n), F32)],
        compiler_params=_params("arbitrary"),
        name="proj_index_kw",
    )(h, w, g2, b2, *tables)


def _mm_kernel(a_ref, w_ref, o_ref):
    o_ref[...] = _dot(a_ref[...], w_ref[...]).astype(o_ref.dtype)


def _matmul(a, w, out_dtype, tm, tn, name):
    n, k = a.shape
    cols = w.shape[1]
    return pl.pallas_call(
        _mm_kernel,
        grid=(n // tm, cols // tn),
        in_specs=[pl.BlockSpec((tm, k), lambda i, j: (i, 0)), pl.BlockSpec((k, tn), lambda i, j: (0, j))],
        out_specs=pl.BlockSpec((tm, tn), lambda i, j: (i, j)),
        out_shape=jax.ShapeDtypeStruct((n, cols), out_dtype),
        compiler_params=_params("arbitrary", "arbitrary"),
        name=name,
    )(a, w)


def _dsa_kernel(qi_ref, wi_ref, k2_ref, ql_ref, ckv_ref, ckvt_ref, ba_ref, wuvt_ref, o_ref,
                key_scr, qm_scr, acc_scr, m_scr, l_scr, *, tq, tk, topk, n_delta):
    i = pl.program_id(1)
    q0 = i * tq
    nck = (q0 + tq + tk - 1) // tk
    nsub_k = tk // BAND_BLOCK
    nsub_q = tq // BAND_BLOCK
    key_row = lax.broadcasted_iota(jnp.int32, (tk, tq), 0)
    q_pos = q0 + lax.broadcasted_iota(jnp.int32, (tk, tq), 1)
    lane = lax.broadcasted_iota(jnp.int32, (tq, LANES), 1)

    for h in range(IDX_HEADS):
        qs = qi_ref[:, (h // 2) * LANES:(h // 2 + 1) * LANES].astype(F32)
        keep = (lane >= IDX_HEAD_DIM) if h % 2 else (lane < IDX_HEAD_DIM)
        qm_scr[h] = jnp.where(keep, qs, 0.0).astype(BF16)

    def idx_body(c, carry):
        k2 = k2_ref[pl.ds(pl.multiple_of(c * tk, tk), tk), :]
        score = jnp.zeros((tk, tq), F32)
        for h in range(IDX_HEADS):
            logits = _dot_nt(k2, qm_scr[h])
            score = score + wi_ref[h:h + 1, :] * jnp.maximum(logits, 0.0)
        score = jnp.where(c * tk + key_row <= q_pos, score, -jnp.inf)
        bits = pltpu.bitcast(score, jnp.int32)
        key_scr[c] = bits ^ ((bits >> 31) & 0x7FFFFFFF)
        return carry

    lax.fori_loop(0, nck, idx_body, 0)

    def bit_body(b, tx):
        cand_x = tx | jnp.left_shift(jnp.int32(1), 31 - b)
        cand = cand_x ^ INT_MIN

        def cnt_body(c, part):
            ge = (key_scr[c] >= cand).astype(jnp.int32)
            return part + jnp.sum(ge.reshape(tk // SUBLANES, SUBLANES, tq), axis=0)

        part = lax.fori_loop(0, nck, cnt_body, jnp.zeros((SUBLANES, tq), jnp.int32))
        cnt = jnp.sum(part.astype(F32), axis=0, keepdims=True)
        return jnp.where(cnt >= topk, cand_x, tx)

    tx = lax.fori_loop(0, 32, bit_body, jnp.zeros((1, tq), jnp.int32))
    thr = tx ^ INT_MIN

    m_scr[...] = jnp.full(m_scr.shape, NEG_INF, F32)
    l_scr[...] = jnp.zeros(l_scr.shape, F32)
    acc_scr[...] = jnp.zeros(acc_scr.shape, F32)

    def att_body(c, carry):
        kv = ckv_ref[pl.ds(pl.multiple_of(c * tk, tk), tk), :]
        kvt = ckvt_ref[c]
        sel = (key_scr[c] >= thr) & (c * tk + key_row <= q_pos)
        madd = jnp.where(sel, 0.0, NEG_INF)
        base_delta = (q0 - c * tk) // BAND_BLOCK
        s_next = _dot_nt(kv, ql_ref[0])
        for h in range(A_HEADS):
            s_raw = s_next
            if h + 1 < A_HEADS:
                s_next = _dot_nt(kv, ql_ref[h + 1])
            rows = []
            for j in range(nsub_k):
                tiles = []
                for a in range(nsub_q):
                    delta = jnp.clip(base_delta + (a - j), 0, n_delta - 1)
                    tiles.append(ba_ref[h, delta])
                rows.append(tiles[0] if nsub_q == 1 else jnp.concatenate(tiles, axis=1))
            bias = rows[0] if nsub_k == 1 else jnp.concatenate(rows, axis=0)
            s = s_raw + bias + madd
            m_prev = m_scr[h:h + 1, :]
            m_new = jnp.maximum(m_prev, jnp.max(s, axis=0, keepdims=True))
            alpha = jnp.exp(m_prev - m_new)
            p = jnp.exp(s - m_new)
            l_scr[h:h + 1, :] = alpha * l_scr[h:h + 1, :] + jnp.sum(p, axis=0, keepdims=True)
            m_scr[h:h + 1, :] = m_new
            if h + 1 < A_HEADS:
                nxt = pltpu.bitcast(s_next[:SUBLANES], jnp.uint32)
                zero = lax.shift_right_logical(lax.shift_right_logical(nxt, jnp.uint32(16)), jnp.uint32(16))
                p = jnp.concatenate([p[:SUBLANES] + zero.astype(F32), p[SUBLANES:]], axis=0)
            acc_scr[h] = acc_scr[h] * alpha + _dot(kvt, p.astype(BF16))
        return carry

    lax.fori_loop(0, nck, att_body, 0)

    for h in range(A_HEADS):
        o_lat_t = (acc_scr[h] / l_scr[h:h + 1, :]).astype(BF16)
        y_t = _dot(wuvt_ref[h], o_lat_t)
        o_ref[:, h * A_HEAD_DIM:(h + 1) * A_HEAD_DIM] = y_t.T.astype(o_ref.dtype)


def _dsa_attention(qi, wi_t, k2, ql, ckv, ckv_t, bias_a, wuv_t, batch, seq, tq):
    tk = tq
    n = batch * seq
    nq = seq // tq
    nck = seq // tk
    topk = min(DSA_TOPK, seq // 4)
    assert topk <= tk and seq % tq == 0 and tq % BAND_BLOCK == 0
    n_delta = bias_a.shape[1]
    kern = functools.partial(_dsa_kernel, tq=tq, tk=tk, topk=topk, n_delta=n_delta)
    once = pl.Buffered(1)
    return pl.pallas_call(
        kern,
        grid=(batch, nq),
        in_specs=[
            pl.BlockSpec((tq, IDX_Q_COLS), lambda b, i: (b * nq + i, 0)),
            pl.BlockSpec((LANES, tq), lambda b, i: (0, b * nq + i)),
            pl.BlockSpec((seq, LANES), lambda b, i: (b, 0), pipeline_mode=once),
            pl.BlockSpec((A_HEADS, tq, A_KV_RANK), lambda b, i: (0, b * nq + i, 0)),
            pl.BlockSpec((seq, A_KV_RANK), lambda b, i: (b, 0), pipeline_mode=once),
            pl.BlockSpec((nck, A_KV_RANK, tk), lambda b, i: (b, 0, 0), pipeline_mode=once),
            pl.BlockSpec(bias_a.shape, lambda b, i: (0, 0, 0, 0), pipeline_mode=once),
            pl.BlockSpec(wuv_t.shape, lambda b, i: (0, 0, 0), pipeline_mode=once),
        ],
        out_specs=pl.BlockSpec((tq, A_OUT), lambda b, i: (b * nq + i, 0)),
        out_shape=jax.ShapeDtypeStruct((n, A_OUT), BF16),
        scratch_shapes=[
            pltpu.VMEM((nck, tk, tq), jnp.int32),
            pltpu.VMEM((IDX_HEADS, tq, LANES), BF16),
            pltpu.VMEM((A_HEADS, A_KV_RANK, tq), F32),
            pltpu.VMEM((A_HEADS, tq), F32),
            pltpu.VMEM((A_HEADS, tq), F32),
        ],
        compiler_params=_params("arbitrary", "arbitrary"),
        name="dsa_attention",
    )(qi, wi_t, k2, ql, ckv, ckv_t, bias_a, wuv_t)


def _deinterleave_matrix(tm, dil):
    per = tm // dil
    i = jnp.arange(tm)
    src = (i % per) * dil + i // per
    return (src[:, None] == i[None, :]).astype(BF16)


def _mm_stream_kernel(a_ref, w_ref, p_ref, o_ref):
    z = _dot(a_ref[...], w_ref[...]).astype(BF16)
    zs = _dot(p_ref[...], z).astype(BF16)
    o_ref[0] = zs.reshape(o_ref.shape[1:])


def _proj_dilated(h, w, batch, seq, dil, tm=512):
    n, d = h.shape
    c = w.shape[1]
    if dil == 1:
        return _matmul(h, w, BF16, 1024, c // 2, "proj_dilated_qkv_g0").reshape(batch, 1, seq, c)
    per_seq = seq // tm
    return pl.pallas_call(
        _mm_stream_kernel,
        grid=(n // tm,),
        in_specs=[
            pl.BlockSpec((tm, d), lambda i: (i, 0)),
            pl.BlockSpec((d, c), lambda i: (0, 0)),
            pl.BlockSpec((tm, tm), lambda i: (0, 0)),
        ],
        out_specs=pl.BlockSpec((1, dil, tm // dil, c), lambda i: (i // per_seq, 0, i % per_seq, 0)),
        out_shape=jax.ShapeDtypeStruct((batch, dil, seq // dil, c), BF16),
        compiler_params=_params("arbitrary"),
        name=f"proj_dilated_qkv_d{dil}",
    )(h, w, _deinterleave_matrix(tm, dil))


LSE_LANES = LANES // B_HEADS_PER_GROUP


def _dil_kernel(q_ref, kc_ref, kp_ref, vc_ref, vp_ref, bias_ref, o_ref, *, nsub, steps):
    first = pl.program_id(2) == 0
    P = BAND_BLOCK
    dh = B_HEAD_DIM
    row = lax.broadcasted_iota(jnp.int32, (P, 2 * P), 0)
    u = lax.broadcasted_iota(jnp.int32, (P, 2 * P), 1)
    back = row + P - u
    band = (back >= 0) & (back <= steps)
    band_first = band & ((u >= P) | jnp.logical_not(first))
    lane_head = lax.broadcasted_iota(jnp.int32, (P, LANES), 1) // LSE_LANES
    for a in range(nsub):
        valid = band_first if a == 0 else band
        rs = slice(a * P, (a + 1) * P)
        lse_all = jnp.zeros((P, LANES), F32)
        for j in range(B_HEADS_PER_GROUP):
            cs = slice(j * dh, (j + 1) * dh)
            if a == 0:
                k_prev, v_prev = kp_ref[0, 0, :, cs], vp_ref[0, 0, :, cs]
            else:
                ps = slice((a - 1) * P, a * P)
                k_prev, v_prev = kc_ref[0, 0, ps, cs], vc_ref[0, 0, ps, cs]
            k_cat = jnp.concatenate([k_prev, kc_ref[0, 0, rs, cs]], axis=0)
            v_cat = jnp.concatenate([v_prev, vc_ref[0, 0, rs, cs]], axis=0)
            s = _dot_nt(q_ref[0, 0, rs, cs], k_cat) * (dh ** -0.5) + bias_ref[0, j]
            s = jnp.where(valid, s, NEG_INF)
            mx = jnp.max(s, axis=-1, keepdims=True)
            p = jnp.exp(s - mx)
            den = jnp.sum(p, axis=-1, keepdims=True)
            o_ref[0, 0, rs, cs] = _dot((p / den).astype(BF16), v_cat)
            lse_all = jnp.where(lane_head == j, mx + jnp.log(den), lse_all)
        o_ref[0, 0, rs, B_OUT:] = lse_all


def _dilated_group(qkv, bias_b, g, batch, seq):
    window, dil = B_GROUPS[g]
    steps = window // dil
    assert steps <= BAND_BLOCK
    m = seq // dil
    rows = min(4 * BAND_BLOCK, m)
    assert m % rows == 0 and rows % BAND_BLOCK == 0
    nsub = rows // BAND_BLOCK
    width = B_OUT

    def cur(which):
        return pl.BlockSpec((1, 1, rows, width), lambda b, r, i: (b, r, i, which))

    def prev(which):
        return pl.BlockSpec((1, 1, BAND_BLOCK, width),
                            lambda b, r, i: (b, r, jnp.maximum(i * nsub - 1, 0), which))

    return pl.pallas_call(
        functools.partial(_dil_kernel, nsub=nsub, steps=steps),
        grid=(batch, dil, m // rows),
        in_specs=[cur(0), cur(1), prev(1), cur(2), prev(2),
                  pl.BlockSpec((1, B_HEADS_PER_GROUP, BAND_BLOCK, 2 * BAND_BLOCK),
                               lambda b, r, i: (g, 0, 0, 0))],
        out_specs=pl.BlockSpec((1, 1, rows, width + LANES), lambda b, r, i: (b, r, i, 0)),
        out_shape=jax.ShapeDtypeStruct((batch, dil, m, width + LANES), F32),
        compiler_params=_params("arbitrary", "arbitrary", "arbitrary"),
        name=f"dilated_attention_g{g}",
    )(qkv, qkv, qkv, qkv, qkv, bias_b)


def _to_token_order(blk, pinv):
    x = blk.reshape(pinv.shape[0], blk.shape[-1])
    hi = x.astype(BF16)
    r1 = x - hi.astype(F32)
    mid = r1.astype(BF16)
    lo = (r1 - mid.astype(F32)).astype(BF16)
    return (_dot(pinv, hi) + _dot(pinv, mid)) + _dot(pinv, lo)


def _dil_merge_kernel(t0_ref, t1_ref, t2_ref, p1_ref, p2_ref, y_ref):
    groups = [t0_ref[...], _to_token_order(t1_ref[0], p1_ref[...]), _to_token_order(t2_ref[0], p2_ref[...])]
    tm = y_ref.shape[0]
    dh = B_HEAD_DIM
    for j in range(B_HEADS_PER_GROUP):
        lse = [jnp.broadcast_to(t[:, B_OUT + j * LSE_LANES:B_OUT + j * LSE_LANES + 1], (tm, dh))
               for t in groups]
        mx = jnp.maximum(jnp.maximum(lse[0], lse[1]), lse[2])
        e = [jnp.exp(a - mx) for a in lse]
        den = e[0] + e[1] + e[2]
        y = sum((e[g] / den) * groups[g][:, j * dh:(j + 1) * dh] for g in range(3))
        y_ref[:, j * dh:(j + 1) * dh] = y.astype(y_ref.dtype)


def _dilated_merge(packed, batch, seq, tm=512):
    n = batch * seq
    w = packed[0].shape[-1]
    per_seq = seq // tm
    dils = [d for _, d in B_GROUPS]
    assert dils[0] == 1 and len(dils) == 3

    def stream(dil):
        return pl.BlockSpec((1, dil, tm // dil, w), lambda i: (i // per_seq, 0, i % per_seq, 0))

    pspec = pl.BlockSpec((tm, tm), lambda i: (0, 0))
    return pl.pallas_call(
        _dil_merge_kernel,
        grid=(n // tm,),
        in_specs=[pl.BlockSpec((tm, w), lambda i: (i, 0)), stream(dils[1]), stream(dils[2]), pspec, pspec],
        out_specs=pl.BlockSpec((tm, B_OUT), lambda i: (i, 0)),
        out_shape=jax.ShapeDtypeStruct((n, B_OUT), BF16),
        compiler_params=_params("arbitrary"),
        name="dilated_merge",
    )(packed[0].reshape(n, w), packed[1], packed[2],
      _deinterleave_matrix(tm, dils[1]).T, _deinterleave_matrix(tm, dils[2]).T)


def _merge_kernel(h_ref, ya_ref, yb_ref, wga_ref, wgb_ref, wa_ref, wb_ref, o_ref):
    h = h_ref[...]
    ga = jax.nn.sigmoid(_dot(h, wga_ref[...]))
    gb = jax.nn.sigmoid(_dot(h, wgb_ref[...]))
    merged = ga * _dot(ya_ref[...], wa_ref[...]) + gb * _dot(yb_ref[...], wb_ref[...])
    o_ref[...] = merged.astype(o_ref.dtype)


def _gated_merge(h, ya, yb, wga, wgb, wa, wb, tm=512, tn=512):
    n, d = h.shape

    def rows(k):
        return pl.BlockSpec((tm, k), lambda i, j: (i, 0))

    def cols(k):
        return pl.BlockSpec((k, tn), lambda i, j: (0, j))

    return pl.pallas_call(
        _merge_kernel,
        grid=(n // tm, d // tn),
        in_specs=[rows(d), rows(A_OUT), rows(B_OUT), cols(d), cols(d), cols(A_OUT), cols(B_OUT)],
        out_specs=pl.BlockSpec((tm, tn), lambda i, j: (i, j)),
        out_shape=jax.ShapeDtypeStruct((n, d), BF16),
        compiler_params=_params("arbitrary", "arbitrary"),
        name="gated_merge",
    )(h, ya, yb, wga, wgb, wa, wb)


def _mm_res_kernel(a_ref, w_ref, x_ref, g_ref, o_ref):
    o_ref[...] = x_ref[...] + g_ref[0] * _dot(a_ref[...], w_ref[...])


def _matmul_residual(a, w, x, gate, seq, tm, tn, name):
    n, k = a.shape
    d = w.shape[1]
    per = seq // tm
    return pl.pallas_call(
        _mm_res_kernel,
        grid=(n // tm, d // tn),
        in_specs=[
            pl.BlockSpec((tm, k), lambda i, j: (i, 0)),
            pl.BlockSpec((k, tn), lambda i, j: (0, j)),
            pl.BlockSpec((tm, tn), lambda i, j: (i, j)),
            pl.BlockSpec((1, 1, tn), lambda i, j: (i // per, 0, j)),
        ],
        out_specs=pl.BlockSpec((tm, tn), lambda i, j: (i, j)),
        out_shape=jax.ShapeDtypeStruct((n, d), F32),
        compiler_params=_params("arbitrary", "arbitrary"),
        name=name,
    )(a, w, x, gate)


def _ffn_up_kernel(h_ref, wg_ref, wu_ref, o_ref):
    h = h_ref[...]
    a = _dot(h, wg_ref[...])
    o_ref[...] = (a * jax.nn.sigmoid(a) * _dot(h, wu_ref[...])).astype(o_ref.dtype)


def _ffn_up(h, wg, wu, tm=1024, tn=512):
    n, d = h.shape
    f = wg.shape[1]
    wspec = pl.BlockSpec((d, tn), lambda i, j: (0, j))
    return pl.pallas_call(
        _ffn_up_kernel,
        grid=(n // tm, f // tn),
        in_specs=[pl.BlockSpec((tm, d), lambda i, j: (i, 0)), wspec, wspec],
        out_specs=pl.BlockSpec((tm, tn), lambda i, j: (i, j)),
        out_shape=jax.ShapeDtypeStruct((n, f), BF16),
        compiler_params=_params("arbitrary", "arbitrary"),
        name="ffn_up",
    )(h, wg, wu)


def _rope_tables(seq):
    half = IDX_ROPE_DIM // 2
    freqs = ROPE_THETA ** (-jnp.arange(half, dtype=F32) / half)
    ang = jnp.arange(seq).astype(F32)[:, None] * freqs[None, :]
    cos, sin = jnp.cos(ang), jnp.sin(ang)
    rest = IDX_HEAD_DIM - IDX_ROPE_DIM
    one = jnp.ones((seq, rest), F32)
    zr = jnp.zeros((seq, rest), F32)
    zh = jnp.zeros((seq, half), F32)
    cos_t = jnp.concatenate([cos, cos, one], axis=1)
    sin_lo = jnp.concatenate([-sin, zh, zr], axis=1)
    sin_hi = jnp.concatenate([zh, sin, zr], axis=1)
    rep = LANES // IDX_HEAD_DIM
    return tuple(jnp.tile(t, (1, rep)) for t in (cos_t, sin_lo, sin_hi))


def _layer(x, h_mod, tables, bias_a, bias_b, batch, seq, w_in, kv_norm_g, idx_ln_g, idx_ln_b,
           w_uk, w_uv, w_a_up, w_b_up, w_out, norm1_g, norm2_g, w_ff_gate, w_ff_up, w_ff_down):
    shift1, scale1, gate1, shift2, scale2, gate2 = h_mod
    offs = [0]
    for s in IN_SIZES:
        offs.append(offs[-1] + s)
    seg = [w_in[:, offs[k]:offs[k + 1]] for k in range(len(IN_SIZES))]
    w_qa, w_kv, w_qi, w_ki, w_wi, w_qkvb, w_gate = seg
    d = w_in.shape[0]
    rep = LANES // IDX_HEAD_DIM
    w_kw = jnp.concatenate(
        [w_ki] * rep + [w_wi, jnp.zeros((d, LANES - IDX_HEADS), w_in.dtype)], axis=1).astype(BF16)
    ln_g2 = jnp.tile(idx_ln_g, rep).reshape(1, LANES)
    ln_b2 = jnp.tile(idx_ln_b, rep).reshape(1, LANES)

    h = _norm_mod(x, norm1_g, scale1, shift1, seq)
    ql = _proj_qlat(h, w_qa.astype(BF16), w_uk.astype(BF16))
    ckv, ckv_t = _proj_ckv(h, w_kv.astype(BF16), kv_norm_g, DSA_TILE)
    qi = _proj_qi(h, w_qi.astype(BF16), tables, seq)
    k2, wi_t = _proj_kw(h, w_kw, ln_g2, ln_b2, tables, seq)
    wuv_t = jnp.swapaxes(w_uv, 1, 2).astype(BF16)
    ya = _dsa_attention(qi, wi_t, k2, ql, ckv, ckv_t, bias_a, wuv_t, batch, seq, DSA_TILE)

    ng = len(B_GROUPS)
    packed = []
    for g, (_, dil) in enumerate(B_GROUPS):
        w_g = jnp.concatenate([w_qkvb[:, (s * ng + g) * B_OUT:(s * ng + g + 1) * B_OUT] for s in range(3)],
                              axis=1).astype(BF16)
        packed.append(_dilated_group(_proj_dilated(h, w_g, batch, seq, dil), bias_b, g, batch, seq))
    yb = _dilated_merge(packed, batch, seq)

    merged = _gated_merge(h, ya, yb, w_gate[:, :d].astype(BF16), w_gate[:, d:].astype(BF16),
                          w_a_up.astype(BF16), w_b_up.astype(BF16))
    x = _matmul_residual(merged, w_out.astype(BF16), x, gate1, seq, 1024, 512, "out_proj_residual")

    h2 = _norm_mod(x, norm2_g, scale2, shift2, seq)
    act = _ffn_up(h2, w_ff_gate.astype(BF16), w_ff_up.astype(BF16))
    return _matmul_residual(act, w_ff_down.astype(BF16), x, gate2, seq, 512, 512, "ffn_down_residual")


def kernel(x, c, rel_bias, w_ada, b_ada, norm1_g, w_in, kv_norm_g, idx_ln_g, idx_ln_b, w_uk, w_uv,
           w_a_up, w_b_up, w_out, norm2_g, w_ff_gate, w_ff_up, w_ff_down, final_g):
    batch, seq, d = x.shape
    depth = w_ada.shape[0]
    assert d == D_MODEL and seq % (B_GROUPS[-1][1] * BAND_BLOCK) == 0
    n = batch * seq
    rows = -(-batch // SUBLANES) * SUBLANES
    mod = _modulation(jnp.pad(c, ((0, rows - batch), (0, 0))), w_ada, b_ada)
    mod = mod[:, :batch].reshape(depth, batch, 6, 1, d)
    bias_a = _bias_a_table(rel_bias, min(seq // BAND_BLOCK, FAR_DELTA + 1))
    bias_b = _bias_b_table(rel_bias)
    tables = _rope_tables(seq)
    xf = x.reshape(n, d)
    for l in range(depth):
        h_mod = [mod[l, :, k] for k in range(6)]
        xf = _layer(xf, h_mod, tables, bias_a, bias_b, batch, seq, w_in[l], kv_norm_g[l], idx_ln_g[l],
                    idx_ln_b[l], w_uk[l], w_uv[l], w_a_up[l], w_b_up[l], w_out[l], norm1_g[l],
                    norm2_g[l], w_ff_gate[l], w_ff_up[l], w_ff_down[l])
    return _final_norm(xf, final_g).reshape(batch, seq, d)
```

```python
import functools
import math

import jax
import jax.numpy as jnp
from jax import lax
from jax.experimental import pallas as pl
from jax.experimental.pallas import tpu as pltpu

D_MODEL = 2048
A_HEADS = 8
A_HEAD_DIM = 128
A_KV_RANK = 256
IDX_HEADS = 16
IDX_HEAD_DIM = 64
IDX_ROPE_DIM = 32
ROPE_THETA = 10000.0
DSA_TOPK = 256
B_GROUPS = ((128, 1), (512, 4), (2048, 16))
B_HEADS_PER_GROUP = 4
B_HEAD_DIM = 128
B_HEADS = B_HEADS_PER_GROUP * len(B_GROUPS)
BAND_BLOCK = 128
REL_BUCKETS = 32
REL_MAX_DISTANCE = 2048
N_BIAS_HEADS = A_HEADS + B_HEADS
D_FF = -(-8 * D_MODEL // (3 * 256)) * 256
NORM_EPS = 1e-6
NEG_INF = -1e30

A_Q_COLS = A_HEADS * A_HEAD_DIM
IDX_Q_COLS = IDX_HEADS * IDX_HEAD_DIM
B_QKV_COLS = 3 * B_HEADS * B_HEAD_DIM
GATE_COLS = 2 * D_MODEL
IN_SIZES = (A_Q_COLS, A_KV_RANK, IDX_Q_COLS, IDX_HEAD_DIM, IDX_HEADS, B_QKV_COLS, GATE_COLS)
A_OUT = A_HEADS * A_HEAD_DIM
B_OUT = B_HEADS_PER_GROUP * B_HEAD_DIM

LANES = 128
SUBLANES = 8
VMEM_LIMIT_BYTES = 56 * 1024 * 1024

BF16 = jnp.bfloat16
F32 = jnp.float32
LOG2E = math.log2(math.e)
HALF_BITS = 16
HALF_MASK = 2 ** HALF_BITS - 1
HALF_BIAS = 2 ** (HALF_BITS - 1)
PACKED_ROWS = 2 * SUBLANES

FAR_DELTA = -(-(REL_MAX_DISTANCE + BAND_BLOCK - 1) // BAND_BLOCK)
DSA_TILE = 512
DSA_STRIP = 256


def _dot(a, b):
    return jnp.dot(a, b, preferred_element_type=F32)


def _dot_nt(a, b):
    return lax.dot_general(a, b, (((1,), (1,)), ((), ())), preferred_element_type=F32)


def _params(*sem):
    return pltpu.CompilerParams(dimension_semantics=sem, vmem_limit_bytes=VMEM_LIMIT_BYTES)


def _mod_kernel(c_ref, w_ref, b_ref, o_ref):
    c = c_ref[...]
    a = (c * jax.nn.sigmoid(c)).astype(BF16)
    o_ref[0] = _dot(a, w_ref[0].astype(BF16)) + b_ref[0]


def _modulation(c_pad, w_ada, b_ada, tn=1024):
    depth, d, n6 = w_ada.shape
    rows = c_pad.shape[0]
    return pl.pallas_call(
        _mod_kernel,
        grid=(depth, n6 // tn),
        in_specs=[
            pl.BlockSpec((rows, d), lambda l, j: (0, 0)),
            pl.BlockSpec((1, d, tn), lambda l, j: (l, 0, j)),
            pl.BlockSpec((1, 1, tn), lambda l, j: (l, 0, j)),
        ],
        out_specs=pl.BlockSpec((1, rows, tn), lambda l, j: (l, 0, j)),
        out_shape=jax.ShapeDtypeStruct((depth, rows, n6), F32),
        compiler_params=_params("arbitrary", "arbitrary"),
        name="adaln_modulation",
    )(c_pad, w_ada, b_ada.reshape(depth, 1, n6))


def _bucket(dist):
    n = jnp.maximum(dist, 0)
    exact = REL_BUCKETS // 2
    nf = jnp.maximum(n, 1).astype(F32)
    large = exact + (jnp.log(nf / exact) / math.log(REL_MAX_DISTANCE / exact)
                     * (REL_BUCKETS - exact)).astype(jnp.int32)
    return jnp.where(n < exact, n, jnp.minimum(large, REL_BUCKETS - 1))


def _lookup(rb_ref, bucket, head):
    t = jnp.zeros(bucket.shape, F32)
    for k in range(REL_BUCKETS):
        t = jnp.where(bucket == k, rb_ref[k, head], t)
    return t


def _bias_a_kernel(rb_ref, o_ref):
    delta = pl.program_id(0)
    row = lax.broadcasted_iota(jnp.int32, (BAND_BLOCK, BAND_BLOCK), 0)
    col = lax.broadcasted_iota(jnp.int32, (BAND_BLOCK, BAND_BLOCK), 1)
    bucket = _bucket(delta * BAND_BLOCK + col - row)
    for h in range(A_HEADS):
        o_ref[h, 0] = _lookup(rb_ref, bucket, h) * LOG2E


def _bias_a_table(rel_bias, n_delta):
    return pl.pallas_call(
        _bias_a_kernel,
        grid=(n_delta,),
        in_specs=[pl.BlockSpec(memory_space=pltpu.SMEM)],
        out_specs=pl.BlockSpec((A_HEADS, 1, BAND_BLOCK, BAND_BLOCK), lambda d: (0, d, 0, 0)),
        out_shape=jax.ShapeDtypeStruct((A_HEADS, n_delta, BAND_BLOCK, BAND_BLOCK), F32),
        compiler_params=_params("arbitrary"),
        name="rel_bias_table_a",
    )(rel_bias)


def _bias_b_kernel(rb_ref, o_ref):
    g = pl.program_id(0)
    dil = jnp.where(g == 0, B_GROUPS[0][1], jnp.where(g == 1, B_GROUPS[1][1], B_GROUPS[2][1]))
    row = lax.broadcasted_iota(jnp.int32, (BAND_BLOCK, 2 * BAND_BLOCK), 0)
    u = lax.broadcasted_iota(jnp.int32, (BAND_BLOCK, 2 * BAND_BLOCK), 1)
    bucket = _bucket((row + BAND_BLOCK - u) * dil)
    for j in range(B_HEADS_PER_GROUP):
        o_ref[0, j] = _lookup(rb_ref, bucket, A_HEADS + g * B_HEADS_PER_GROUP + j)


def _bias_b_table(rel_bias):
    ng = len(B_GROUPS)
    return pl.pallas_call(
        _bias_b_kernel,
        grid=(ng,),
        in_specs=[pl.BlockSpec(memory_space=pltpu.SMEM)],
        out_specs=pl.BlockSpec((1, B_HEADS_PER_GROUP, BAND_BLOCK, 2 * BAND_BLOCK),
                               lambda g: (g, 0, 0, 0)),
        out_shape=jax.ShapeDtypeStruct((ng, B_HEADS_PER_GROUP, BAND_BLOCK, 2 * BAND_BLOCK), F32),
        compiler_params=_params("arbitrary"),
        name="rel_bias_table_b",
    )(rel_bias)


def _rms(x):
    return x * lax.rsqrt(jnp.mean(x * x, axis=-1, keepdims=True) + NORM_EPS)


def _norm_mod_kernel(x_ref, g_ref, sc_ref, sh_ref, o_ref):
    y = _rms(x_ref[...]) * g_ref[...]
    o_ref[...] = (y * (1.0 + sc_ref[0]) + sh_ref[0]).astype(o_ref.dtype)


def _norm_mod(x, g, scale, shift, seq, tm=512):
    n, d = x.shape
    per = seq // tm
    return pl.pallas_call(
        _norm_mod_kernel,
        grid=(n // tm,),
        in_specs=[
            pl.BlockSpec((tm, d), lambda i: (i, 0)),
            pl.BlockSpec((1, d), lambda i: (0, 0)),
            pl.BlockSpec((1, 1, d), lambda i: (i // per, 0, 0)),
            pl.BlockSpec((1, 1, d), lambda i: (i // per, 0, 0)),
        ],
        out_specs=pl.BlockSpec((tm, d), lambda i: (i, 0)),
        out_shape=jax.ShapeDtypeStruct((n, d), BF16),
        compiler_params=_params("arbitrary"),
        name="norm_modulate",
    )(x, g.reshape(1, d), scale, shift)


def _qlat_kernel(h_ref, w_ref, wuk_ref, o_ref):
    z = _dot(h_ref[...], w_ref[...])
    for hd in range(A_HEADS):
        zh = z[:, hd * A_HEAD_DIM:(hd + 1) * A_HEAD_DIM].astype(BF16)
        o_ref[hd] = (_dot(zh, wuk_ref[hd]) * (A_HEAD_DIM ** -0.5 * LOG2E)).astype(o_ref.dtype)


def _proj_qlat(h, w, wuk, tm=512):
    n, d = h.shape
    return pl.pallas_call(
        _qlat_kernel,
        grid=(n // tm,),
        in_specs=[
            pl.BlockSpec((tm, d), lambda i: (i, 0)),
            pl.BlockSpec((d, A_Q_COLS), lambda i: (0, 0)),
            pl.BlockSpec((A_HEADS, A_HEAD_DIM, A_KV_RANK), lambda i: (0, 0, 0)),
        ],
        out_specs=pl.BlockSpec((A_HEADS, tm, A_KV_RANK), lambda i: (0, i, 0)),
        out_shape=jax.ShapeDtypeStruct((A_HEADS, n, A_KV_RANK), BF16),
        compiler_params=_params("arbitrary"),
        name="proj_q_latent",
    )(h, w, wuk)


def _ckv_kernel(h_ref, w_ref, g_ref, o_ref, ot_ref, *, tk):
    z = _dot(h_ref[...], w_ref[...])
    ckv = _rms(z) * g_ref[...]
    o_ref[...] = ckv.astype(o_ref.dtype)
    for s in range(ot_ref.shape[0]):
        ot_ref[s] = ckv[s * tk:(s + 1) * tk, :].T.astype(ot_ref.dtype)


def _proj_ckv(h, w, g, tk, tm=1024):
    n, d = h.shape
    return pl.pallas_call(
        functools.partial(_ckv_kernel, tk=tk),
        grid=(n // tm,),
        in_specs=[
            pl.BlockSpec((tm, d), lambda i: (i, 0)),
            pl.BlockSpec((d, A_KV_RANK), lambda i: (0, 0)),
            pl.BlockSpec((1, A_KV_RANK), lambda i: (0, 0)),
        ],
        out_specs=[pl.BlockSpec((tm, A_KV_RANK), lambda i: (i, 0)),
                   pl.BlockSpec((tm // tk, A_KV_RANK, tk), lambda i: (i, 0, 0))],
        out_shape=[jax.ShapeDtypeStruct((n, A_KV_RANK), BF16),
                   jax.ShapeDtypeStruct((n // tk, A_KV_RANK, tk), BF16)],
        compiler_params=_params("arbitrary"),
        name="proj_latent_kv",
    )(h, w, g.reshape(1, A_KV_RANK))


def _rope(z, cos_t, sin_lo, sin_hi):
    half = IDX_ROPE_DIM // 2
    return (z * cos_t + pltpu.roll(z, half, 1) * sin_hi
            + pltpu.roll(z, LANES - half, 1) * sin_lo)


def _qi_kernel(h_ref, w_ref, cos_ref, slo_ref, shi_ref, o_ref):
    z = _dot(h_ref[...], w_ref[...])
    cos_t, slo, shi = cos_ref[...], slo_ref[...], shi_ref[...]
    for s in range(IDX_Q_COLS // LANES):
        zs = z[:, s * LANES:(s + 1) * LANES]
        o_ref[:, s * LANES:(s + 1) * LANES] = _rope(zs, cos_t, slo, shi).astype(o_ref.dtype)


def _proj_qi(h, w, tables, seq, tm=512):
    n, d = h.shape
    per = seq // tm
    tspec = pl.BlockSpec((tm, LANES), lambda i: (i % per, 0))
    return pl.pallas_call(
        _qi_kernel,
        grid=(n // tm,),
        in_specs=[
            pl.BlockSpec((tm, d), lambda i: (i, 0)),
            pl.BlockSpec((d, IDX_Q_COLS), lambda i: (0, 0)),
            tspec, tspec, tspec,
        ],
        out_specs=pl.BlockSpec((tm, IDX_Q_COLS), lambda i: (i, 0)),
        out_shape=jax.ShapeDtypeStruct((n, IDX_Q_COLS), BF16),
        compiler_params=_params("arbitrary"),
        name="proj_index_q",
    )(h, w, *tables)


def _kw_kernel(h_ref, w_ref, g_ref, b_ref, cos_ref, slo_ref, shi_ref, k_ref, wi_ref):
    z = _dot(h_ref[...], w_ref[...])
    zk = z[:, :LANES]
    mu = jnp.mean(zk, axis=-1, keepdims=True)
    var = jnp.mean(jnp.square(zk - mu), axis=-1, keepdims=True)
    kn = (zk - mu) * lax.rsqrt(var + NORM_EPS) * g_ref[...] + b_ref[...]
    k_ref[...] = _rope(kn, cos_ref[...], slo_ref[...], shi_ref[...]).astype(k_ref.dtype)
    wi_ref[...] = (z[:, LANES:] * (IDX_HEADS ** -0.5 * IDX_HEAD_DIM ** -0.5)).T


def _proj_kw(h, w, g2, b2, tables, seq, tm=1024):
    n, d = h.shape
    per = seq // tm
    tspec = pl.BlockSpec((tm, LANES), lambda i: (i % per, 0))
    vspec = pl.BlockSpec((1, LANES), lambda i: (0, 0))
    ospec = pl.BlockSpec((tm, LANES), lambda i: (i, 0))
    return pl.pallas_call(
        _kw_kernel,
        grid=(n // tm,),
        in_specs=[
            pl.BlockSpec((tm, d), lambda i: (i, 0)),
            pl.BlockSpec((d, 2 * LANES), lambda i: (0, 0)),
            vspec, vspec, tspec, tspec, tspec,
        ],
        out_specs=[ospec, pl.BlockSpec((LANES, tm), lambda i: (0, i))],
        out_shape=[jax.ShapeDtypeStruct((n, LANES), BF16), jax.ShapeDtypeStruct((LANES, n), F32)],
        compiler_params=_params("arbitrary"),
        name="proj_index_kw",
    )(h, w, g2, b2, *tables)


def _mm_kernel(a_ref, w_ref, o_ref):
    o_ref[...] = _dot(a_ref[...], w_ref[...]).astype(o_ref.dtype)


def _matmul(a, w, out_dtype, tm, tn, name):
    n, k = a.shape
    cols = w.shape[1]
    return pl.pallas_call(
        _mm_kernel,
        grid=(n // tm, cols // tn),
        in_specs=[pl.BlockSpec((tm, k), lambda i, j: (i, 0)), pl.BlockSpec((k, tn), lambda i, j: (0, j))],
        out_specs=pl.BlockSpec((tm, tn), lambda i, j: (i, j)),
        out_shape=jax.ShapeDtypeStruct((n, cols), out_dtype),
        compiler_params=_params("arbitrary", "arbitrary"),
        name=name,
    )(a, w)


def _dsa_kernel(qi_ref, wi_ref, k2_ref, ql_ref, ckv_ref, ckvt_ref, ba_ref, wuvt_ref, o_ref,
                key_scr, hi_scr, lo_scr, qm_scr, acc_scr, m_scr, l_scr, *, tq, tk, topk, n_delta):
    i = pl.program_id(1)
    q0 = i * tq
    nck = (q0 + tq + tk - 1) // tk
    nsub_k = tk // BAND_BLOCK
    qw = min(tq, DSA_STRIP)
    key_row = lax.broadcasted_iota(jnp.int32, (tk, tq), 0)
    q_pos = q0 + lax.broadcasted_iota(jnp.int32, (tk, tq), 1)
    lane = lax.broadcasted_iota(jnp.int32, (tq, LANES), 1)

    for h in range(IDX_HEADS):
        qs = qi_ref[:, (h // 2) * LANES:(h // 2 + 1) * LANES].astype(F32)
        keep = (lane >= IDX_HEAD_DIM) if h % 2 else (lane < IDX_HEAD_DIM)
        qm_scr[h] = jnp.where(keep, qs, 0.0).astype(BF16)

    def idx_body(c, carry):
        k2 = k2_ref[pl.ds(pl.multiple_of(c * tk, tk), tk), :]
        score = jnp.zeros((tk, tq), F32)
        for h in range(IDX_HEADS):
            logits = _dot_nt(k2, qm_scr[h])
            score = score + wi_ref[h:h + 1, :] * jnp.maximum(logits, 0.0)
        score = jnp.where(c * tk + key_row <= q_pos, score, -jnp.inf)
        bits = pltpu.bitcast(score, jnp.int32)
        key = bits ^ ((bits >> 31) & 0x7FFFFFFF)
        key_scr[c] = key
        hi_scr[c] = (key >> HALF_BITS).astype(jnp.int16)
        lo_scr[c] = ((key & HALF_MASK) - HALF_BIAS).astype(jnp.int16)
        return carry

    lax.fori_loop(0, nck, idx_body, 0)

    def count16(mask_fn):
        def cnt_body(c, part):
            hit = mask_fn(c).astype(jnp.int16)
            for g in range(tk // PACKED_ROWS):
                part = part + hit[g * PACKED_ROWS:(g + 1) * PACKED_ROWS]
            return part

        part = lax.fori_loop(0, nck, cnt_body, jnp.zeros((PACKED_ROWS, tq), jnp.int16))
        return jnp.sum(part.astype(F32), axis=0, keepdims=True)

    def select16(src_scr, need):
        def bit_body(b, tx):
            cand_x = tx | jnp.left_shift(jnp.int32(1), HALF_BITS - 1 - b)
            cand = (cand_x - HALF_BIAS).astype(jnp.int16)
            cnt = count16(lambda c: src_scr[c] >= cand)
            return jnp.where(cnt >= need, cand_x, tx)

        return lax.fori_loop(0, HALF_BITS, bit_body, jnp.zeros((1, tq), jnp.int32)) - HALF_BIAS

    t_hi = select16(hi_scr, topk)
    t_hi16 = t_hi.astype(jnp.int16)

    def tie_body(c, carry):
        lo_scr[c] = jnp.where(hi_scr[c] == t_hi16, lo_scr[c], jnp.int16(-HALF_BIAS))
        return carry

    lax.fori_loop(0, nck, tie_body, 0)
    above = count16(lambda c: hi_scr[c] > t_hi16)
    t_lo = select16(lo_scr, topk - above)
    thr = (t_hi << HALF_BITS) | (t_lo + HALF_BIAS)

    m_scr[...] = jnp.full(m_scr.shape, NEG_INF, F32)
    l_scr[...] = jnp.zeros(l_scr.shape, F32)
    acc_scr[...] = jnp.zeros(acc_scr.shape, F32)

    def att_body(c, carry):
        kv = ckv_ref[pl.ds(pl.multiple_of(c * tk, tk), tk), :]
        kvt = ckvt_ref[c]
        sel = (key_scr[c] >= thr) & (c * tk + key_row <= q_pos)
        madd = jnp.where(sel, 0.0, NEG_INF)
        base_delta = (q0 - c * tk) // BAND_BLOCK
        items = [(h, w) for h in range(A_HEADS) for w in range(tq // qw)]

        def scores(item):
            h, w = item
            return _dot_nt(kv, ql_ref[h, w * qw:(w + 1) * qw, :])

        s_next = scores(items[0])
        for n_item, (h, w) in enumerate(items):
            qs = slice(w * qw, (w + 1) * qw)
            s_raw = s_next
            if n_item + 1 < len(items):
                s_next = scores(items[n_item + 1])
            rows = []
            for j in range(nsub_k):
                tiles = []
                for a in range(w * qw // BAND_BLOCK, (w + 1) * qw // BAND_BLOCK):
                    delta = jnp.clip(base_delta + (a - j), 0, n_delta - 1)
                    tiles.append(ba_ref[h, delta])
                rows.append(tiles[0] if len(tiles) == 1 else jnp.concatenate(tiles, axis=1))
            bias = rows[0] if nsub_k == 1 else jnp.concatenate(rows, axis=0)
            s = s_raw + bias + madd[:, qs]
            m_prev = m_scr[h:h + 1, qs]
            m_new = jnp.maximum(m_prev, jnp.max(s, axis=0, keepdims=True))
            alpha = jnp.exp2(m_prev - m_new)
            p = jnp.exp2(s - m_new)
            l_scr[h:h + 1, qs] = alpha * l_scr[h:h + 1, qs] + jnp.sum(p, axis=0, keepdims=True)
            m_scr[h:h + 1, qs] = m_new
            if n_item + 1 < len(items):
                nxt = pltpu.bitcast(s_next[:SUBLANES], jnp.uint32)
                zero = lax.shift_right_logical(lax.shift_right_logical(nxt, jnp.uint32(16)), jnp.uint32(16))
                p = jnp.concatenate([p[:SUBLANES] + zero.astype(F32), p[SUBLANES:]], axis=0)
            acc_scr[h, :, qs] = acc_scr[h, :, qs] * alpha + _dot(kvt, p.astype(BF16))
        return carry

    lax.fori_loop(0, nck, att_body, 0)

    for h in range(A_HEADS):
        o_lat_t = (acc_scr[h] / l_scr[h:h + 1, :]).astype(BF16)
        y_t = _dot(wuvt_ref[h], o_lat_t)
        o_ref[:, h * A_HEAD_DIM:(h + 1) * A_HEAD_DIM] = y_t.T.astype(o_ref.dtype)


def _dsa_attention(qi, wi_t, k2, ql, ckv, ckv_t, bias_a, wuv_t, batch, seq, tq):
    tk = tq
    n = batch * seq
    nq = seq // tq
    nck = seq // tk
    topk = min(DSA_TOPK, seq // 4)
    assert topk <= tk and seq % tq == 0 and tq % BAND_BLOCK == 0
    n_delta = bias_a.shape[1]
    kern = functools.partial(_dsa_kernel, tq=tq, tk=tk, topk=topk, n_delta=n_delta)
    once = pl.Buffered(1)
    return pl.pallas_call(
        kern,
        grid=(batch, nq),
        in_specs=[
            pl.BlockSpec((tq, IDX_Q_COLS), lambda b, i: (b * nq + i, 0)),
            pl.BlockSpec((LANES, tq), lambda b, i: (0, b * nq + i)),
            pl.BlockSpec((seq, LANES), lambda b, i: (b, 0), pipeline_mode=once),
            pl.BlockSpec((A_HEADS, tq, A_KV_RANK), lambda b, i: (0, b * nq + i, 0)),
            pl.BlockSpec((seq, A_KV_RANK), lambda b, i: (b, 0), pipeline_mode=once),
            pl.BlockSpec((nck, A_KV_RANK, tk), lambda b, i: (b, 0, 0), pipeline_mode=once),
            pl.BlockSpec(bias_a.shape, lambda b, i: (0, 0, 0, 0), pipeline_mode=once),
            pl.BlockSpec(wuv_t.shape, lambda b, i: (0, 0, 0), pipeline_mode=once),
        ],
        out_specs=pl.BlockSpec((tq, A_OUT), lambda b, i: (b * nq + i, 0)),
        out_shape=jax.ShapeDtypeStruct((n, A_OUT), BF16),
        scratch_shapes=[
            pltpu.VMEM((nck, tk, tq), jnp.int32),
            pltpu.VMEM((nck, tk, tq), jnp.int16),
            pltpu.VMEM((nck, tk, tq), jnp.int16),
            pltpu.VMEM((IDX_HEADS, tq, LANES), BF16),
            pltpu.VMEM((A_HEADS, A_KV_RANK, tq), F32),
            pltpu.VMEM((A_HEADS, tq), F32),
            pltpu.VMEM((A_HEADS, tq), F32),
        ],
        compiler_params=_params("arbitrary", "arbitrary"),
        name="dsa_attention",
    )(qi, wi_t, k2, ql, ckv, ckv_t, bias_a, wuv_t)


def _deinterleave_matrix(tm, dil):
    per = tm // dil
    i = jnp.arange(tm)
    src = (i % per) * dil + i // per
    return (src[:, None] == i[None, :]).astype(BF16)


def _mm_stream_kernel(a_ref, w_ref, p_ref, o_ref):
    z = _dot(a_ref[...], w_ref[...]).astype(BF16)
    zs = _dot(p_ref[...], z).astype(BF16)
    o_ref[0] = zs.reshape(o_ref.shape[1:])


def _proj_dilated(h, w, batch, seq, dil, tm=512):
    n, d = h.shape
    c = w.shape[1]
    if dil == 1:
        return _matmul(h, w, BF16, 1024, c // 2, "proj_dilated_qkv_g0").reshape(batch, 1, seq, c)
    per_seq = seq // tm
    return pl.pallas_call(
        _mm_stream_kernel,
        grid=(n // tm,),
        in_specs=[
            pl.BlockSpec((tm, d), lambda i: (i, 0)),
            pl.BlockSpec((d, c), lambda i: (0, 0)),
            pl.BlockSpec((tm, tm), lambda i: (0, 0)),
        ],
        out_specs=pl.BlockSpec((1, dil, tm // dil, c), lambda i: (i // per_seq, 0, i % per_seq, 0)),
        out_shape=jax.ShapeDtypeStruct((batch, dil, seq // dil, c), BF16),
        compiler_params=_params("arbitrary"),
        name=f"proj_dilated_qkv_d{dil}",
    )(h, w, _deinterleave_matrix(tm, dil))


LSE_LANES = LANES // B_HEADS_PER_GROUP


def _dil_kernel(q_ref, kc_ref, kp_ref, vc_ref, vp_ref, bias_ref, o_ref, *, nsub, steps):
    first = pl.program_id(2) == 0
    P = BAND_BLOCK
    dh = B_HEAD_DIM
    row = lax.broadcasted_iota(jnp.int32, (P, 2 * P), 0)
    u = lax.broadcasted_iota(jnp.int32, (P, 2 * P), 1)
    back = row + P - u
    band = (back >= 0) & (back <= steps)
    band_first = band & ((u >= P) | jnp.logical_not(first))
    lane_head = lax.broadcasted_iota(jnp.int32, (P, LANES), 1) // LSE_LANES
    for a in range(nsub):
        valid = band_first if a == 0 else band
        rs = slice(a * P, (a + 1) * P)
        lse_all = jnp.zeros((P, LANES), F32)
        for j in range(B_HEADS_PER_GROUP):
            cs = slice(j * dh, (j + 1) * dh)
            if a == 0:
                k_prev, v_prev = kp_ref[0, 0, :, cs], vp_ref[0, 0, :, cs]
            else:
                ps = slice((a - 1) * P, a * P)
                k_prev, v_prev = kc_ref[0, 0, ps, cs], vc_ref[0, 0, ps, cs]
            k_cat = jnp.concatenate([k_prev, kc_ref[0, 0, rs, cs]], axis=0)
            v_cat = jnp.concatenate([v_prev, vc_ref[0, 0, rs, cs]], axis=0)
            s = _dot_nt(q_ref[0, 0, rs, cs], k_cat) * (dh ** -0.5) + bias_ref[0, j]
            s = jnp.where(valid, s, NEG_INF)
            mx = jnp.max(s, axis=-1, keepdims=True)
            p = jnp.exp(s - mx)
            den = jnp.sum(p, axis=-1, keepdims=True)
            o_ref[0, 0, rs, cs] = _dot((p / den).astype(BF16), v_cat)
            lse_all = jnp.where(lane_head == j, mx + jnp.log(den), lse_all)
        o_ref[0, 0, rs, B_OUT:] = lse_all


def _dilated_group(qkv, bias_b, g, batch, seq):
    window, dil = B_GROUPS[g]
    steps = window // dil
    assert steps <= BAND_BLOCK
    m = seq // dil
    rows = min(4 * BAND_BLOCK, m)
    assert m % rows == 0 and rows % BAND_BLOCK == 0
    nsub = rows // BAND_BLOCK
    width = B_OUT

    def cur(which):
        return pl.BlockSpec((1, 1, rows, width), lambda b, r, i: (b, r, i, which))

    def prev(which):
        return pl.BlockSpec((1, 1, BAND_BLOCK, width),
                            lambda b, r, i: (b, r, jnp.maximum(i * nsub - 1, 0), which))

    return pl.pallas_call(
        functools.partial(_dil_kernel, nsub=nsub, steps=steps),
        grid=(batch, dil, m // rows),
        in_specs=[cur(0), cur(1), prev(1), cur(2), prev(2),
                  pl.BlockSpec((1, B_HEADS_PER_GROUP, BAND_BLOCK, 2 * BAND_BLOCK),
                               lambda b, r, i: (g, 0, 0, 0))],
        out_specs=pl.BlockSpec((1, 1, rows, width + LANES), lambda b, r, i: (b, r, i, 0)),
        out_shape=jax.ShapeDtypeStruct((batch, dil, m, width + LANES), F32),
        compiler_params=_params("arbitrary", "arbitrary", "arbitrary"),
        name=f"dilated_attention_g{g}",
    )(qkv, qkv, qkv, qkv, qkv, bias_b)


def _to_token_order(blk, pinv):
    x = blk.reshape(pinv.shape[0], blk.shape[-1])
    hi = x.astype(BF16)
    r1 = x - hi.astype(F32)
    mid = r1.astype(BF16)
    lo = (r1 - mid.astype(F32)).astype(BF16)
    return (_dot(pinv, hi) + _dot(pinv, mid)) + _dot(pinv, lo)


def _dil_merge_kernel(t0_ref, t1_ref, t2_ref, p1_ref, p2_ref, y_ref):
    groups = [t0_ref[...], _to_token_order(t1_ref[0], p1_ref[...]), _to_token_order(t2_ref[0], p2_ref[...])]
    tm = y_ref.shape[0]
    dh = B_HEAD_DIM
    for j in range(B_HEADS_PER_GROUP):
        lse = [jnp.broadcast_to(t[:, B_OUT + j * LSE_LANES:B_OUT + j * LSE_LANES + 1], (tm, dh))
               for t in groups]
        mx = jnp.maximum(jnp.maximum(lse[0], lse[1]), lse[2])
        e = [jnp.exp(a - mx) for a in lse]
        den = e[0] + e[1] + e[2]
        y = sum((e[g] / den) * groups[g][:, j * dh:(j + 1) * dh] for g in range(3))
        y_ref[:, j * dh:(j + 1) * dh] = y.astype(y_ref.dtype)


def _dilated_merge(packed, batch, seq, tm=512):
    n = batch * seq
    w = packed[0].shape[-1]
    per_seq = seq // tm
    dils = [d for _, d in B_GROUPS]
    assert dils[0] == 1 and len(dils) == 3

    def stream(dil):
        return pl.BlockSpec((1, dil, tm // dil, w), lambda i: (i // per_seq, 0, i % per_seq, 0))

    pspec = pl.BlockSpec((tm, tm), lambda i: (0, 0))
    return pl.pallas_call(
        _dil_merge_kernel,
        grid=(n // tm,),
        in_specs=[pl.BlockSpec((tm, w), lambda i: (i, 0)), stream(dils[1]), stream(dils[2]), pspec, pspec],
        out_specs=pl.BlockSpec((tm, B_OUT), lambda i: (i, 0)),
        out_shape=jax.ShapeDtypeStruct((n, B_OUT), BF16),
        compiler_params=_params("arbitrary"),
        name="dilated_merge",
    )(packed[0].reshape(n, w), packed[1], packed[2],
      _deinterleave_matrix(tm, dils[1]).T, _deinterleave_matrix(tm, dils[2]).T)


def _merge_kernel(h_ref, ya_ref, yb_ref, wga_ref, wgb_ref, wa_ref, wb_ref, o_ref):
    h = h_ref[...]
    ga = jax.nn.sigmoid(_dot(h, wga_ref[...]))
    gb = jax.nn.sigmoid(_dot(h, wgb_ref[...]))
    merged = ga * _dot(ya_ref[...], wa_ref[...]) + gb * _dot(yb_ref[...], wb_ref[...])
    o_ref[...] = merged.astype(o_ref.dtype)


def _gated_merge(h, ya, yb, wga, wgb, wa, wb, tm=1024, tn=512):
    n, d = h.shape

    def rows(k):
        return pl.BlockSpec((tm, k), lambda i, j: (i, 0))

    def cols(k):
        return pl.BlockSpec((k, tn), lambda i, j: (0, j))

    return pl.pallas_call(
        _merge_kernel,
        grid=(n // tm, d // tn),
        in_specs=[rows(d), rows(A_OUT), rows(B_OUT), cols(d), cols(d), cols(A_OUT), cols(B_OUT)],
        out_specs=pl.BlockSpec((tm, tn), lambda i, j: (i, j)),
        out_shape=jax.ShapeDtypeStruct((n, d), BF16),
        compiler_params=_params("arbitrary", "arbitrary"),
        name="gated_merge",
    )(h, ya, yb, wga, wgb, wa, wb)


def _res_norm_kernel(a_ref, w_ref, x_ref, gate_ref, g_ref, sc_ref, sh_ref, xo_ref, h_ref):
    xn = x_ref[...] + gate_ref[0] * _dot(a_ref[...], w_ref[...])
    xo_ref[...] = xn
    h_ref[...] = (_rms(xn) * g_ref[...] * (1.0 + sc_ref[0]) + sh_ref[0]).astype(h_ref.dtype)


def _res_final_kernel(a_ref, w_ref, x_ref, gate_ref, g_ref, o_ref):
    xn = x_ref[...] + gate_ref[0] * _dot(a_ref[...], w_ref[...])
    o_ref[...] = _rms(xn) * g_ref[...]


def _matmul_residual_norm(a, w, x, gate, g, scale, shift, seq, tm, name):
    n, k = a.shape
    d = w.shape[1]
    per = seq // tm
    row = pl.BlockSpec((tm, d), lambda i: (i, 0))
    vec = pl.BlockSpec((1, 1, d), lambda i: (i // per, 0, 0))
    in_specs = [
        pl.BlockSpec((tm, k), lambda i: (i, 0)),
        pl.BlockSpec((k, d), lambda i: (0, 0), pipeline_mode=pl.Buffered(1)),
        row, vec,
        pl.BlockSpec((1, d), lambda i: (0, 0)),
    ]
    args = [a, w, x, gate, g.reshape(1, d)]
    if scale is None:
        return pl.pallas_call(
            _res_final_kernel, grid=(n // tm,), in_specs=in_specs, out_specs=row,
            out_shape=jax.ShapeDtypeStruct((n, d), F32),
            compiler_params=_params("arbitrary"), name=name,
        )(*args)
    return pl.pallas_call(
        _res_norm_kernel, grid=(n // tm,), in_specs=in_specs + [vec, vec], out_specs=[row, row],
        out_shape=[jax.ShapeDtypeStruct((n, d), F32), jax.ShapeDtypeStruct((n, d), BF16)],
        compiler_params=_params("arbitrary"), name=name,
    )(*args, scale, shift)


def _ffn_up_kernel(h_ref, wg_ref, wu_ref, o_ref):
    h = h_ref[...]
    a = _dot(h, wg_ref[...])
    o_ref[...] = (a * jax.nn.sigmoid(a) * _dot(h, wu_ref[...])).astype(o_ref.dtype)


def _ffn_up(h, wg, wu, tm=1024, tn=512):
    n, d = h.shape
    f = wg.shape[1]
    wspec = pl.BlockSpec((d, tn), lambda i, j: (0, j))
    return pl.pallas_call(
        _ffn_up_kernel,
        grid=(n // tm, f // tn),
        in_specs=[pl.BlockSpec((tm, d), lambda i, j: (i, 0)), wspec, wspec],
        out_specs=pl.BlockSpec((tm, tn), lambda i, j: (i, j)),
        out_shape=jax.ShapeDtypeStruct((n, f), BF16),
        compiler_params=_params("arbitrary", "arbitrary"),
        name="ffn_up",
    )(h, wg, wu)


def _rope_tables(seq):
    half = IDX_ROPE_DIM // 2
    freqs = ROPE_THETA ** (-jnp.arange(half, dtype=F32) / half)
    ang = jnp.arange(seq).astype(F32)[:, None] * freqs[None, :]
    cos, sin = jnp.cos(ang), jnp.sin(ang)
    rest = IDX_HEAD_DIM - IDX_ROPE_DIM
    one = jnp.ones((seq, rest), F32)
    zr = jnp.zeros((seq, rest), F32)
    zh = jnp.zeros((seq, half), F32)
    cos_t = jnp.concatenate([cos, cos, one], axis=1)
    sin_lo = jnp.concatenate([-sin, zh, zr], axis=1)
    sin_hi = jnp.concatenate([zh, sin, zr], axis=1)
    rep = LANES // IDX_HEAD_DIM
    return tuple(jnp.tile(t, (1, rep)) for t in (cos_t, sin_lo, sin_hi))


def _layer(x, h, h_mod, next_norm, tables, bias_a, bias_b, batch, seq, w_in, kv_norm_g, idx_ln_g, idx_ln_b,
           w_uk, w_uv, w_a_up, w_b_up, w_out, norm2_g, w_ff_gate, w_ff_up, w_ff_down):
    _, _, gate1, shift2, scale2, gate2 = h_mod
    offs = [0]
    for s in IN_SIZES:
        offs.append(offs[-1] + s)
    seg = [w_in[:, offs[k]:offs[k + 1]] for k in range(len(IN_SIZES))]
    w_qa, w_kv, w_qi, w_ki, w_wi, w_qkvb, w_gate = seg
    d = w_in.shape[0]
    rep = LANES // IDX_HEAD_DIM
    w_kw = jnp.concatenate(
        [w_ki] * rep + [w_wi, jnp.zeros((d, LANES - IDX_HEADS), w_in.dtype)], axis=1).astype(BF16)
    ln_g2 = jnp.tile(idx_ln_g, rep).reshape(1, LANES)
    ln_b2 = jnp.tile(idx_ln_b, rep).reshape(1, LANES)

    ql = _proj_qlat(h, w_qa.astype(BF16), w_uk.astype(BF16))
    ckv, ckv_t = _proj_ckv(h, w_kv.astype(BF16), kv_norm_g, DSA_TILE)
    qi = _proj_qi(h, w_qi.astype(BF16), tables, seq)
    k2, wi_t = _proj_kw(h, w_kw, ln_g2, ln_b2, tables, seq)
    wuv_t = jnp.swapaxes(w_uv, 1, 2).astype(BF16)
    ya = _dsa_attention(qi, wi_t, k2, ql, ckv, ckv_t, bias_a, wuv_t, batch, seq, DSA_TILE)

    ng = len(B_GROUPS)
    packed = []
    for g, (_, dil) in enumerate(B_GROUPS):
        w_g = jnp.concatenate([w_qkvb[:, (s * ng + g) * B_OUT:(s * ng + g + 1) * B_OUT] for s in range(3)],
                              axis=1).astype(BF16)
        packed.append(_dilated_group(_proj_dilated(h, w_g, batch, seq, dil), bias_b, g, batch, seq))
    yb = _dilated_merge(packed, batch, seq)

    merged = _gated_merge(h, ya, yb, w_gate[:, :d].astype(BF16), w_gate[:, d:].astype(BF16),
                          w_a_up.astype(BF16), w_b_up.astype(BF16))
    x, h2 = _matmul_residual_norm(merged, w_out.astype(BF16), x, gate1, norm2_g, scale2, shift2, seq,
                                  512, "out_proj_residual")
    act = _ffn_up(h2, w_ff_gate.astype(BF16), w_ff_up.astype(BF16))
    return _matmul_residual_norm(act, w_ff_down.astype(BF16), x, gate2, *next_norm, seq, 256,
                                 "ffn_down_residual")


def kernel(x, c, rel_bias, w_ada, b_ada, norm1_g, w_in, kv_norm_g, idx_ln_g, idx_ln_b, w_uk, w_uv,
           w_a_up, w_b_up, w_out, norm2_g, w_ff_gate, w_ff_up, w_ff_down, final_g):
    batch, seq, d = x.shape
    depth = w_ada.shape[0]
    assert d == D_MODEL and seq % (B_GROUPS[-1][1] * BAND_BLOCK) == 0
    n = batch * seq
    rows = -(-batch // SUBLANES) * SUBLANES
    mod = _modulation(jnp.pad(c, ((0, rows - batch), (0, 0))), w_ada, b_ada)
    mod = mod[:, :batch].reshape(depth, batch, 6, 1, d)
    bias_a = _bias_a_table(rel_bias, min(seq // BAND_BLOCK, FAR_DELTA + 1))
    bias_b = _bias_b_table(rel_bias)
    tables = _rope_tables(seq)
    xf = x.reshape(n, d)
    h_mods = [[mod[l, :, k] for k in range(6)] for l in range(depth)]
    h = _norm_mod(xf, norm1_g[0], h_mods[0][1], h_mods[0][0], seq)
    for l in range(depth):
        last = l + 1 == depth
        next_norm = (final_g, None, None) if last else (norm1_g[l + 1], h_mods[l + 1][1], h_mods[l + 1][0])
        out = _layer(xf, h, h_mods[l], next_norm, tables, bias_a, bias_b, batch, seq, w_in[l], kv_norm_g[l],
                     idx_ln_g[l], idx_ln_b[l], w_uk[l], w_uv[l], w_a_up[l], w_b_up[l], w_out[l],
                     norm2_g[l], w_ff_gate[l], w_ff_up[l], w_ff_down[l])
        if last:
            return out.reshape(batch, seq, d)
        xf, h = out
```

```python
import functools
import math

import jax
import jax.numpy as jnp
from jax import lax
from jax.experimental import pallas as pl
from jax.experimental.pallas import tpu as pltpu

D_MODEL = 2048
A_HEADS = 8
A_HEAD_DIM = 128
A_KV_RANK = 256
IDX_HEADS = 16
IDX_HEAD_DIM = 64
IDX_ROPE_DIM = 32
ROPE_THETA = 10000.0
DSA_TOPK = 256
B_GROUPS = ((128, 1), (512, 4), (2048, 16))
B_HEADS_PER_GROUP = 4
B_HEAD_DIM = 128
B_HEADS = B_HEADS_PER_GROUP * len(B_GROUPS)
BAND_BLOCK = 128
REL_BUCKETS = 32
REL_MAX_DISTANCE = 2048
N_BIAS_HEADS = A_HEADS + B_HEADS
D_FF = -(-8 * D_MODEL // (3 * 256)) * 256
NORM_EPS = 1e-6
NEG_INF = -1e30

A_Q_COLS = A_HEADS * A_HEAD_DIM
IDX_Q_COLS = IDX_HEADS * IDX_HEAD_DIM
B_QKV_COLS = 3 * B_HEADS * B_HEAD_DIM
GATE_COLS = 2 * D_MODEL
IN_SIZES = (A_Q_COLS, A_KV_RANK, IDX_Q_COLS, IDX_HEAD_DIM, IDX_HEADS, B_QKV_COLS, GATE_COLS)
A_OUT = A_HEADS * A_HEAD_DIM
B_OUT = B_HEADS_PER_GROUP * B_HEAD_DIM

LANES = 128
SUBLANES = 8
VMEM_LIMIT_BYTES = 56 * 1024 * 1024

BF16 = jnp.bfloat16
F32 = jnp.float32
LOG2E = math.log2(math.e)
HALF_BITS = 16
HALF_MASK = 2 ** HALF_BITS - 1
HALF_BIAS = 2 ** (HALF_BITS - 1)
PACKED_ROWS = 2 * SUBLANES
KV_ROWS = A_KV_RANK + PACKED_ROWS

FAR_DELTA = -(-(REL_MAX_DISTANCE + BAND_BLOCK - 1) // BAND_BLOCK)
DSA_TILE = 512
DSA_STRIP = 256


def _dot(a, b):
    return jnp.dot(a, b, preferred_element_type=F32)


def _dot_nt(a, b):
    return lax.dot_general(a, b, (((1,), (1,)), ((), ())), preferred_element_type=F32)


def _params(*sem):
    return pltpu.CompilerParams(dimension_semantics=sem, vmem_limit_bytes=VMEM_LIMIT_BYTES)


def _mod_kernel(c_ref, w_ref, b_ref, o_ref):
    c = c_ref[...]
    a = (c * jax.nn.sigmoid(c)).astype(BF16)
    o_ref[0] = _dot(a, w_ref[0].astype(BF16)) + b_ref[0]


def _modulation(c_pad, w_ada, b_ada, tn=1024):
    depth, d, n6 = w_ada.shape
    rows = c_pad.shape[0]
    return pl.pallas_call(
        _mod_kernel,
        grid=(depth, n6 // tn),
        in_specs=[
            pl.BlockSpec((rows, d), lambda l, j: (0, 0)),
            pl.BlockSpec((1, d, tn), lambda l, j: (l, 0, j)),
            pl.BlockSpec((1, 1, tn), lambda l, j: (l, 0, j)),
        ],
        out_specs=pl.BlockSpec((1, rows, tn), lambda l, j: (l, 0, j)),
        out_shape=jax.ShapeDtypeStruct((depth, rows, n6), F32),
        compiler_params=_params("arbitrary", "arbitrary"),
        name="adaln_modulation",
    )(c_pad, w_ada, b_ada.reshape(depth, 1, n6))


def _bucket(dist):
    n = jnp.maximum(dist, 0)
    exact = REL_BUCKETS // 2
    nf = jnp.maximum(n, 1).astype(F32)
    large = exact + (jnp.log(nf / exact) / math.log(REL_MAX_DISTANCE / exact)
                     * (REL_BUCKETS - exact)).astype(jnp.int32)
    return jnp.where(n < exact, n, jnp.minimum(large, REL_BUCKETS - 1))


def _lookup(rb_ref, bucket, head):
    t = jnp.zeros(bucket.shape, F32)
    for k in range(REL_BUCKETS):
        t = jnp.where(bucket == k, rb_ref[k, head], t)
    return t


def _bias_a_kernel(rb_ref, o_ref):
    delta = pl.program_id(0)
    row = lax.broadcasted_iota(jnp.int32, (BAND_BLOCK, BAND_BLOCK), 0)
    col = lax.broadcasted_iota(jnp.int32, (BAND_BLOCK, BAND_BLOCK), 1)
    bucket = _bucket(delta * BAND_BLOCK + col - row)
    for h in range(A_HEADS):
        o_ref[h, 0] = _lookup(rb_ref, bucket, h) * LOG2E


def _bias_a_table(rel_bias, n_delta):
    return pl.pallas_call(
        _bias_a_kernel,
        grid=(n_delta,),
        in_specs=[pl.BlockSpec(memory_space=pltpu.SMEM)],
        out_specs=pl.BlockSpec((A_HEADS, 1, BAND_BLOCK, BAND_BLOCK), lambda d: (0, d, 0, 0)),
        out_shape=jax.ShapeDtypeStruct((A_HEADS, n_delta, BAND_BLOCK, BAND_BLOCK), F32),
        compiler_params=_params("arbitrary"),
        name="rel_bias_table_a",
    )(rel_bias)


def _bias_b_kernel(rb_ref, o_ref):
    g = pl.program_id(0)
    dil = jnp.where(g == 0, B_GROUPS[0][1], jnp.where(g == 1, B_GROUPS[1][1], B_GROUPS[2][1]))
    row = lax.broadcasted_iota(jnp.int32, (BAND_BLOCK, 2 * BAND_BLOCK), 0)
    u = lax.broadcasted_iota(jnp.int32, (BAND_BLOCK, 2 * BAND_BLOCK), 1)
    bucket = _bucket((row + BAND_BLOCK - u) * dil)
    for j in range(B_HEADS_PER_GROUP):
        o_ref[0, j] = _lookup(rb_ref, bucket, A_HEADS + g * B_HEADS_PER_GROUP + j)


def _bias_b_table(rel_bias):
    ng = len(B_GROUPS)
    return pl.pallas_call(
        _bias_b_kernel,
        grid=(ng,),
        in_specs=[pl.BlockSpec(memory_space=pltpu.SMEM)],
        out_specs=pl.BlockSpec((1, B_HEADS_PER_GROUP, BAND_BLOCK, 2 * BAND_BLOCK),
                               lambda g: (g, 0, 0, 0)),
        out_shape=jax.ShapeDtypeStruct((ng, B_HEADS_PER_GROUP, BAND_BLOCK, 2 * BAND_BLOCK), F32),
        compiler_params=_params("arbitrary"),
        name="rel_bias_table_b",
    )(rel_bias)


def _rms(x):
    return x * lax.rsqrt(jnp.mean(x * x, axis=-1, keepdims=True) + NORM_EPS)


def _norm_mod_kernel(x_ref, g_ref, sc_ref, sh_ref, o_ref):
    y = _rms(x_ref[...]) * g_ref[...]
    o_ref[...] = (y * (1.0 + sc_ref[0]) + sh_ref[0]).astype(o_ref.dtype)


def _norm_mod(x, g, scale, shift, seq, tm=512):
    n, d = x.shape
    per = seq // tm
    return pl.pallas_call(
        _norm_mod_kernel,
        grid=(n // tm,),
        in_specs=[
            pl.BlockSpec((tm, d), lambda i: (i, 0)),
            pl.BlockSpec((1, d), lambda i: (0, 0)),
            pl.BlockSpec((1, 1, d), lambda i: (i // per, 0, 0)),
            pl.BlockSpec((1, 1, d), lambda i: (i // per, 0, 0)),
        ],
        out_specs=pl.BlockSpec((tm, d), lambda i: (i, 0)),
        out_shape=jax.ShapeDtypeStruct((n, d), BF16),
        compiler_params=_params("arbitrary"),
        name="norm_modulate",
    )(x, g.reshape(1, d), scale, shift)


def _qlat_kernel(h_ref, w_ref, wuk_ref, o_ref):
    z = _dot(h_ref[...], w_ref[...])
    for hd in range(A_HEADS):
        zh = z[:, hd * A_HEAD_DIM:(hd + 1) * A_HEAD_DIM].astype(BF16)
        o_ref[hd] = (_dot(zh, wuk_ref[hd]) * (A_HEAD_DIM ** -0.5 * LOG2E)).astype(o_ref.dtype)


def _proj_qlat(h, w, wuk, tm=512):
    n, d = h.shape
    return pl.pallas_call(
        _qlat_kernel,
        grid=(n // tm,),
        in_specs=[
            pl.BlockSpec((tm, d), lambda i: (i, 0)),
            pl.BlockSpec((d, A_Q_COLS), lambda i: (0, 0)),
            pl.BlockSpec((A_HEADS, A_HEAD_DIM, A_KV_RANK), lambda i: (0, 0, 0)),
        ],
        out_specs=pl.BlockSpec((A_HEADS, tm, A_KV_RANK), lambda i: (0, i, 0)),
        out_shape=jax.ShapeDtypeStruct((A_HEADS, n, A_KV_RANK), BF16),
        compiler_params=_params("arbitrary"),
        name="proj_q_latent",
    )(h, w, wuk)


def _ckv_kernel(h_ref, w_ref, g_ref, o_ref, ot_ref, *, tk):
    z = _dot(h_ref[...], w_ref[...])
    ckv = _rms(z) * g_ref[...]
    o_ref[...] = ckv.astype(o_ref.dtype)
    ones = jnp.ones((KV_ROWS - A_KV_RANK, tk), F32)
    for s in range(ot_ref.shape[0]):
        ot_ref[s] = jnp.concatenate([ckv[s * tk:(s + 1) * tk, :].T, ones], axis=0).astype(ot_ref.dtype)


def _proj_ckv(h, w, g, tk, tm=1024):
    n, d = h.shape
    return pl.pallas_call(
        functools.partial(_ckv_kernel, tk=tk),
        grid=(n // tm,),
        in_specs=[
            pl.BlockSpec((tm, d), lambda i: (i, 0)),
            pl.BlockSpec((d, A_KV_RANK), lambda i: (0, 0)),
            pl.BlockSpec((1, A_KV_RANK), lambda i: (0, 0)),
        ],
        out_specs=[pl.BlockSpec((tm, A_KV_RANK), lambda i: (i, 0)),
                   pl.BlockSpec((tm // tk, KV_ROWS, tk), lambda i: (i, 0, 0))],
        out_shape=[jax.ShapeDtypeStruct((n, A_KV_RANK), BF16),
                   jax.ShapeDtypeStruct((n // tk, KV_ROWS, tk), BF16)],
        compiler_params=_params("arbitrary"),
        name="proj_latent_kv",
    )(h, w, g.reshape(1, A_KV_RANK))


def _rope(z, cos_t, sin_lo, sin_hi):
    half = IDX_ROPE_DIM // 2
    return (z * cos_t + pltpu.roll(z, half, 1) * sin_hi
            + pltpu.roll(z, LANES - half, 1) * sin_lo)


def _qi_kernel(h_ref, w_ref, cos_ref, slo_ref, shi_ref, o_ref):
    z = _dot(h_ref[...], w_ref[...])
    cos_t, slo, shi = cos_ref[...], slo_ref[...], shi_ref[...]
    for s in range(IDX_Q_COLS // LANES):
        zs = z[:, s * LANES:(s + 1) * LANES]
        o_ref[:, s * LANES:(s + 1) * LANES] = _rope(zs, cos_t, slo, shi).astype(o_ref.dtype)


def _proj_qi(h, w, tables, seq, tm=512):
    n, d = h.shape
    per = seq // tm
    tspec = pl.BlockSpec((tm, LANES), lambda i: (i % per, 0))
    return pl.pallas_call(
        _qi_kernel,
        grid=(n // tm,),
        in_specs=[
            pl.BlockSpec((tm, d), lambda i: (i, 0)),
            pl.BlockSpec((d, IDX_Q_COLS), lambda i: (0, 0)),
            tspec, tspec, tspec,
        ],
        out_specs=pl.BlockSpec((tm, IDX_Q_COLS), lambda i: (i, 0)),
        out_shape=jax.ShapeDtypeStruct((n, IDX_Q_COLS), BF16),
        compiler_params=_params("arbitrary"),
        name="proj_index_q",
    )(h, w, *tables)


def _kw_kernel(h_ref, w_ref, g_ref, b_ref, cos_ref, slo_ref, shi_ref, k_ref, wi_ref):
    z = _dot(h_ref[...], w_ref[...])
    zk = z[:, :LANES]
    mu = jnp.mean(zk, axis=-1, keepdims=True)
    var = jnp.mean(jnp.square(zk - mu), axis=-1, keepdims=True)
    kn = (zk - mu) * lax.rsqrt(var + NORM_EPS) * g_ref[...] + b_ref[...]
    k_ref[...] = _rope(kn, cos_ref[...], slo_ref[...], shi_ref[...]).astype(k_ref.dtype)
    wi_ref[...] = (z[:, LANES:] * (IDX_HEADS ** -0.5 * IDX_HEAD_DIM ** -0.5)).T


def _proj_kw(h, w, g2, b2, tables, seq, tm=1024):
    n, d = h.shape
    per = seq // tm
    tspec = pl.BlockSpec((tm, LANES), lambda i: (i % per, 0))
    vspec = pl.BlockSpec((1, LANES), lambda i: (0, 0))
    ospec = pl.BlockSpec((tm, LANES), lambda i: (i, 0))
    return pl.pallas_call(
        _kw_kernel,
        grid=(n // tm,),
        in_specs=[
            pl.BlockSpec((tm, d), lambda i: (i, 0)),
            pl.BlockSpec((d, 2 * LANES), lambda i: (0, 0)),
            vspec, vspec, tspec, tspec, tspec,
        ],
        out_specs=[ospec, pl.BlockSpec((LANES, tm), lambda i: (0, i))],
        out_shape=[jax.ShapeDtypeStruct((n, LANES), BF16), jax.ShapeDtypeStruct((LANES, n), F32)],
        compiler_params=_params("arbitrary"),
        name="proj_index_kw",
    )(h, w, g2, b2, *tables)


def _mm_kernel(a_ref, w_ref, o_ref):
    o_ref[...] = _dot(a_ref[...], w_ref[...]).astype(o_ref.dtype)


def _matmul(a, w, out_dtype, tm, tn, name):
    n, k = a.shape
    cols = w.shape[1]
    return pl.pallas_call(
        _mm_kernel,
        grid=(n // tm, cols // tn),
        in_specs=[pl.BlockSpec((tm, k), lambda i, j: (i, 0)), pl.BlockSpec((k, tn), lambda i, j: (0, j))],
        out_specs=pl.BlockSpec((tm, tn), lambda i, j: (i, j)),
        out_shape=jax.ShapeDtypeStruct((n, cols), out_dtype),
        compiler_params=_params("arbitrary", "arbitrary"),
        name=name,
    )(a, w)


def _dsa_kernel(qi_ref, wi_ref, k2_ref, ql_ref, ckv_ref, ckvt_ref, ba_ref, wuvt_ref, o_ref,
                key_scr, hi_scr, lo_scr, qm_scr, acc_scr, m_scr, madd_scr, tie_scr,
                *, tq, tk, topk, n_delta):
    i = pl.program_id(1)
    q0 = i * tq
    nck = (q0 + tq + tk - 1) // tk
    nsub_k = tk // BAND_BLOCK
    qw = min(tq, DSA_STRIP)
    key_row = lax.broadcasted_iota(jnp.int32, (tk, tq), 0)
    q_pos = q0 + lax.broadcasted_iota(jnp.int32, (tk, tq), 1)
    lane = lax.broadcasted_iota(jnp.int32, (tq, LANES), 1)

    for h in range(IDX_HEADS):
        qs = qi_ref[:, (h // 2) * LANES:(h // 2 + 1) * LANES].astype(F32)
        keep = (lane >= IDX_HEAD_DIM) if h % 2 else (lane < IDX_HEAD_DIM)
        qm_scr[h] = jnp.where(keep, qs, 0.0).astype(BF16)

    def idx_body(c, carry):
        k2 = k2_ref[pl.ds(pl.multiple_of(c * tk, tk), tk), :]
        score = jnp.zeros((tk, tq), F32)
        for h in range(IDX_HEADS):
            logits = _dot_nt(k2, qm_scr[h])
            score = score + wi_ref[h:h + 1, :] * jnp.maximum(logits, 0.0)
        score = jnp.where(c * tk + key_row <= q_pos, score, -jnp.inf)
        bits = jnp.where(score == 0.0, 0, pltpu.bitcast(score, jnp.int32))
        key = bits ^ ((bits >> 31) & 0x7FFFFFFF)
        key_scr[c] = key
        hi_scr[c] = (key >> HALF_BITS).astype(jnp.int16)
        lo_scr[c] = ((key & HALF_MASK) - HALF_BIAS).astype(jnp.int16)
        return carry

    lax.fori_loop(0, nck, idx_body, 0)

    def count16(mask_fn):
        def cnt_body(c, part):
            hit = mask_fn(c).astype(jnp.int16)
            for g in range(tk // PACKED_ROWS):
                part = part + hit[g * PACKED_ROWS:(g + 1) * PACKED_ROWS]
            return part

        part = lax.fori_loop(0, nck, cnt_body, jnp.zeros((PACKED_ROWS, tq), jnp.int16))
        return jnp.sum(part.astype(F32), axis=0, keepdims=True)

    def select16(src_scr, need):
        def bit_body(b, tx):
            cand_x = tx | jnp.left_shift(jnp.int32(1), HALF_BITS - 1 - b)
            cand = (cand_x - HALF_BIAS).astype(jnp.int16)
            cnt = count16(lambda c: src_scr[c] >= cand)
            return jnp.where(cnt >= need, cand_x, tx)

        return lax.fori_loop(0, HALF_BITS, bit_body, jnp.zeros((1, tq), jnp.int32)) - HALF_BIAS

    t_hi = select16(hi_scr, topk)
    t_hi16 = t_hi.astype(jnp.int16)

    def tie_body(c, carry):
        lo_scr[c] = jnp.where(hi_scr[c] == t_hi16, lo_scr[c], jnp.int16(-HALF_BIAS))
        return carry

    lax.fori_loop(0, nck, tie_body, 0)
    above = count16(lambda c: hi_scr[c] > t_hi16)
    t_lo = select16(lo_scr, topk - above)
    thr = (t_hi << HALF_BITS) | (t_lo + HALF_BIAS)

    def count32(mask_fn):
        def cnt_body(c, part):
            hit = mask_fn(key_scr[c]).astype(jnp.int32)
            return part + jnp.sum(hit.reshape(tk // SUBLANES, SUBLANES, tq), axis=0)

        part = lax.fori_loop(0, nck, cnt_body, jnp.zeros((SUBLANES, tq), jnp.int32))
        return jnp.sum(part.astype(F32), axis=0, keepdims=True)

    has_tie = jnp.max(count32(lambda key: key >= thr)) > topk
    tie_scr[0:1, :] = jnp.zeros((1, tq), F32)

    @pl.when(has_tie)
    def _():
        tie_scr[1:2, :] = topk - count32(lambda key: key > thr)

    m_scr[...] = jnp.full(m_scr.shape, NEG_INF, F32)
    acc_scr[...] = jnp.zeros(acc_scr.shape, F32)

    def att_body(c, carry):
        kv = ckv_ref[pl.ds(pl.multiple_of(c * tk, tk), tk), :]
        kvt = ckvt_ref[c]
        causal = c * tk + key_row <= q_pos

        @pl.when(jnp.logical_not(has_tie))
        def _():
            madd_scr[...] = jnp.where((key_scr[c] >= thr) & causal, 0.0, NEG_INF)

        @pl.when(has_tie)
        def _():
            key = key_scr[c]
            tied = key == thr
            below = (lax.broadcasted_iota(jnp.int32, (tk, tk), 0)
                     >= lax.broadcasted_iota(jnp.int32, (tk, tk), 1))
            tied_f = tied.astype(F32)
            rank = _dot(below.astype(F32).astype(BF16), tied_f.astype(BF16)) + tie_scr[0:1, :]
            keep = (key > thr) | (tied & (rank <= tie_scr[1:2, :]))
            madd_scr[...] = jnp.where(keep & causal, 0.0, NEG_INF)
            tie_scr[0:1, :] = tie_scr[0:1, :] + jnp.sum(tied_f, axis=0, keepdims=True)

        madd = madd_scr[...]
        base_delta = (q0 - c * tk) // BAND_BLOCK
        items = [(h, w) for h in range(A_HEADS) for w in range(tq // qw)]

        def scores(item):
            h, w = item
            return _dot_nt(kv, ql_ref[h, w * qw:(w + 1) * qw, :])

        s_next = scores(items[0])
        for n_item, (h, w) in enumerate(items):
            qs = slice(w * qw, (w + 1) * qw)
            s_raw = s_next
            if n_item + 1 < len(items):
                s_next = scores(items[n_item + 1])
            rows = []
            for j in range(nsub_k):
                tiles = []
                for a in range(w * qw // BAND_BLOCK, (w + 1) * qw // BAND_BLOCK):
                    delta = jnp.clip(base_delta + (a - j), 0, n_delta - 1)
                    tiles.append(ba_ref[h, delta])
                rows.append(tiles[0] if len(tiles) == 1 else jnp.concatenate(tiles, axis=1))
            bias = rows[0] if nsub_k == 1 else jnp.concatenate(rows, axis=0)
            s = s_raw + bias + madd[:, qs]
            m_prev = m_scr[h:h + 1, qs]
            m_new = jnp.maximum(m_prev, jnp.max(s, axis=0, keepdims=True))
            alpha = jnp.exp2(m_prev - m_new)
            p = jnp.exp2(s - m_new)
            m_scr[h:h + 1, qs] = m_new
            if n_item + 1 < len(items):
                nxt = pltpu.bitcast(s_next[:SUBLANES], jnp.uint32)
                zero = lax.shift_right_logical(lax.shift_right_logical(nxt, jnp.uint32(16)), jnp.uint32(16))
                p = jnp.concatenate([p[:SUBLANES] + zero.astype(F32), p[SUBLANES:]], axis=0)
            acc_scr[h, :, qs] = acc_scr[h, :, qs] * alpha + _dot(kvt, p.astype(BF16))
        return carry

    lax.fori_loop(0, nck, att_body, 0)

    for h in range(A_HEADS):
        o_lat_t = (acc_scr[h, :A_KV_RANK, :] / acc_scr[h, A_KV_RANK:A_KV_RANK + 1, :]).astype(BF16)
        y_t = _dot(wuvt_ref[h], o_lat_t)
        o_ref[:, h * A_HEAD_DIM:(h + 1) * A_HEAD_DIM] = y_t.T.astype(o_ref.dtype)


def _dsa_attention(qi, wi_t, k2, ql, ckv, ckv_t, bias_a, wuv_t, batch, seq, tq):
    tk = tq
    n = batch * seq
    nq = seq // tq
    nck = seq // tk
    topk = min(DSA_TOPK, seq // 4)
    assert topk <= tk and seq % tq == 0 and tq % BAND_BLOCK == 0
    n_delta = bias_a.shape[1]
    kern = functools.partial(_dsa_kernel, tq=tq, tk=tk, topk=topk, n_delta=n_delta)
    once = pl.Buffered(1)
    return pl.pallas_call(
        kern,
        grid=(batch, nq),
        in_specs=[
            pl.BlockSpec((tq, IDX_Q_COLS), lambda b, i: (b * nq + i, 0)),
            pl.BlockSpec((LANES, tq), lambda b, i: (0, b * nq + i)),
            pl.BlockSpec((seq, LANES), lambda b, i: (b, 0), pipeline_mode=once),
            pl.BlockSpec((A_HEADS, tq, A_KV_RANK), lambda b, i: (0, b * nq + i, 0)),
            pl.BlockSpec((seq, A_KV_RANK), lambda b, i: (b, 0), pipeline_mode=once),
            pl.BlockSpec((nck, KV_ROWS, tk), lambda b, i: (b, 0, 0), pipeline_mode=once),
            pl.BlockSpec(bias_a.shape, lambda b, i: (0, 0, 0, 0), pipeline_mode=once),
            pl.BlockSpec(wuv_t.shape, lambda b, i: (0, 0, 0), pipeline_mode=once),
        ],
        out_specs=pl.BlockSpec((tq, A_OUT), lambda b, i: (b * nq + i, 0)),
        out_shape=jax.ShapeDtypeStruct((n, A_OUT), BF16),
        scratch_shapes=[
            pltpu.VMEM((nck, tk, tq), jnp.int32),
            pltpu.VMEM((nck, tk, tq), jnp.int16),
            pltpu.VMEM((nck, tk, tq), jnp.int16),
            pltpu.VMEM((IDX_HEADS, tq, LANES), BF16),
            pltpu.VMEM((A_HEADS, KV_ROWS, tq), F32),
            pltpu.VMEM((A_HEADS, tq), F32),
            pltpu.VMEM((tk, tq), F32),
            pltpu.VMEM((SUBLANES, tq), F32),
        ],
        compiler_params=_params("arbitrary", "arbitrary"),
        name="dsa_attention",
    )(qi, wi_t, k2, ql, ckv, ckv_t, bias_a, wuv_t)


def _deinterleave_matrix(tm, dil):
    per = tm // dil
    i = jnp.arange(tm)
    src = (i % per) * dil + i // per
    return (src[:, None] == i[None, :]).astype(BF16)


def _mm_stream_kernel(a_ref, w_ref, p_ref, o_ref):
    z = _dot(a_ref[...], w_ref[...]).astype(BF16)
    zs = _dot(p_ref[...], z).astype(BF16)
    o_ref[0] = zs.reshape(o_ref.shape[1:])


def _proj_dilated(h, w, batch, seq, dil, tm=256):
    n, d = h.shape
    c = w.shape[1]
    if dil == 1:
        return _matmul(h, w, BF16, 1024, c // 2, "proj_dilated_qkv_g0").reshape(batch, 1, seq, c)
    per_seq = seq // tm
    return pl.pallas_call(
        _mm_stream_kernel,
        grid=(n // tm,),
        in_specs=[
            pl.BlockSpec((tm, d), lambda i: (i, 0)),
            pl.BlockSpec((d, c), lambda i: (0, 0)),
            pl.BlockSpec((tm, tm), lambda i: (0, 0)),
        ],
        out_specs=pl.BlockSpec((1, dil, tm // dil, c), lambda i: (i // per_seq, 0, i % per_seq, 0)),
        out_shape=jax.ShapeDtypeStruct((batch, dil, seq // dil, c), BF16),
        compiler_params=_params("arbitrary"),
        name=f"proj_dilated_qkv_d{dil}",
    )(h, w, _deinterleave_matrix(tm, dil))


LSE_LANES = LANES // B_HEADS_PER_GROUP


def _dil_kernel(q_ref, kc_ref, kp_ref, vc_ref, vp_ref, bias_ref, o_ref, *, nsub, steps):
    first = pl.program_id(2) == 0
    P = BAND_BLOCK
    dh = B_HEAD_DIM
    row = lax.broadcasted_iota(jnp.int32, (P, 2 * P), 0)
    u = lax.broadcasted_iota(jnp.int32, (P, 2 * P), 1)
    back = row + P - u
    band = (back >= 0) & (back <= steps)
    band_first = band & ((u >= P) | jnp.logical_not(first))
    lane_head = lax.broadcasted_iota(jnp.int32, (P, LANES), 1) // LSE_LANES
    for a in range(nsub):
        valid = band_first if a == 0 else band
        rs = slice(a * P, (a + 1) * P)
        lse_all = jnp.zeros((P, LANES), F32)
        for j in range(B_HEADS_PER_GROUP):
            cs = slice(j * dh, (j + 1) * dh)
            if a == 0:
                k_prev, v_prev = kp_ref[0, 0, :, cs], vp_ref[0, 0, :, cs]
            else:
                ps = slice((a - 1) * P, a * P)
                k_prev, v_prev = kc_ref[0, 0, ps, cs], vc_ref[0, 0, ps, cs]
            k_cat = jnp.concatenate([k_prev, kc_ref[0, 0, rs, cs]], axis=0)
            v_cat = jnp.concatenate([v_prev, vc_ref[0, 0, rs, cs]], axis=0)
            s = _dot_nt(q_ref[0, 0, rs, cs], k_cat) * (dh ** -0.5) + bias_ref[0, j]
            s = jnp.where(valid, s, NEG_INF)
            mx = jnp.max(s, axis=-1, keepdims=True)
            p = jnp.exp(s - mx)
            den = jnp.sum(p, axis=-1, keepdims=True)
            o_ref[0, 0, rs, cs] = _dot((p / den).astype(BF16), v_cat)
            lse_all = jnp.where(lane_head == j, mx + jnp.log(den), lse_all)
        o_ref[0, 0, rs, B_OUT:] = lse_all


def _dilated_group(qkv, bias_b, g, batch, seq):
    window, dil = B_GROUPS[g]
    steps = window // dil
    assert steps <= BAND_BLOCK
    m = seq // dil
    rows = min(4 * BAND_BLOCK, m)
    assert m % rows == 0 and rows % BAND_BLOCK == 0
    nsub = rows // BAND_BLOCK
    width = B_OUT

    def cur(which):
        return pl.BlockSpec((1, 1, rows, width), lambda b, r, i: (b, r, i, which))

    def prev(which):
        return pl.BlockSpec((1, 1, BAND_BLOCK, width),
                            lambda b, r, i: (b, r, jnp.maximum(i * nsub - 1, 0), which))

    return pl.pallas_call(
        functools.partial(_dil_kernel, nsub=nsub, steps=steps),
        grid=(batch, dil, m // rows),
        in_specs=[cur(0), cur(1), prev(1), cur(2), prev(2),
                  pl.BlockSpec((1, B_HEADS_PER_GROUP, BAND_BLOCK, 2 * BAND_BLOCK),
                               lambda b, r, i: (g, 0, 0, 0))],
        out_specs=pl.BlockSpec((1, 1, rows, width + LANES), lambda b, r, i: (b, r, i, 0)),
        out_shape=jax.ShapeDtypeStruct((batch, dil, m, width + LANES), F32),
        compiler_params=_params("arbitrary", "arbitrary", "arbitrary"),
        name=f"dilated_attention_g{g}",
    )(qkv, qkv, qkv, qkv, qkv, bias_b)


def _to_token_order(blk, pinv):
    x = blk.reshape(pinv.shape[0], blk.shape[-1])
    hi = x.astype(BF16)
    r1 = x - hi.astype(F32)
    mid = r1.astype(BF16)
    lo = (r1 - mid.astype(F32)).astype(BF16)
    return (_dot(pinv, hi) + _dot(pinv, mid)) + _dot(pinv, lo)


def _dil_merge_kernel(t0_ref, t1_ref, t2_ref, p1_ref, p2_ref, y_ref):
    groups = [t0_ref[...], _to_token_order(t1_ref[0], p1_ref[...]), _to_token_order(t2_ref[0], p2_ref[...])]
    tm = y_ref.shape[0]
    dh = B_HEAD_DIM
    for j in range(B_HEADS_PER_GROUP):
        lse = [jnp.broadcast_to(t[:, B_OUT + j * LSE_LANES:B_OUT + j * LSE_LANES + 1], (tm, dh))
               for t in groups]
        mx = jnp.maximum(jnp.maximum(lse[0], lse[1]), lse[2])
        e = [jnp.exp(a - mx) for a in lse]
        den = e[0] + e[1] + e[2]
        y = sum((e[g] / den) * groups[g][:, j * dh:(j + 1) * dh] for g in range(3))
        y_ref[:, j * dh:(j + 1) * dh] = y.astype(y_ref.dtype)


def _dilated_merge(packed, batch, seq, tm=256):
    n = batch * seq
    w = packed[0].shape[-1]
    per_seq = seq // tm
    dils = [d for _, d in B_GROUPS]
    assert dils[0] == 1 and len(dils) == 3

    def stream(dil):
        return pl.BlockSpec((1, dil, tm // dil, w), lambda i: (i // per_seq, 0, i % per_seq, 0))

    pspec = pl.BlockSpec((tm, tm), lambda i: (0, 0))
    return pl.pallas_call(
        _dil_merge_kernel,
        grid=(n // tm,),
        in_specs=[pl.BlockSpec((tm, w), lambda i: (i, 0)), stream(dils[1]), stream(dils[2]), pspec, pspec],
        out_specs=pl.BlockSpec((tm, B_OUT), lambda i: (i, 0)),
        out_shape=jax.ShapeDtypeStruct((n, B_OUT), BF16),
        compiler_params=_params("arbitrary"),
        name="dilated_merge",
    )(packed[0].reshape(n, w), packed[1], packed[2],
      _deinterleave_matrix(tm, dils[1]).T, _deinterleave_matrix(tm, dils[2]).T)


def _merge_kernel(h_ref, ya_ref, yb_ref, wga_ref, wgb_ref, wa_ref, wb_ref, o_ref):
    h = h_ref[...]
    ga = jax.nn.sigmoid(_dot(h, wga_ref[...]))
    gb = jax.nn.sigmoid(_dot(h, wgb_ref[...]))
    merged = ga * _dot(ya_ref[...], wa_ref[...]) + gb * _dot(yb_ref[...], wb_ref[...])
    o_ref[...] = merged.astype(o_ref.dtype)


def _gated_merge(h, ya, yb, wga, wgb, wa, wb, tm=1024, tn=512):
    n, d = h.shape

    def rows(k):
        return pl.BlockSpec((tm, k), lambda i, j: (i, 0))

    def cols(k):
        return pl.BlockSpec((k, tn), lambda i, j: (0, j))

    return pl.pallas_call(
        _merge_kernel,
        grid=(n // tm, d // tn),
        in_specs=[rows(d), rows(A_OUT), rows(B_OUT), cols(d), cols(d), cols(A_OUT), cols(B_OUT)],
        out_specs=pl.BlockSpec((tm, tn), lambda i, j: (i, j)),
        out_shape=jax.ShapeDtypeStruct((n, d), BF16),
        compiler_params=_params("arbitrary", "arbitrary"),
        name="gated_merge",
    )(h, ya, yb, wga, wgb, wa, wb)


def _res_norm_kernel(a_ref, w_ref, x_ref, gate_ref, g_ref, sc_ref, sh_ref, xo_ref, h_ref):
    xn = x_ref[...] + gate_ref[0] * _dot(a_ref[...], w_ref[...])
    xo_ref[...] = xn
    h_ref[...] = (_rms(xn) * g_ref[...] * (1.0 + sc_ref[0]) + sh_ref[0]).astype(h_ref.dtype)


def _res_final_kernel(a_ref, w_ref, x_ref, gate_ref, g_ref, o_ref):
    xn = x_ref[...] + gate_ref[0] * _dot(a_ref[...], w_ref[...])
    o_ref[...] = _rms(xn) * g_ref[...]


def _matmul_residual_norm(a, w, x, gate, g, scale, shift, seq, tm, name):
    n, k = a.shape
    d = w.shape[1]
    per = seq // tm
    row = pl.BlockSpec((tm, d), lambda i: (i, 0))
    vec = pl.BlockSpec((1, 1, d), lambda i: (i // per, 0, 0))
    in_specs = [
        pl.BlockSpec((tm, k), lambda i: (i, 0)),
        pl.BlockSpec((k, d), lambda i: (0, 0), pipeline_mode=pl.Buffered(1)),
        row, vec,
        pl.BlockSpec((1, d), lambda i: (0, 0)),
    ]
    args = [a, w, x, gate, g.reshape(1, d)]
    if scale is None:
        return pl.pallas_call(
            _res_final_kernel, grid=(n // tm,), in_specs=in_specs, out_specs=row,
            out_shape=jax.ShapeDtypeStruct((n, d), F32),
            compiler_params=_params("arbitrary"), name=name,
        )(*args)
    return pl.pallas_call(
        _res_norm_kernel, grid=(n // tm,), in_specs=in_specs + [vec, vec], out_specs=[row, row],
        out_shape=[jax.ShapeDtypeStruct((n, d), F32), jax.ShapeDtypeStruct((n, d), BF16)],
        compiler_params=_params("arbitrary"), name=name,
    )(*args, scale, shift)


def _ffn_up_kernel(h_ref, wg_ref, wu_ref, o_ref):
    h = h_ref[...]
    a = _dot(h, wg_ref[...])
    o_ref[...] = (a * jax.nn.sigmoid(a) * _dot(h, wu_ref[...])).astype(o_ref.dtype)


def _ffn_up(h, wg, wu, tm=1024, tn=512):
    n, d = h.shape
    f = wg.shape[1]
    wspec = pl.BlockSpec((d, tn), lambda i, j: (0, j))
    return pl.pallas_call(
        _ffn_up_kernel,
        grid=(n // tm, f // tn),
        in_specs=[pl.BlockSpec((tm, d), lambda i, j: (i, 0)), wspec, wspec],
        out_specs=pl.BlockSpec((tm, tn), lambda i, j: (i, j)),
        out_shape=jax.ShapeDtypeStruct((n, f), BF16),
        compiler_params=_params("arbitrary", "arbitrary"),
        name="ffn_up",
    )(h, wg, wu)


def _rope_tables(seq):
    half = IDX_ROPE_DIM // 2
    freqs = ROPE_THETA ** (-jnp.arange(half, dtype=F32) / half)
    ang = jnp.arange(seq).astype(F32)[:, None] * freqs[None, :]
    cos, sin = jnp.cos(ang), jnp.sin(ang)
    rest = IDX_HEAD_DIM - IDX_ROPE_DIM
    one = jnp.ones((seq, rest), F32)
    zr = jnp.zeros((seq, rest), F32)
    zh = jnp.zeros((seq, half), F32)
    cos_t = jnp.concatenate([cos, cos, one], axis=1)
    sin_lo = jnp.concatenate([-sin, zh, zr], axis=1)
    sin_hi = jnp.concatenate([zh, sin, zr], axis=1)
    rep = LANES // IDX_HEAD_DIM
    return tuple(jnp.tile(t, (1, rep)) for t in (cos_t, sin_lo, sin_hi))


def _layer(x, h, h_mod, next_norm, tables, bias_a, bias_b, batch, seq, w_in, kv_norm_g, idx_ln_g, idx_ln_b,
           w_uk, w_uv, w_a_up, w_b_up, w_out, norm2_g, w_ff_gate, w_ff_up, w_ff_down):
    _, _, gate1, shift2, scale2, gate2 = h_mod
    offs = [0]
    for s in IN_SIZES:
        offs.append(offs[-1] + s)
    seg = [w_in[:, offs[k]:offs[k + 1]] for k in range(len(IN_SIZES))]
    w_qa, w_kv, w_qi, w_ki, w_wi, w_qkvb, w_gate = seg
    d = w_in.shape[0]
    rep = LANES // IDX_HEAD_DIM
    w_kw = jnp.concatenate(
        [w_ki] * rep + [w_wi, jnp.zeros((d, LANES - IDX_HEADS), w_in.dtype)], axis=1).astype(BF16)
    ln_g2 = jnp.tile(idx_ln_g, rep).reshape(1, LANES)
    ln_b2 = jnp.tile(idx_ln_b, rep).reshape(1, LANES)

    ql = _proj_qlat(h, w_qa.astype(BF16), w_uk.astype(BF16))
    ckv, ckv_t = _proj_ckv(h, w_kv.astype(BF16), kv_norm_g, DSA_TILE)
    qi = _proj_qi(h, w_qi.astype(BF16), tables, seq)
    k2, wi_t = _proj_kw(h, w_kw, ln_g2, ln_b2, tables, seq)
    wuv_t = jnp.swapaxes(w_uv, 1, 2).astype(BF16)
    ya = _dsa_attention(qi, wi_t, k2, ql, ckv, ckv_t, bias_a, wuv_t, batch, seq, DSA_TILE)

    ng = len(B_GROUPS)
    packed = []
    for g, (_, dil) in enumerate(B_GROUPS):
        w_g = jnp.concatenate([w_qkvb[:, (s * ng + g) * B_OUT:(s * ng + g + 1) * B_OUT] for s in range(3)],
                              axis=1).astype(BF16)
        packed.append(_dilated_group(_proj_dilated(h, w_g, batch, seq, dil), bias_b, g, batch, seq))
    yb = _dilated_merge(packed, batch, seq)

    merged = _gated_merge(h, ya, yb, w_gate[:, :d].astype(BF16), w_gate[:, d:].astype(BF16),
                          w_a_up.astype(BF16), w_b_up.astype(BF16))
    x, h2 = _matmul_residual_norm(merged, w_out.astype(BF16), x, gate1, norm2_g, scale2, shift2, seq,
                                  512, "out_proj_residual")
    act = _ffn_up(h2, w_ff_gate.astype(BF16), w_ff_up.astype(BF16))
    return _matmul_residual_norm(act, w_ff_down.astype(BF16), x, gate2, *next_norm, seq, 256,
                                 "ffn_down_residual")


def kernel(x, c, rel_bias, w_ada, b_ada, norm1_g, w_in, kv_norm_g, idx_ln_g, idx_ln_b, w_uk, w_uv,
           w_a_up, w_b_up, w_out, norm2_g, w_ff_gate, w_ff_up, w_ff_down, final_g):
    batch, seq, d = x.shape
    depth = w_ada.shape[0]
    assert d == D_MODEL and seq % (B_GROUPS[-1][1] * BAND_BLOCK) == 0
    n = batch * seq
    rows = -(-batch // SUBLANES) * SUBLANES
    mod = _modulation(jnp.pad(c, ((0, rows - batch), (0, 0))), w_ada, b_ada)
    mod = mod[:, :batch].reshape(depth, batch, 6, 1, d)
    bias_a = _bias_a_table(rel_bias, min(seq // BAND_BLOCK, FAR_DELTA + 1))
    bias_b = _bias_b_table(rel_bias)
    tables = _rope_tables(seq)
    xf = x.reshape(n, d)
    h_mods = [[mod[l, :, k] for k in range(6)] for l in range(depth)]
    h = _norm_mod(xf, norm1_g[0], h_mods[0][1], h_mods[0][0], seq)
    for l in range(depth):
        last = l + 1 == depth
        next_norm = (final_g, None, None) if last else (norm1_g[l + 1], h_mods[l + 1][1], h_mods[l + 1][0])
        out = _layer(xf, h, h_mods[l], next_norm, tables, bias_a, bias_b, batch, seq, w_in[l], kv_norm_g[l],
                     idx_ln_g[l], idx_ln_b[l], w_uk[l], w_uv[l], w_a_up[l], w_b_up[l], w_out[l],
                     norm2_g[l], w_ff_gate[l], w_ff_up[l], w_ff_down[l])
        if last:
            return out.reshape(batch, seq, d)
        xf, h = out
```

```python
import functools
import math

import jax
import jax.numpy as jnp
from jax import lax
from jax.experimental import pallas as pl
from jax.experimental.pallas import tpu as pltpu

D_MODEL = 2048
A_HEADS = 8
A_HEAD_DIM = 128
A_KV_RANK = 256
IDX_HEADS = 16
IDX_HEAD_DIM = 64
IDX_ROPE_DIM = 32
ROPE_THETA = 10000.0
DSA_TOPK = 256
B_GROUPS = ((128, 1), (512, 4), (2048, 16))
B_HEADS_PER_GROUP = 4
B_HEAD_DIM = 128
B_HEADS = B_HEADS_PER_GROUP * len(B_GROUPS)
BAND_BLOCK = 128
REL_BUCKETS = 32
REL_MAX_DISTANCE = 2048
N_BIAS_HEADS = A_HEADS + B_HEADS
D_FF = -(-8 * D_MODEL // (3 * 256)) * 256
NORM_EPS = 1e-6
NEG_INF = -1e30

A_Q_COLS = A_HEADS * A_HEAD_DIM
IDX_Q_COLS = IDX_HEADS * IDX_HEAD_DIM
B_QKV_COLS = 3 * B_HEADS * B_HEAD_DIM
GATE_COLS = 2 * D_MODEL
IN_SIZES = (A_Q_COLS, A_KV_RANK, IDX_Q_COLS, IDX_HEAD_DIM, IDX_HEADS, B_QKV_COLS, GATE_COLS)
A_OUT = A_HEADS * A_HEAD_DIM
B_OUT = B_HEADS_PER_GROUP * B_HEAD_DIM

LANES = 128
SUBLANES = 8
VMEM_LIMIT_BYTES = 56 * 1024 * 1024

BF16 = jnp.bfloat16
F32 = jnp.float32
LOG2E = math.log2(math.e)
HALF_BITS = 16
HALF_MASK = 2 ** HALF_BITS - 1
HALF_BIAS = 2 ** (HALF_BITS - 1)
PACKED_ROWS = 2 * SUBLANES
KV_ROWS = A_KV_RANK + PACKED_ROWS

FAR_DELTA = -(-(REL_MAX_DISTANCE + BAND_BLOCK - 1) // BAND_BLOCK)
DSA_TILE = 512
DSA_STRIP = 256


def _dot(a, b):
    return jnp.dot(a, b, preferred_element_type=F32)


def _dot_nt(a, b):
    return lax.dot_general(a, b, (((1,), (1,)), ((), ())), preferred_element_type=F32)


def _params(*sem):
    return pltpu.CompilerParams(dimension_semantics=sem, vmem_limit_bytes=VMEM_LIMIT_BYTES)


def _mod_kernel(c_ref, w_ref, b_ref, o_ref):
    c = c_ref[...]
    a = (c * jax.nn.sigmoid(c)).astype(BF16)
    o_ref[0] = _dot(a, w_ref[0].astype(BF16)) + b_ref[0]


def _modulation(c_pad, w_ada, b_ada, tn=2048):
    depth, d, n6 = w_ada.shape
    rows = c_pad.shape[0]
    return pl.pallas_call(
        _mod_kernel,
        grid=(depth, n6 // tn),
        in_specs=[
            pl.BlockSpec((rows, d), lambda l, j: (0, 0)),
            pl.BlockSpec((1, d, tn), lambda l, j: (l, 0, j)),
            pl.BlockSpec((1, 1, tn), lambda l, j: (l, 0, j)),
        ],
        out_specs=pl.BlockSpec((1, rows, tn), lambda l, j: (l, 0, j)),
        out_shape=jax.ShapeDtypeStruct((depth, rows, n6), F32),
        compiler_params=_params("arbitrary", "arbitrary"),
        name="adaln_modulation",
    )(c_pad, w_ada, b_ada.reshape(depth, 1, n6))


def _bucket(dist):
    n = jnp.maximum(dist, 0)
    exact = REL_BUCKETS // 2
    nf = jnp.maximum(n, 1).astype(F32)
    large = exact + (jnp.log(nf / exact) / math.log(REL_MAX_DISTANCE / exact)
                     * (REL_BUCKETS - exact)).astype(jnp.int32)
    return jnp.where(n < exact, n, jnp.minimum(large, REL_BUCKETS - 1))


def _lookup(rb_ref, bucket, head):
    t = jnp.zeros(bucket.shape, F32)
    for k in range(REL_BUCKETS):
        t = jnp.where(bucket == k, rb_ref[k, head], t)
    return t


def _bias_a_kernel(rb_ref, o_ref):
    delta = pl.program_id(0)
    row = lax.broadcasted_iota(jnp.int32, (BAND_BLOCK, BAND_BLOCK), 0)
    col = lax.broadcasted_iota(jnp.int32, (BAND_BLOCK, BAND_BLOCK), 1)
    bucket = _bucket(delta * BAND_BLOCK + col - row)
    for h in range(A_HEADS):
        o_ref[h, 0] = _lookup(rb_ref, bucket, h) * LOG2E


def _bias_a_table(rel_bias, n_delta):
    return pl.pallas_call(
        _bias_a_kernel,
        grid=(n_delta,),
        in_specs=[pl.BlockSpec(memory_space=pltpu.SMEM)],
        out_specs=pl.BlockSpec((A_HEADS, 1, BAND_BLOCK, BAND_BLOCK), lambda d: (0, d, 0, 0)),
        out_shape=jax.ShapeDtypeStruct((A_HEADS, n_delta, BAND_BLOCK, BAND_BLOCK), F32),
        compiler_params=_params("arbitrary"),
        name="rel_bias_table_a",
    )(rel_bias)


def _bias_b_kernel(rb_ref, o_ref):
    g = pl.program_id(0)
    dil = jnp.where(g == 0, B_GROUPS[0][1], jnp.where(g == 1, B_GROUPS[1][1], B_GROUPS[2][1]))
    row = lax.broadcasted_iota(jnp.int32, (BAND_BLOCK, 2 * BAND_BLOCK), 0)
    u = lax.broadcasted_iota(jnp.int32, (BAND_BLOCK, 2 * BAND_BLOCK), 1)
    bucket = _bucket((row + BAND_BLOCK - u) * dil)
    for j in range(B_HEADS_PER_GROUP):
        o_ref[0, j] = _lookup(rb_ref, bucket, A_HEADS + g * B_HEADS_PER_GROUP + j)


def _bias_b_table(rel_bias):
    ng = len(B_GROUPS)
    return pl.pallas_call(
        _bias_b_kernel,
        grid=(ng,),
        in_specs=[pl.BlockSpec(memory_space=pltpu.SMEM)],
        out_specs=pl.BlockSpec((1, B_HEADS_PER_GROUP, BAND_BLOCK, 2 * BAND_BLOCK),
                               lambda g: (g, 0, 0, 0)),
        out_shape=jax.ShapeDtypeStruct((ng, B_HEADS_PER_GROUP, BAND_BLOCK, 2 * BAND_BLOCK), F32),
        compiler_params=_params("arbitrary"),
        name="rel_bias_table_b",
    )(rel_bias)


def _rms(x):
    return x * lax.rsqrt(jnp.mean(x * x, axis=-1, keepdims=True) + NORM_EPS)


def _norm_mod_kernel(x_ref, g_ref, sc_ref, sh_ref, o_ref):
    y = _rms(x_ref[...]) * g_ref[...]
    o_ref[...] = (y * (1.0 + sc_ref[0]) + sh_ref[0]).astype(o_ref.dtype)


def _norm_mod(x, g, scale, shift, seq, tm=512):
    n, d = x.shape
    per = seq // tm
    return pl.pallas_call(
        _norm_mod_kernel,
        grid=(n // tm,),
        in_specs=[
            pl.BlockSpec((tm, d), lambda i: (i, 0)),
            pl.BlockSpec((1, d), lambda i: (0, 0)),
            pl.BlockSpec((1, 1, d), lambda i: (i // per, 0, 0)),
            pl.BlockSpec((1, 1, d), lambda i: (i // per, 0, 0)),
        ],
        out_specs=pl.BlockSpec((tm, d), lambda i: (i, 0)),
        out_shape=jax.ShapeDtypeStruct((n, d), BF16),
        compiler_params=_params("arbitrary"),
        name="norm_modulate",
    )(x, g.reshape(1, d), scale, shift)


def _qlat_kernel(h_ref, w_ref, wuk_ref, o_ref):
    z = _dot(h_ref[...], w_ref[...])
    for hd in range(A_HEADS):
        zh = z[:, hd * A_HEAD_DIM:(hd + 1) * A_HEAD_DIM].astype(BF16)
        o_ref[hd] = (_dot(zh, wuk_ref[hd]) * (A_HEAD_DIM ** -0.5 * LOG2E)).astype(o_ref.dtype)


def _proj_qlat(h, w, wuk, tm=512):
    n, d = h.shape
    return pl.pallas_call(
        _qlat_kernel,
        grid=(n // tm,),
        in_specs=[
            pl.BlockSpec((tm, d), lambda i: (i, 0)),
            pl.BlockSpec((d, A_Q_COLS), lambda i: (0, 0)),
            pl.BlockSpec((A_HEADS, A_HEAD_DIM, A_KV_RANK), lambda i: (0, 0, 0)),
        ],
        out_specs=pl.BlockSpec((A_HEADS, tm, A_KV_RANK), lambda i: (0, i, 0)),
        out_shape=jax.ShapeDtypeStruct((A_HEADS, n, A_KV_RANK), BF16),
        compiler_params=_params("arbitrary"),
        name="proj_q_latent",
    )(h, w, wuk)


def _ckv_kernel(h_ref, w_ref, g_ref, o_ref, ot_ref, *, tk):
    z = _dot(h_ref[...], w_ref[...])
    ckv = _rms(z) * g_ref[...]
    o_ref[...] = ckv.astype(o_ref.dtype)
    ones = jnp.ones((KV_ROWS - A_KV_RANK, tk), F32)
    for s in range(ot_ref.shape[0]):
        ot_ref[s] = jnp.concatenate([ckv[s * tk:(s + 1) * tk, :].T, ones], axis=0).astype(ot_ref.dtype)


def _proj_ckv(h, w, g, tk, tm=1024):
    n, d = h.shape
    return pl.pallas_call(
        functools.partial(_ckv_kernel, tk=tk),
        grid=(n // tm,),
        in_specs=[
            pl.BlockSpec((tm, d), lambda i: (i, 0)),
            pl.BlockSpec((d, A_KV_RANK), lambda i: (0, 0)),
            pl.BlockSpec((1, A_KV_RANK), lambda i: (0, 0)),
        ],
        out_specs=[pl.BlockSpec((tm, A_KV_RANK), lambda i: (i, 0)),
                   pl.BlockSpec((tm // tk, KV_ROWS, tk), lambda i: (i, 0, 0))],
        out_shape=[jax.ShapeDtypeStruct((n, A_KV_RANK), BF16),
                   jax.ShapeDtypeStruct((n // tk, KV_ROWS, tk), BF16)],
        compiler_params=_params("arbitrary"),
        name="proj_latent_kv",
    )(h, w, g.reshape(1, A_KV_RANK))


def _rope(z, cos_t, sin_lo, sin_hi):
    half = IDX_ROPE_DIM // 2
    return (z * cos_t + pltpu.roll(z, half, 1) * sin_hi
            + pltpu.roll(z, LANES - half, 1) * sin_lo)


def _qi_kernel(h_ref, w_ref, cos_ref, slo_ref, shi_ref, o_ref):
    z = _dot(h_ref[...], w_ref[...])
    cos_t, slo, shi = cos_ref[...], slo_ref[...], shi_ref[...]
    for s in range(IDX_Q_COLS // LANES):
        zs = z[:, s * LANES:(s + 1) * LANES]
        o_ref[:, s * LANES:(s + 1) * LANES] = _rope(zs, cos_t, slo, shi).astype(o_ref.dtype)


def _proj_qi(h, w, tables, seq, tm=512):
    n, d = h.shape
    per = seq // tm
    tspec = pl.BlockSpec((tm, LANES), lambda i: (i % per, 0))
    return pl.pallas_call(
        _qi_kernel,
        grid=(n // tm,),
        in_specs=[
            pl.BlockSpec((tm, d), lambda i: (i, 0)),
            pl.BlockSpec((d, IDX_Q_COLS), lambda i: (0, 0)),
            tspec, tspec, tspec,
        ],
        out_specs=pl.BlockSpec((tm, IDX_Q_COLS), lambda i: (i, 0)),
        out_shape=jax.ShapeDtypeStruct((n, IDX_Q_COLS), BF16),
        compiler_params=_params("arbitrary"),
        name="proj_index_q",
    )(h, w, *tables)


def _kw_kernel(h_ref, w_ref, g_ref, b_ref, cos_ref, slo_ref, shi_ref, k_ref, wi_ref):
    z = _dot(h_ref[...], w_ref[...])
    zk = z[:, :LANES]
    mu = jnp.mean(zk, axis=-1, keepdims=True)
    var = jnp.mean(jnp.square(zk - mu), axis=-1, keepdims=True)
    kn = (zk - mu) * lax.rsqrt(var + NORM_EPS) * g_ref[...] + b_ref[...]
    k_ref[...] = _rope(kn, cos_ref[...], slo_ref[...], shi_ref[...]).astype(k_ref.dtype)
    wi_ref[...] = (z[:, LANES:] * (IDX_HEADS ** -0.5 * IDX_HEAD_DIM ** -0.5)).T


def _proj_kw(h, w, g2, b2, tables, seq, tm=1024):
    n, d = h.shape
    per = seq // tm
    tspec = pl.BlockSpec((tm, LANES), lambda i: (i % per, 0))
    vspec = pl.BlockSpec((1, LANES), lambda i: (0, 0))
    ospec = pl.BlockSpec((tm, LANES), lambda i: (i, 0))
    return pl.pallas_call(
        _kw_kernel,
        grid=(n // tm,),
        in_specs=[
            pl.BlockSpec((tm, d), lambda i: (i, 0)),
            pl.BlockSpec((d, 2 * LANES), lambda i: (0, 0)),
            vspec, vspec, tspec, tspec, tspec,
        ],
        out_specs=[ospec, pl.BlockSpec((LANES, tm), lambda i: (0, i))],
        out_shape=[jax.ShapeDtypeStruct((n, LANES), BF16), jax.ShapeDtypeStruct((LANES, n), F32)],
        compiler_params=_params("arbitrary"),
        name="proj_index_kw",
    )(h, w, g2, b2, *tables)


def _mm_kernel(a_ref, w_ref, o_ref):
    o_ref[...] = _dot(a_ref[...], w_ref[...]).astype(o_ref.dtype)


def _matmul(a, w, out_dtype, tm, tn, name):
    n, k = a.shape
    cols = w.shape[1]
    return pl.pallas_call(
        _mm_kernel,
        grid=(n // tm, cols // tn),
        in_specs=[pl.BlockSpec((tm, k), lambda i, j: (i, 0)), pl.BlockSpec((k, tn), lambda i, j: (0, j))],
        out_specs=pl.BlockSpec((tm, tn), lambda i, j: (i, j)),
        out_shape=jax.ShapeDtypeStruct((n, cols), out_dtype),
        compiler_params=_params("arbitrary", "arbitrary"),
        name=name,
    )(a, w)


def _dsa_kernel(qi_ref, wi_ref, k2_ref, ql_ref, ckv_ref, ckvt_ref, ba_ref, wuvt_ref, o_ref,
                key_scr, hi_scr, lo_scr, qm_scr, acc_scr, m_scr, madd_scr, tie_scr,
                *, tq, tk, topk, n_delta):
    i = pl.program_id(1)
    q0 = i * tq
    nck = (q0 + tq + tk - 1) // tk
    nsub_k = tk // BAND_BLOCK
    qw = min(tq, DSA_STRIP)
    key_row = lax.broadcasted_iota(jnp.int32, (tk, tq), 0)
    q_pos = q0 + lax.broadcasted_iota(jnp.int32, (tk, tq), 1)
    lane = lax.broadcasted_iota(jnp.int32, (tq, LANES), 1)

    for h in range(IDX_HEADS):
        qs = qi_ref[:, (h // 2) * LANES:(h // 2 + 1) * LANES].astype(F32)
        keep = (lane >= IDX_HEAD_DIM) if h % 2 else (lane < IDX_HEAD_DIM)
        qm_scr[h] = jnp.where(keep, qs, 0.0).astype(BF16)

    def idx_body(c, carry):
        k2 = k2_ref[pl.ds(pl.multiple_of(c * tk, tk), tk), :]
        score = jnp.zeros((tk, tq), F32)
        for h in range(IDX_HEADS):
            logits = _dot_nt(k2, qm_scr[h])
            score = score + wi_ref[h:h + 1, :] * jnp.maximum(logits, 0.0)
        score = jnp.where(c * tk + key_row <= q_pos, score, -jnp.inf)
        bits = jnp.where(score == 0.0, 0, pltpu.bitcast(score, jnp.int32))
        key = bits ^ ((bits >> 31) & 0x7FFFFFFF)
        key_scr[c] = key
        hi_scr[c] = (key >> HALF_BITS).astype(jnp.int16)
        lo_scr[c] = ((key & HALF_MASK) - HALF_BIAS).astype(jnp.int16)
        return carry

    lax.fori_loop(0, nck, idx_body, 0)

    def count16(mask_fn):
        def cnt_body(c, part):
            hit = mask_fn(c).astype(jnp.int16)
            for g in range(tk // PACKED_ROWS):
                part = part + hit[g * PACKED_ROWS:(g + 1) * PACKED_ROWS]
            return part

        part = lax.fori_loop(0, nck, cnt_body, jnp.zeros((PACKED_ROWS, tq), jnp.int16))
        return jnp.sum(part.astype(F32), axis=0, keepdims=True)

    def select16(src_scr, need, n_all):
        def bit_body(b, carry):
            tx, n_ge = carry
            cand_x = tx | jnp.left_shift(jnp.int32(1), HALF_BITS - 1 - b)
            cand = (cand_x - HALF_BIAS).astype(jnp.int16)
            cnt = count16(lambda c: src_scr[c] >= cand)
            take = cnt >= need
            return jnp.where(take, cand_x, tx), jnp.where(take, cnt, n_ge)

        tx, n_ge = lax.fori_loop(0, HALF_BITS, bit_body, (jnp.zeros((1, tq), jnp.int32), n_all))
        return tx - HALF_BIAS, n_ge

    n_keys = jnp.full((1, tq), 1.0, F32) * (nck * tk).astype(F32)
    t_hi, n_hi_ge = select16(hi_scr, topk, n_keys)
    t_hi16 = t_hi.astype(jnp.int16)

    def tie_body(c, carry):
        lo_scr[c] = jnp.where(hi_scr[c] == t_hi16, lo_scr[c], jnp.int16(-HALF_BIAS))
        return carry

    lax.fori_loop(0, nck, tie_body, 0)
    above = count16(lambda c: hi_scr[c] > t_hi16)
    t_lo, n_lo_ge = select16(lo_scr, topk - above, n_hi_ge - above)
    thr = (t_hi << HALF_BITS) | (t_lo + HALF_BIAS)
    n_ge = above + n_lo_ge

    def count32(mask_fn):
        def cnt_body(c, part):
            hit = mask_fn(key_scr[c]).astype(jnp.int32)
            return part + jnp.sum(hit.reshape(tk // SUBLANES, SUBLANES, tq), axis=0)

        part = lax.fori_loop(0, nck, cnt_body, jnp.zeros((SUBLANES, tq), jnp.int32))
        return jnp.sum(part.astype(F32), axis=0, keepdims=True)

    has_tie = jnp.max(n_ge) > topk
    tie_scr[0:1, :] = jnp.zeros((1, tq), F32)

    @pl.when(has_tie)
    def _():
        tie_scr[1:2, :] = topk - count32(lambda key: key > thr)

    m_scr[...] = jnp.full(m_scr.shape, NEG_INF, F32)
    acc_scr[...] = jnp.zeros(acc_scr.shape, F32)

    def att_body(c, carry):
        kv = ckv_ref[pl.ds(pl.multiple_of(c * tk, tk), tk), :]
        kvt = ckvt_ref[c]
        causal = c * tk + key_row <= q_pos

        @pl.when(jnp.logical_not(has_tie))
        def _():
            madd_scr[...] = jnp.where((key_scr[c] >= thr) & causal, 0.0, NEG_INF)

        @pl.when(has_tie)
        def _():
            key = key_scr[c]
            tied = key == thr
            below = (lax.broadcasted_iota(jnp.int32, (tk, tk), 0)
                     >= lax.broadcasted_iota(jnp.int32, (tk, tk), 1))
            tied_f = tied.astype(F32)
            rank = _dot(below.astype(F32).astype(BF16), tied_f.astype(BF16)) + tie_scr[0:1, :]
            keep = (key > thr) | (tied & (rank <= tie_scr[1:2, :]))
            madd_scr[...] = jnp.where(keep & causal, 0.0, NEG_INF)
            tie_scr[0:1, :] = tie_scr[0:1, :] + jnp.sum(tied_f, axis=0, keepdims=True)

        madd = madd_scr[...]
        base_delta = (q0 - c * tk) // BAND_BLOCK
        items = [(h, w) for h in range(A_HEADS) for w in range(tq // qw)]

        def scores(item):
            h, w = item
            return _dot_nt(kv, ql_ref[h, w * qw:(w + 1) * qw, :])

        s_next = scores(items[0])
        for n_item, (h, w) in enumerate(items):
            qs = slice(w * qw, (w + 1) * qw)
            s_raw = s_next
            if n_item + 1 < len(items):
                s_next = scores(items[n_item + 1])
            rows = []
            for j in range(nsub_k):
                tiles = []
                for a in range(w * qw // BAND_BLOCK, (w + 1) * qw // BAND_BLOCK):
                    delta = jnp.clip(base_delta + (a - j), 0, n_delta - 1)
                    tiles.append(ba_ref[h, delta])
                rows.append(tiles[0] if len(tiles) == 1 else jnp.concatenate(tiles, axis=1))
            bias = rows[0] if nsub_k == 1 else jnp.concatenate(rows, axis=0)
            s = s_raw + bias + madd[:, qs]
            m_prev = m_scr[h:h + 1, qs]
            m_new = jnp.maximum(m_prev, jnp.max(s, axis=0, keepdims=True))
            alpha = jnp.exp2(m_prev - m_new)
            p = jnp.exp2(s - m_new)
            m_scr[h:h + 1, qs] = m_new
            if n_item + 1 < len(items):
                nxt = pltpu.bitcast(s_next[:SUBLANES], jnp.uint32)
                zero = lax.shift_right_logical(lax.shift_right_logical(nxt, jnp.uint32(16)), jnp.uint32(16))
                p = jnp.concatenate([p[:SUBLANES] + zero.astype(F32), p[SUBLANES:]], axis=0)
            acc_scr[h, :, qs] = acc_scr[h, :, qs] * alpha + _dot(kvt, p.astype(BF16))
        return carry

    lax.fori_loop(0, nck, att_body, 0)

    for h in range(A_HEADS):
        o_lat_t = (acc_scr[h, :A_KV_RANK, :] / acc_scr[h, A_KV_RANK:A_KV_RANK + 1, :]).astype(BF16)
        y_t = _dot(wuvt_ref[h], o_lat_t)
        o_ref[:, h * A_HEAD_DIM:(h + 1) * A_HEAD_DIM] = y_t.T.astype(o_ref.dtype)


def _dsa_attention(qi, wi_t, k2, ql, ckv, ckv_t, bias_a, wuv_t, batch, seq, tq):
    tk = tq
    n = batch * seq
    nq = seq // tq
    nck = seq // tk
    topk = min(DSA_TOPK, seq // 4)
    assert topk <= tk and seq % tq == 0 and tq % BAND_BLOCK == 0
    n_delta = bias_a.shape[1]
    kern = functools.partial(_dsa_kernel, tq=tq, tk=tk, topk=topk, n_delta=n_delta)
    once = pl.Buffered(1)
    return pl.pallas_call(
        kern,
        grid=(batch, nq),
        in_specs=[
            pl.BlockSpec((tq, IDX_Q_COLS), lambda b, i: (b * nq + i, 0)),
            pl.BlockSpec((LANES, tq), lambda b, i: (0, b * nq + i)),
            pl.BlockSpec((seq, LANES), lambda b, i: (b, 0), pipeline_mode=once),
            pl.BlockSpec((A_HEADS, tq, A_KV_RANK), lambda b, i: (0, b * nq + i, 0)),
            pl.BlockSpec((seq, A_KV_RANK), lambda b, i: (b, 0), pipeline_mode=once),
            pl.BlockSpec((nck, KV_ROWS, tk), lambda b, i: (b, 0, 0), pipeline_mode=once),
            pl.BlockSpec(bias_a.shape, lambda b, i: (0, 0, 0, 0), pipeline_mode=once),
            pl.BlockSpec(wuv_t.shape, lambda b, i: (0, 0, 0), pipeline_mode=once),
        ],
        out_specs=pl.BlockSpec((tq, A_OUT), lambda b, i: (b * nq + i, 0)),
        out_shape=jax.ShapeDtypeStruct((n, A_OUT), BF16),
        scratch_shapes=[
            pltpu.VMEM((nck, tk, tq), jnp.int32),
            pltpu.VMEM((nck, tk, tq), jnp.int16),
            pltpu.VMEM((nck, tk, tq), jnp.int16),
            pltpu.VMEM((IDX_HEADS, tq, LANES), BF16),
            pltpu.VMEM((A_HEADS, KV_ROWS, tq), F32),
            pltpu.VMEM((A_HEADS, tq), F32),
            pltpu.VMEM((tk, tq), F32),
            pltpu.VMEM((SUBLANES, tq), F32),
        ],
        compiler_params=_params("arbitrary", "arbitrary"),
        name="dsa_attention",
    )(qi, wi_t, k2, ql, ckv, ckv_t, bias_a, wuv_t)


def _deinterleave_matrix(tm, dil):
    per = tm // dil
    i = jnp.arange(tm)
    src = (i % per) * dil + i // per
    return (src[:, None] == i[None, :]).astype(BF16)


def _mm_stream_kernel(a_ref, w_ref, p_ref, o_ref):
    z = _dot(a_ref[...], w_ref[...]).astype(BF16)
    zs = _dot(p_ref[...], z).astype(BF16)
    o_ref[0] = zs.reshape(o_ref.shape[1:])


def _proj_dilated(h, w, batch, seq, dil, tm=256):
    n, d = h.shape
    c = w.shape[1]
    if dil == 1:
        return _matmul(h, w, BF16, 1024, c // 2, "proj_dilated_qkv_g0").reshape(batch, 1, seq, c)
    per_seq = seq // tm
    return pl.pallas_call(
        _mm_stream_kernel,
        grid=(n // tm,),
        in_specs=[
            pl.BlockSpec((tm, d), lambda i: (i, 0)),
            pl.BlockSpec((d, c), lambda i: (0, 0)),
            pl.BlockSpec((tm, tm), lambda i: (0, 0)),
        ],
        out_specs=pl.BlockSpec((1, dil, tm // dil, c), lambda i: (i // per_seq, 0, i % per_seq, 0)),
        out_shape=jax.ShapeDtypeStruct((batch, dil, seq // dil, c), BF16),
        compiler_params=_params("arbitrary"),
        name=f"proj_dilated_qkv_d{dil}",
    )(h, w, _deinterleave_matrix(tm, dil))


LSE_LANES = LANES // B_HEADS_PER_GROUP


def _dil_kernel(q_ref, kc_ref, kp_ref, vc_ref, vp_ref, bias_ref, o_ref, *, nsub, steps):
    first = pl.program_id(2) == 0
    P = BAND_BLOCK
    dh = B_HEAD_DIM
    row = lax.broadcasted_iota(jnp.int32, (P, 2 * P), 0)
    u = lax.broadcasted_iota(jnp.int32, (P, 2 * P), 1)
    back = row + P - u
    band = (back >= 0) & (back <= steps)
    band_first = band & ((u >= P) | jnp.logical_not(first))
    lane_head = lax.broadcasted_iota(jnp.int32, (P, LANES), 1) // LSE_LANES
    ones_cols = jnp.ones((2 * P, dh), BF16)
    for a in range(nsub):
        valid = band_first if a == 0 else band
        rs = slice(a * P, (a + 1) * P)
        lse_all = jnp.zeros((P, LANES), F32)
        for j in range(B_HEADS_PER_GROUP):
            cs = slice(j * dh, (j + 1) * dh)
            if a == 0:
                k_prev, v_prev = kp_ref[0, 0, :, cs], vp_ref[0, 0, :, cs]
            else:
                ps = slice((a - 1) * P, a * P)
                k_prev, v_prev = kc_ref[0, 0, ps, cs], vc_ref[0, 0, ps, cs]
            k_cat = jnp.concatenate([k_prev, kc_ref[0, 0, rs, cs]], axis=0)
            v_cat = jnp.concatenate([v_prev, vc_ref[0, 0, rs, cs]], axis=0)
            s = _dot_nt(q_ref[0, 0, rs, cs], k_cat) * (dh ** -0.5) + bias_ref[0, j]
            s = jnp.where(valid, s, NEG_INF)
            mx = jnp.max(s, axis=-1, keepdims=True)
            p = jnp.exp(s - mx).astype(BF16)
            pv = _dot(p, jnp.concatenate([v_cat, ones_cols], axis=1))
            den = pv[:, dh:]
            o_ref[0, 0, rs, cs] = pv[:, :dh] / den
            lse_all = jnp.where(lane_head == j, mx + jnp.log(den), lse_all)
        o_ref[0, 0, rs, B_OUT:] = lse_all


def _dilated_group(qkv, bias_b, g, batch, seq):
    window, dil = B_GROUPS[g]
    steps = window // dil
    assert steps <= BAND_BLOCK
    m = seq // dil
    rows = min(4 * BAND_BLOCK, m)
    assert m % rows == 0 and rows % BAND_BLOCK == 0
    nsub = rows // BAND_BLOCK
    width = B_OUT

    def cur(which):
        return pl.BlockSpec((1, 1, rows, width), lambda b, r, i: (b, r, i, which))

    def prev(which):
        return pl.BlockSpec((1, 1, BAND_BLOCK, width),
                            lambda b, r, i: (b, r, jnp.maximum(i * nsub - 1, 0), which))

    return pl.pallas_call(
        functools.partial(_dil_kernel, nsub=nsub, steps=steps),
        grid=(batch, dil, m // rows),
        in_specs=[cur(0), cur(1), prev(1), cur(2), prev(2),
                  pl.BlockSpec((1, B_HEADS_PER_GROUP, BAND_BLOCK, 2 * BAND_BLOCK),
                               lambda b, r, i: (g, 0, 0, 0))],
        out_specs=pl.BlockSpec((1, 1, rows, width + LANES), lambda b, r, i: (b, r, i, 0)),
        out_shape=jax.ShapeDtypeStruct((batch, dil, m, width + LANES), F32),
        compiler_params=_params("arbitrary", "arbitrary", "arbitrary"),
        name=f"dilated_attention_g{g}",
    )(qkv, qkv, qkv, qkv, qkv, bias_b)


def _to_token_order(blk, pinv):
    x = blk.reshape(pinv.shape[0], blk.shape[-1])
    hi = x.astype(BF16)
    r1 = x - hi.astype(F32)
    mid = r1.astype(BF16)
    lo = (r1 - mid.astype(F32)).astype(BF16)
    return (_dot(pinv, hi) + _dot(pinv, mid)) + _dot(pinv, lo)


def _dil_merge_kernel(t0_ref, t1_ref, t2_ref, p1_ref, p2_ref, y_ref):
    groups = [t0_ref[...], _to_token_order(t1_ref[0], p1_ref[...]), _to_token_order(t2_ref[0], p2_ref[...])]
    tm = y_ref.shape[0]
    dh = B_HEAD_DIM
    for j in range(B_HEADS_PER_GROUP):
        lse = [jnp.broadcast_to(t[:, B_OUT + j * LSE_LANES:B_OUT + j * LSE_LANES + 1], (tm, dh))
               for t in groups]
        mx = jnp.maximum(jnp.maximum(lse[0], lse[1]), lse[2])
        e = [jnp.exp(a - mx) for a in lse]
        den = e[0] + e[1] + e[2]
        y = sum((e[g] / den) * groups[g][:, j * dh:(j + 1) * dh] for g in range(3))
        y_ref[:, j * dh:(j + 1) * dh] = y.astype(y_ref.dtype)


def _dilated_merge(packed, batch, seq, tm=256):
    n = batch * seq
    w = packed[0].shape[-1]
    per_seq = seq // tm
    dils = [d for _, d in B_GROUPS]
    assert dils[0] == 1 and len(dils) == 3

    def stream(dil):
        return pl.BlockSpec((1, dil, tm // dil, w), lambda i: (i // per_seq, 0, i % per_seq, 0))

    pspec = pl.BlockSpec((tm, tm), lambda i: (0, 0))
    return pl.pallas_call(
        _dil_merge_kernel,
        grid=(n // tm,),
        in_specs=[pl.BlockSpec((tm, w), lambda i: (i, 0)), stream(dils[1]), stream(dils[2]), pspec, pspec],
        out_specs=pl.BlockSpec((tm, B_OUT), lambda i: (i, 0)),
        out_shape=jax.ShapeDtypeStruct((n, B_OUT), BF16),
        compiler_params=_params("arbitrary"),
        name="dilated_merge",
    )(packed[0].reshape(n, w), packed[1], packed[2],
      _deinterleave_matrix(tm, dils[1]).T, _deinterleave_matrix(tm, dils[2]).T)


def _merge_kernel(h_ref, ya_ref, yb_ref, wga_ref, wgb_ref, wa_ref, wb_ref, o_ref):
    h = h_ref[...]
    ga = jax.nn.sigmoid(_dot(h, wga_ref[...]))
    gb = jax.nn.sigmoid(_dot(h, wgb_ref[...]))
    merged = ga * _dot(ya_ref[...], wa_ref[...]) + gb * _dot(yb_ref[...], wb_ref[...])
    o_ref[...] = merged.astype(o_ref.dtype)


def _gated_merge(h, ya, yb, wga, wgb, wa, wb, tm=1024, tn=512):
    n, d = h.shape

    def rows(k):
        return pl.BlockSpec((tm, k), lambda i, j: (i, 0))

    def cols(k):
        return pl.BlockSpec((k, tn), lambda i, j: (0, j))

    return pl.pallas_call(
        _merge_kernel,
        grid=(n // tm, d // tn),
        in_specs=[rows(d), rows(A_OUT), rows(B_OUT), cols(d), cols(d), cols(A_OUT), cols(B_OUT)],
        out_specs=pl.BlockSpec((tm, tn), lambda i, j: (i, j)),
        out_shape=jax.ShapeDtypeStruct((n, d), BF16),
        compiler_params=_params("arbitrary", "arbitrary"),
        name="gated_merge",
    )(h, ya, yb, wga, wgb, wa, wb)


def _res_norm_kernel(a_ref, w_ref, x_ref, gate_ref, g_ref, sc_ref, sh_ref, xo_ref, h_ref):
    xn = x_ref[...] + gate_ref[0] * _dot(a_ref[...], w_ref[...])
    xo_ref[...] = xn
    h_ref[...] = (_rms(xn) * g_ref[...] * (1.0 + sc_ref[0]) + sh_ref[0]).astype(h_ref.dtype)


def _res_final_kernel(a_ref, w_ref, x_ref, gate_ref, g_ref, o_ref):
    xn = x_ref[...] + gate_ref[0] * _dot(a_ref[...], w_ref[...])
    o_ref[...] = _rms(xn) * g_ref[...]


def _matmul_residual_norm(a, w, x, gate, g, scale, shift, seq, tm, name):
    n, k = a.shape
    d = w.shape[1]
    per = seq // tm
    row = pl.BlockSpec((tm, d), lambda i: (i, 0))
    vec = pl.BlockSpec((1, 1, d), lambda i: (i // per, 0, 0))
    in_specs = [
        pl.BlockSpec((tm, k), lambda i: (i, 0)),
        pl.BlockSpec((k, d), lambda i: (0, 0), pipeline_mode=pl.Buffered(1)),
        row, vec,
        pl.BlockSpec((1, d), lambda i: (0, 0)),
    ]
    args = [a, w, x, gate, g.reshape(1, d)]
    if scale is None:
        return pl.pallas_call(
            _res_final_kernel, grid=(n // tm,), in_specs=in_specs, out_specs=row,
            out_shape=jax.ShapeDtypeStruct((n, d), F32),
            compiler_params=_params("arbitrary"), name=name,
        )(*args)
    return pl.pallas_call(
        _res_norm_kernel, grid=(n // tm,), in_specs=in_specs + [vec, vec], out_specs=[row, row],
        out_shape=[jax.ShapeDtypeStruct((n, d), F32), jax.ShapeDtypeStruct((n, d), BF16)],
        compiler_params=_params("arbitrary"), name=name,
    )(*args, scale, shift)


def _ffn_up_kernel(h_ref, wg_ref, wu_ref, o_ref, wg_scr, wu_scr):
    @pl.when(pl.program_id(1) == 0)
    def _():
        wg_scr[...] = wg_ref[...].astype(BF16)
        wu_scr[...] = wu_ref[...].astype(BF16)

    h = h_ref[...]
    a = _dot(h, wg_scr[...])
    o_ref[...] = (a * jax.nn.sigmoid(a) * _dot(h, wu_scr[...])).astype(o_ref.dtype)


def _ffn_up(h, wg, wu, tm=1024, tn=512):
    n, d = h.shape
    f = wg.shape[1]
    wspec = pl.BlockSpec((d, tn), lambda j, i: (0, j))
    return pl.pallas_call(
        _ffn_up_kernel,
        grid=(f // tn, n // tm),
        in_specs=[pl.BlockSpec((tm, d), lambda j, i: (i, 0)), wspec, wspec],
        out_specs=pl.BlockSpec((tm, tn), lambda j, i: (i, j)),
        out_shape=jax.ShapeDtypeStruct((n, f), BF16),
        scratch_shapes=[pltpu.VMEM((d, tn), BF16), pltpu.VMEM((d, tn), BF16)],
        compiler_params=_params("arbitrary", "arbitrary"),
        name="ffn_up",
    )(h, wg, wu)


def _rope_tables(seq):
    half = IDX_ROPE_DIM // 2
    freqs = ROPE_THETA ** (-jnp.arange(half, dtype=F32) / half)
    ang = jnp.arange(seq).astype(F32)[:, None] * freqs[None, :]
    cos, sin = jnp.cos(ang), jnp.sin(ang)
    rest = IDX_HEAD_DIM - IDX_ROPE_DIM
    one = jnp.ones((seq, rest), F32)
    zr = jnp.zeros((seq, rest), F32)
    zh = jnp.zeros((seq, half), F32)
    cos_t = jnp.concatenate([cos, cos, one], axis=1)
    sin_lo = jnp.concatenate([-sin, zh, zr], axis=1)
    sin_hi = jnp.concatenate([zh, sin, zr], axis=1)
    rep = LANES // IDX_HEAD_DIM
    return tuple(jnp.tile(t, (1, rep)) for t in (cos_t, sin_lo, sin_hi))


def _layer(x, h, h_mod, next_norm, tables, bias_a, bias_b, batch, seq, w_in, kv_norm_g, idx_ln_g, idx_ln_b,
           w_uk, w_uv, w_a_up, w_b_up, w_out, norm2_g, w_ff_gate, w_ff_up, w_ff_down):
    _, _, gate1, shift2, scale2, gate2 = h_mod
    offs = [0]
    for s in IN_SIZES:
        offs.append(offs[-1] + s)
    seg = [w_in[:, offs[k]:offs[k + 1]] for k in range(len(IN_SIZES))]
    w_qa, w_kv, w_qi, w_ki, w_wi, w_qkvb, w_gate = seg
    d = w_in.shape[0]
    rep = LANES // IDX_HEAD_DIM
    w_kw = jnp.concatenate(
        [w_ki] * rep + [w_wi, jnp.zeros((d, LANES - IDX_HEADS), w_in.dtype)], axis=1).astype(BF16)
    ln_g2 = jnp.tile(idx_ln_g, rep).reshape(1, LANES)
    ln_b2 = jnp.tile(idx_ln_b, rep).reshape(1, LANES)

    ql = _proj_qlat(h, w_qa.astype(BF16), w_uk.astype(BF16))
    ckv, ckv_t = _proj_ckv(h, w_kv.astype(BF16), kv_norm_g, DSA_TILE)
    qi = _proj_qi(h, w_qi.astype(BF16), tables, seq)
    k2, wi_t = _proj_kw(h, w_kw, ln_g2, ln_b2, tables, seq)
    wuv_t = jnp.swapaxes(w_uv, 1, 2).astype(BF16)
    ya = _dsa_attention(qi, wi_t, k2, ql, ckv, ckv_t, bias_a, wuv_t, batch, seq, DSA_TILE)

    ng = len(B_GROUPS)
    packed = []
    for g, (_, dil) in enumerate(B_GROUPS):
        w_g = jnp.concatenate([w_qkvb[:, (s * ng + g) * B_OUT:(s * ng + g + 1) * B_OUT] for s in range(3)],
                              axis=1).astype(BF16)
        packed.append(_dilated_group(_proj_dilated(h, w_g, batch, seq, dil), bias_b, g, batch, seq))
    yb = _dilated_merge(packed, batch, seq)

    merged = _gated_merge(h, ya, yb, w_gate[:, :d].astype(BF16), w_gate[:, d:].astype(BF16),
                          w_a_up.astype(BF16), w_b_up.astype(BF16))
    x, h2 = _matmul_residual_norm(merged, w_out.astype(BF16), x, gate1, norm2_g, scale2, shift2, seq,
                                  512, "out_proj_residual")
    act = _ffn_up(h2, w_ff_gate, w_ff_up)
    return _matmul_residual_norm(act, w_ff_down.astype(BF16), x, gate2, *next_norm, seq, 256,
                                 "ffn_down_residual")


def kernel(x, c, rel_bias, w_ada, b_ada, norm1_g, w_in, kv_norm_g, idx_ln_g, idx_ln_b, w_uk, w_uv,
           w_a_up, w_b_up, w_out, norm2_g, w_ff_gate, w_ff_up, w_ff_down, final_g):
    batch, seq, d = x.shape
    depth = w_ada.shape[0]
    assert d == D_MODEL and seq % (B_GROUPS[-1][1] * BAND_BLOCK) == 0
    n = batch * seq
    rows = -(-batch // SUBLANES) * SUBLANES
    mod = _modulation(jnp.pad(c, ((0, rows - batch), (0, 0))), w_ada, b_ada)
    mod = mod[:, :batch].reshape(depth, batch, 6, 1, d)
    bias_a = _bias_a_table(rel_bias, min(seq // BAND_BLOCK, FAR_DELTA + 1))
    bias_b = _bias_b_table(rel_bias)
    tables = _rope_tables(seq)
    xf = x.reshape(n, d)
    h_mods = [[mod[l, :, k] for k in range(6)] for l in range(depth)]
    h = _norm_mod(xf, norm1_g[0], h_mods[0][1], h_mods[0][0], seq)
    for l in range(depth):
        last = l + 1 == depth
        next_norm = (final_g, None, None) if last else (norm1_g[l + 1], h_mods[l + 1][1], h_mods[l + 1][0])
        out = _layer(xf, h, h_mods[l], next_norm, tables, bias_a, bias_b, batch, seq, w_in[l], kv_norm_g[l],
                     idx_ln_g[l], idx_ln_b[l], w_uk[l], w_uv[l], w_a_up[l], w_b_up[l], w_out[l],
                     norm2_g[l], w_ff_gate[l], w_ff_up[l], w_ff_down[l])
        if last:
            return out.reshape(batch, seq, d)
        xf, h = out
```

```python
import functools
import math

import jax
import jax.numpy as jnp
from jax import lax
from jax.experimental import pallas as pl
from jax.experimental.pallas import tpu as pltpu

D_MODEL = 2048
A_HEADS = 8
A_HEAD_DIM = 128
A_KV_RANK = 256
IDX_HEADS = 16
IDX_HEAD_DIM = 64
IDX_ROPE_DIM = 32
ROPE_THETA = 10000.0
DSA_TOPK = 256
B_GROUPS = ((128, 1), (512, 4), (2048, 16))
B_HEADS_PER_GROUP = 4
B_HEAD_DIM = 128
B_HEADS = B_HEADS_PER_GROUP * len(B_GROUPS)
BAND_BLOCK = 128
REL_BUCKETS = 32
REL_MAX_DISTANCE = 2048
N_BIAS_HEADS = A_HEADS + B_HEADS
D_FF = -(-8 * D_MODEL // (3 * 256)) * 256
NORM_EPS = 1e-6
NEG_INF = -1e30

A_Q_COLS = A_HEADS * A_HEAD_DIM
IDX_Q_COLS = IDX_HEADS * IDX_HEAD_DIM
B_QKV_COLS = 3 * B_HEADS * B_HEAD_DIM
GATE_COLS = 2 * D_MODEL
IN_SIZES = (A_Q_COLS, A_KV_RANK, IDX_Q_COLS, IDX_HEAD_DIM, IDX_HEADS, B_QKV_COLS, GATE_COLS)
A_OUT = A_HEADS * A_HEAD_DIM
B_OUT = B_HEADS_PER_GROUP * B_HEAD_DIM

LANES = 128
SUBLANES = 8
VMEM_LIMIT_BYTES = 56 * 1024 * 1024

BF16 = jnp.bfloat16
F32 = jnp.float32
LOG2E = math.log2(math.e)
HALF_BITS = 16
HALF_MASK = 2 ** HALF_BITS - 1
HALF_BIAS = 2 ** (HALF_BITS - 1)
PACKED_ROWS = 2 * SUBLANES
KV_ROWS = A_KV_RANK + PACKED_ROWS

FAR_DELTA = -(-(REL_MAX_DISTANCE + BAND_BLOCK - 1) // BAND_BLOCK)
DSA_TILE = 512
DSA_STRIP = 256


def _dot(a, b):
    return jnp.dot(a, b, preferred_element_type=F32)


def _dot_nt(a, b):
    return lax.dot_general(a, b, (((1,), (1,)), ((), ())), preferred_element_type=F32)


def _params(*sem):
    return pltpu.CompilerParams(dimension_semantics=sem, vmem_limit_bytes=VMEM_LIMIT_BYTES)


def _mod_kernel(c_ref, w_ref, b_ref, o_ref):
    c = c_ref[...]
    a = (c * jax.nn.sigmoid(c)).astype(BF16)
    o_ref[0] = _dot(a, w_ref[0].astype(BF16)) + b_ref[0]


def _modulation(c_pad, w_ada, b_ada, tn=2048):
    depth, d, n6 = w_ada.shape
    rows = c_pad.shape[0]
    return pl.pallas_call(
        _mod_kernel,
        grid=(depth, n6 // tn),
        in_specs=[
            pl.BlockSpec((rows, d), lambda l, j: (0, 0)),
            pl.BlockSpec((1, d, tn), lambda l, j: (l, 0, j)),
            pl.BlockSpec((1, 1, tn), lambda l, j: (l, 0, j)),
        ],
        out_specs=pl.BlockSpec((1, rows, tn), lambda l, j: (l, 0, j)),
        out_shape=jax.ShapeDtypeStruct((depth, rows, n6), F32),
        compiler_params=_params("arbitrary", "arbitrary"),
        name="adaln_modulation",
    )(c_pad, w_ada, b_ada.reshape(depth, 1, n6))


def _bucket(dist):
    n = jnp.maximum(dist, 0)
    exact = REL_BUCKETS // 2
    nf = jnp.maximum(n, 1).astype(F32)
    large = exact + (jnp.log(nf / exact) / math.log(REL_MAX_DISTANCE / exact)
                     * (REL_BUCKETS - exact)).astype(jnp.int32)
    return jnp.where(n < exact, n, jnp.minimum(large, REL_BUCKETS - 1))


def _lookup(rb_ref, bucket, head):
    t = jnp.zeros(bucket.shape, F32)
    for k in range(REL_BUCKETS):
        t = jnp.where(bucket == k, rb_ref[k, head], t)
    return t


def _bias_a_kernel(rb_ref, o_ref):
    delta = pl.program_id(0)
    row = lax.broadcasted_iota(jnp.int32, (BAND_BLOCK, BAND_BLOCK), 0)
    col = lax.broadcasted_iota(jnp.int32, (BAND_BLOCK, BAND_BLOCK), 1)
    bucket = _bucket(delta * BAND_BLOCK + col - row)
    for h in range(A_HEADS):
        o_ref[h, 0] = _lookup(rb_ref, bucket, h) * LOG2E


def _bias_a_table(rel_bias, n_delta):
    return pl.pallas_call(
        _bias_a_kernel,
        grid=(n_delta,),
        in_specs=[pl.BlockSpec(memory_space=pltpu.SMEM)],
        out_specs=pl.BlockSpec((A_HEADS, 1, BAND_BLOCK, BAND_BLOCK), lambda d: (0, d, 0, 0)),
        out_shape=jax.ShapeDtypeStruct((A_HEADS, n_delta, BAND_BLOCK, BAND_BLOCK), F32),
        compiler_params=_params("arbitrary"),
        name="rel_bias_table_a",
    )(rel_bias)


def _bias_b_kernel(rb_ref, o_ref):
    g = pl.program_id(0)
    dil = jnp.where(g == 0, B_GROUPS[0][1], jnp.where(g == 1, B_GROUPS[1][1], B_GROUPS[2][1]))
    row = lax.broadcasted_iota(jnp.int32, (BAND_BLOCK, 2 * BAND_BLOCK), 0)
    u = lax.broadcasted_iota(jnp.int32, (BAND_BLOCK, 2 * BAND_BLOCK), 1)
    bucket = _bucket((row + BAND_BLOCK - u) * dil)
    for j in range(B_HEADS_PER_GROUP):
        o_ref[0, j] = _lookup(rb_ref, bucket, A_HEADS + g * B_HEADS_PER_GROUP + j)


def _bias_b_table(rel_bias):
    ng = len(B_GROUPS)
    return pl.pallas_call(
        _bias_b_kernel,
        grid=(ng,),
        in_specs=[pl.BlockSpec(memory_space=pltpu.SMEM)],
        out_specs=pl.BlockSpec((1, B_HEADS_PER_GROUP, BAND_BLOCK, 2 * BAND_BLOCK),
                               lambda g: (g, 0, 0, 0)),
        out_shape=jax.ShapeDtypeStruct((ng, B_HEADS_PER_GROUP, BAND_BLOCK, 2 * BAND_BLOCK), F32),
        compiler_params=_params("arbitrary"),
        name="rel_bias_table_b",
    )(rel_bias)


def _rms(x):
    return x * lax.rsqrt(jnp.mean(x * x, axis=-1, keepdims=True) + NORM_EPS)


def _norm_mod_kernel(x_ref, g_ref, sc_ref, sh_ref, o_ref):
    y = _rms(x_ref[...]) * g_ref[...]
    o_ref[...] = (y * (1.0 + sc_ref[0]) + sh_ref[0]).astype(o_ref.dtype)


def _norm_mod(x, g, scale, shift, seq, tm=512):
    n, d = x.shape
    per = seq // tm
    return pl.pallas_call(
        _norm_mod_kernel,
        grid=(n // tm,),
        in_specs=[
            pl.BlockSpec((tm, d), lambda i: (i, 0)),
            pl.BlockSpec((1, d), lambda i: (0, 0)),
            pl.BlockSpec((1, 1, d), lambda i: (i // per, 0, 0)),
            pl.BlockSpec((1, 1, d), lambda i: (i // per, 0, 0)),
        ],
        out_specs=pl.BlockSpec((tm, d), lambda i: (i, 0)),
        out_shape=jax.ShapeDtypeStruct((n, d), BF16),
        compiler_params=_params("arbitrary"),
        name="norm_modulate",
    )(x, g.reshape(1, d), scale, shift)


def _qlat_kernel(h_ref, w_ref, wuk_ref, o_ref):
    z = _dot(h_ref[...], w_ref[...])
    for hd in range(A_HEADS):
        zh = z[:, hd * A_HEAD_DIM:(hd + 1) * A_HEAD_DIM].astype(BF16)
        o_ref[hd] = (_dot(zh, wuk_ref[hd]) * (A_HEAD_DIM ** -0.5 * LOG2E)).astype(o_ref.dtype)


def _proj_qlat(h, w, wuk, tm=512):
    n, d = h.shape
    return pl.pallas_call(
        _qlat_kernel,
        grid=(n // tm,),
        in_specs=[
            pl.BlockSpec((tm, d), lambda i: (i, 0)),
            pl.BlockSpec((d, A_Q_COLS), lambda i: (0, 0)),
            pl.BlockSpec((A_HEADS, A_HEAD_DIM, A_KV_RANK), lambda i: (0, 0, 0)),
        ],
        out_specs=pl.BlockSpec((A_HEADS, tm, A_KV_RANK), lambda i: (0, i, 0)),
        out_shape=jax.ShapeDtypeStruct((A_HEADS, n, A_KV_RANK), BF16),
        compiler_params=_params("arbitrary"),
        name="proj_q_latent",
    )(h, w, wuk)


def _ckv_kernel(h_ref, w_ref, g_ref, o_ref, ot_ref, *, tk):
    z = _dot(h_ref[...], w_ref[...])
    ckv = _rms(z) * g_ref[...]
    o_ref[...] = ckv.astype(o_ref.dtype)
    ones = jnp.ones((KV_ROWS - A_KV_RANK, tk), F32)
    for s in range(ot_ref.shape[0]):
        ot_ref[s] = jnp.concatenate([ckv[s * tk:(s + 1) * tk, :].T, ones], axis=0).astype(ot_ref.dtype)


def _proj_ckv(h, w, g, tk, tm=1024):
    n, d = h.shape
    return pl.pallas_call(
        functools.partial(_ckv_kernel, tk=tk),
        grid=(n // tm,),
        in_specs=[
            pl.BlockSpec((tm, d), lambda i: (i, 0)),
            pl.BlockSpec((d, A_KV_RANK), lambda i: (0, 0)),
            pl.BlockSpec((1, A_KV_RANK), lambda i: (0, 0)),
        ],
        out_specs=[pl.BlockSpec((tm, A_KV_RANK), lambda i: (i, 0)),
                   pl.BlockSpec((tm // tk, KV_ROWS, tk), lambda i: (i, 0, 0))],
        out_shape=[jax.ShapeDtypeStruct((n, A_KV_RANK), BF16),
                   jax.ShapeDtypeStruct((n // tk, KV_ROWS, tk), BF16)],
        compiler_params=_params("arbitrary"),
        name="proj_latent_kv",
    )(h, w, g.reshape(1, A_KV_RANK))


def _rope(z, cos_t, sin_lo, sin_hi):
    half = IDX_ROPE_DIM // 2
    return (z * cos_t + pltpu.roll(z, half, 1) * sin_hi
            + pltpu.roll(z, LANES - half, 1) * sin_lo)


def _qi_kernel(h_ref, w_ref, cos_ref, slo_ref, shi_ref, o_ref):
    z = _dot(h_ref[...], w_ref[...])
    cos_t, slo, shi = cos_ref[...], slo_ref[...], shi_ref[...]
    for s in range(IDX_Q_COLS // LANES):
        zs = z[:, s * LANES:(s + 1) * LANES]
        o_ref[:, s * LANES:(s + 1) * LANES] = _rope(zs, cos_t, slo, shi).astype(o_ref.dtype)


def _proj_qi(h, w, tables, seq, tm=512):
    n, d = h.shape
    per = seq // tm
    tspec = pl.BlockSpec((tm, LANES), lambda i: (i % per, 0))
    return pl.pallas_call(
        _qi_kernel,
        grid=(n // tm,),
        in_specs=[
            pl.BlockSpec((tm, d), lambda i: (i, 0)),
            pl.BlockSpec((d, IDX_Q_COLS), lambda i: (0, 0)),
            tspec, tspec, tspec,
        ],
        out_specs=pl.BlockSpec((tm, IDX_Q_COLS), lambda i: (i, 0)),
        out_shape=jax.ShapeDtypeStruct((n, IDX_Q_COLS), BF16),
        compiler_params=_params("arbitrary"),
        name="proj_index_q",
    )(h, w, *tables)


def _kw_kernel(h_ref, w_ref, g_ref, b_ref, cos_ref, slo_ref, shi_ref, k_ref, wi_ref):
    z = _dot(h_ref[...], w_ref[...])
    zk = z[:, :LANES]
    mu = jnp.mean(zk, axis=-1, keepdims=True)
    var = jnp.mean(jnp.square(zk - mu), axis=-1, keepdims=True)
    kn = (zk - mu) * lax.rsqrt(var + NORM_EPS) * g_ref[...] + b_ref[...]
    k_ref[...] = _rope(kn, cos_ref[...], slo_ref[...], shi_ref[...]).astype(k_ref.dtype)
    wi_ref[...] = (z[:, LANES:] * (IDX_HEADS ** -0.5 * IDX_HEAD_DIM ** -0.5)).T


def _proj_kw(h, w, g2, b2, tables, seq, tm=1024):
    n, d = h.shape
    per = seq // tm
    tspec = pl.BlockSpec((tm, LANES), lambda i: (i % per, 0))
    vspec = pl.BlockSpec((1, LANES), lambda i: (0, 0))
    ospec = pl.BlockSpec((tm, LANES), lambda i: (i, 0))
    return pl.pallas_call(
        _kw_kernel,
        grid=(n // tm,),
        in_specs=[
            pl.BlockSpec((tm, d), lambda i: (i, 0)),
            pl.BlockSpec((d, 2 * LANES), lambda i: (0, 0)),
            vspec, vspec, tspec, tspec, tspec,
        ],
        out_specs=[ospec, pl.BlockSpec((LANES, tm), lambda i: (0, i))],
        out_shape=[jax.ShapeDtypeStruct((n, LANES), BF16), jax.ShapeDtypeStruct((LANES, n), F32)],
        compiler_params=_params("arbitrary"),
        name="proj_index_kw",
    )(h, w, g2, b2, *tables)


def _mm_kernel(a_ref, w_ref, o_ref):
    o_ref[...] = _dot(a_ref[...], w_ref[...]).astype(o_ref.dtype)


def _matmul(a, w, out_dtype, tm, tn, name):
    n, k = a.shape
    cols = w.shape[1]
    return pl.pallas_call(
        _mm_kernel,
        grid=(n // tm, cols // tn),
        in_specs=[pl.BlockSpec((tm, k), lambda i, j: (i, 0)), pl.BlockSpec((k, tn), lambda i, j: (0, j))],
        out_specs=pl.BlockSpec((tm, tn), lambda i, j: (i, j)),
        out_shape=jax.ShapeDtypeStruct((n, cols), out_dtype),
        compiler_params=_params("arbitrary", "arbitrary"),
        name=name,
    )(a, w)


def _dsa_kernel(qi_ref, wi_ref, k2_ref, ql_ref, ckv_ref, ckvt_ref, ba_ref, wuvt_ref, o_ref,
                key_scr, hi_scr, lo_scr, qm_scr, acc_scr, m_scr, madd_scr, tie_scr,
                *, tq, tk, topk, n_delta):
    i = pl.program_id(1)
    q0 = i * tq
    nck = (q0 + tq + tk - 1) // tk
    nsub_k = tk // BAND_BLOCK
    qw = min(tq, DSA_STRIP)
    key_row = lax.broadcasted_iota(jnp.int32, (tk, tq), 0)
    q_pos = q0 + lax.broadcasted_iota(jnp.int32, (tk, tq), 1)
    lane = lax.broadcasted_iota(jnp.int32, (tq, LANES), 1)

    for h in range(IDX_HEADS):
        qs = qi_ref[:, (h // 2) * LANES:(h // 2 + 1) * LANES].astype(F32)
        keep = (lane >= IDX_HEAD_DIM) if h % 2 else (lane < IDX_HEAD_DIM)
        qm_scr[h] = jnp.where(keep, qs, 0.0).astype(BF16)

    def idx_body(c, carry):
        k2 = k2_ref[pl.ds(pl.multiple_of(c * tk, tk), tk), :]
        score = jnp.zeros((tk, tq), F32)
        for h in range(IDX_HEADS):
            logits = _dot_nt(k2, qm_scr[h])
            score = score + wi_ref[h:h + 1, :] * jnp.maximum(logits, 0.0)
        score = jnp.where(c * tk + key_row <= q_pos, score, -jnp.inf)
        bits = jnp.where(score == 0.0, 0, pltpu.bitcast(score, jnp.int32))
        key = bits ^ ((bits >> 31) & 0x7FFFFFFF)
        key_scr[c] = key
        hi_scr[c] = (key >> HALF_BITS).astype(jnp.int16)
        lo_scr[c] = ((key & HALF_MASK) - HALF_BIAS).astype(jnp.int16)
        return carry

    lax.fori_loop(0, nck, idx_body, 0)

    def count16(mask_fn):
        def cnt_body(c, part):
            hit = mask_fn(c).astype(jnp.int16)
            for g in range(tk // PACKED_ROWS):
                part = part + hit[g * PACKED_ROWS:(g + 1) * PACKED_ROWS]
            return part

        part = lax.fori_loop(0, nck, cnt_body, jnp.zeros((PACKED_ROWS, tq), jnp.int16))
        return jnp.sum(part.astype(F32), axis=0, keepdims=True)

    def select16(src_scr, need, n_all):
        def bit_body(b, carry):
            tx, n_ge = carry
            cand_x = tx | jnp.left_shift(jnp.int32(1), HALF_BITS - 1 - b)
            cand = (cand_x - HALF_BIAS).astype(jnp.int16)
            cnt = count16(lambda c: src_scr[c] >= cand)
            take = cnt >= need
            return jnp.where(take, cand_x, tx), jnp.where(take, cnt, n_ge)

        tx, n_ge = lax.fori_loop(0, HALF_BITS, bit_body, (jnp.zeros((1, tq), jnp.int32), n_all))
        return tx - HALF_BIAS, n_ge

    n_keys = jnp.full((1, tq), 1.0, F32) * (nck * tk).astype(F32)
    t_hi, n_hi_ge = select16(hi_scr, topk, n_keys)
    t_hi16 = t_hi.astype(jnp.int16)

    def tie_body(c, carry):
        lo_scr[c] = jnp.where(hi_scr[c] == t_hi16, lo_scr[c], jnp.int16(-HALF_BIAS))
        return carry

    lax.fori_loop(0, nck, tie_body, 0)
    above = count16(lambda c: hi_scr[c] > t_hi16)
    t_lo, n_lo_ge = select16(lo_scr, topk - above, n_hi_ge - above)
    thr = (t_hi << HALF_BITS) | (t_lo + HALF_BIAS)
    n_ge = above + n_lo_ge

    def count32(mask_fn):
        def cnt_body(c, part):
            hit = mask_fn(key_scr[c]).astype(jnp.int32)
            return part + jnp.sum(hit.reshape(tk // SUBLANES, SUBLANES, tq), axis=0)

        part = lax.fori_loop(0, nck, cnt_body, jnp.zeros((SUBLANES, tq), jnp.int32))
        return jnp.sum(part.astype(F32), axis=0, keepdims=True)

    has_tie = jnp.max(n_ge) > topk
    tie_scr[0:1, :] = jnp.zeros((1, tq), F32)

    @pl.when(has_tie)
    def _():
        tie_scr[1:2, :] = topk - count32(lambda key: key > thr)

    m_scr[...] = jnp.full(m_scr.shape, NEG_INF, F32)
    acc_scr[...] = jnp.zeros(acc_scr.shape, F32)

    def att_body(c, carry):
        kv = ckv_ref[pl.ds(pl.multiple_of(c * tk, tk), tk), :]
        kvt = ckvt_ref[c]
        causal = c * tk + key_row <= q_pos

        @pl.when(jnp.logical_not(has_tie))
        def _():
            madd_scr[...] = jnp.where((key_scr[c] >= thr) & causal, 0.0, NEG_INF)

        @pl.when(has_tie)
        def _():
            key = key_scr[c]
            tied = key == thr
            below = (lax.broadcasted_iota(jnp.int32, (tk, tk), 0)
                     >= lax.broadcasted_iota(jnp.int32, (tk, tk), 1))
            tied_f = tied.astype(F32)
            rank = _dot(below.astype(F32).astype(BF16), tied_f.astype(BF16)) + tie_scr[0:1, :]
            keep = (key > thr) | (tied & (rank <= tie_scr[1:2, :]))
            madd_scr[...] = jnp.where(keep & causal, 0.0, NEG_INF)
            tie_scr[0:1, :] = tie_scr[0:1, :] + jnp.sum(tied_f, axis=0, keepdims=True)

        madd = madd_scr[...]
        base_delta = (q0 - c * tk) // BAND_BLOCK
        items = [(h, w) for h in range(A_HEADS) for w in range(tq // qw)]

        def scores(item):
            h, w = item
            return _dot_nt(kv, ql_ref[h, w * qw:(w + 1) * qw, :])

        s_next = scores(items[0])
        for n_item, (h, w) in enumerate(items):
            qs = slice(w * qw, (w + 1) * qw)
            s_raw = s_next
            if n_item + 1 < len(items):
                s_next = scores(items[n_item + 1])
            rows = []
            for j in range(nsub_k):
                tiles = []
                for a in range(w * qw // BAND_BLOCK, (w + 1) * qw // BAND_BLOCK):
                    delta = jnp.clip(base_delta + (a - j), 0, n_delta - 1)
                    tiles.append(ba_ref[h, delta])
                rows.append(tiles[0] if len(tiles) == 1 else jnp.concatenate(tiles, axis=1))
            bias = rows[0] if nsub_k == 1 else jnp.concatenate(rows, axis=0)
            s = s_raw + bias + madd[:, qs]
            m_prev = m_scr[h:h + 1, qs]
            m_new = jnp.maximum(m_prev, jnp.max(s, axis=0, keepdims=True))
            alpha = jnp.exp2(m_prev - m_new)
            p = jnp.exp2(s - m_new)
            m_scr[h:h + 1, qs] = m_new
            if n_item + 1 < len(items):
                nxt = pltpu.bitcast(s_next[:SUBLANES], jnp.uint32)
                zero = lax.shift_right_logical(lax.shift_right_logical(nxt, jnp.uint32(16)), jnp.uint32(16))
                p = jnp.concatenate([p[:SUBLANES] + zero.astype(F32), p[SUBLANES:]], axis=0)
            acc_scr[h, :, qs] = acc_scr[h, :, qs] * alpha + _dot(kvt, p.astype(BF16))
        return carry

    lax.fori_loop(0, nck, att_body, 0)

    for h in range(A_HEADS):
        o_lat_t = (acc_scr[h, :A_KV_RANK, :] / acc_scr[h, A_KV_RANK:A_KV_RANK + 1, :]).astype(BF16)
        y_t = _dot(wuvt_ref[h], o_lat_t)
        o_ref[:, h * A_HEAD_DIM:(h + 1) * A_HEAD_DIM] = y_t.T.astype(o_ref.dtype)


def _dsa_attention(qi, wi_t, k2, ql, ckv, ckv_t, bias_a, wuv_t, batch, seq, tq):
    tk = tq
    n = batch * seq
    nq = seq // tq
    nck = seq // tk
    topk = min(DSA_TOPK, seq // 4)
    assert topk <= tk and seq % tq == 0 and tq % BAND_BLOCK == 0
    n_delta = bias_a.shape[1]
    kern = functools.partial(_dsa_kernel, tq=tq, tk=tk, topk=topk, n_delta=n_delta)
    once = pl.Buffered(1)
    return pl.pallas_call(
        kern,
        grid=(batch, nq),
        in_specs=[
            pl.BlockSpec((tq, IDX_Q_COLS), lambda b, i: (b * nq + i, 0)),
            pl.BlockSpec((LANES, tq), lambda b, i: (0, b * nq + i)),
            pl.BlockSpec((seq, LANES), lambda b, i: (b, 0), pipeline_mode=once),
            pl.BlockSpec((A_HEADS, tq, A_KV_RANK), lambda b, i: (0, b * nq + i, 0)),
            pl.BlockSpec((seq, A_KV_RANK), lambda b, i: (b, 0), pipeline_mode=once),
            pl.BlockSpec((nck, KV_ROWS, tk), lambda b, i: (b, 0, 0), pipeline_mode=once),
            pl.BlockSpec(bias_a.shape, lambda b, i: (0, 0, 0, 0), pipeline_mode=once),
            pl.BlockSpec(wuv_t.shape, lambda b, i: (0, 0, 0), pipeline_mode=once),
        ],
        out_specs=pl.BlockSpec((tq, A_OUT), lambda b, i: (b * nq + i, 0)),
        out_shape=jax.ShapeDtypeStruct((n, A_OUT), BF16),
        scratch_shapes=[
            pltpu.VMEM((nck, tk, tq), jnp.int32),
            pltpu.VMEM((nck, tk, tq), jnp.int16),
            pltpu.VMEM((nck, tk, tq), jnp.int16),
            pltpu.VMEM((IDX_HEADS, tq, LANES), BF16),
            pltpu.VMEM((A_HEADS, KV_ROWS, tq), F32),
            pltpu.VMEM((A_HEADS, tq), F32),
            pltpu.VMEM((tk, tq), F32),
            pltpu.VMEM((SUBLANES, tq), F32),
        ],
        compiler_params=_params("arbitrary", "arbitrary"),
        name="dsa_attention",
    )(qi, wi_t, k2, ql, ckv, ckv_t, bias_a, wuv_t)


def _deinterleave_matrix(tm, dil):
    per = tm // dil
    i = jnp.arange(tm)
    src = (i % per) * dil + i // per
    return (src[:, None] == i[None, :]).astype(BF16)


def _mm_stream_kernel(a_ref, w_ref, p_ref, o_ref):
    z = _dot(a_ref[...], w_ref[...]).astype(BF16)
    zs = _dot(p_ref[...], z).astype(BF16)
    o_ref[0] = zs.reshape(o_ref.shape[1:])


def _proj_dilated(h, w, batch, seq, dil, tm=256):
    n, d = h.shape
    c = w.shape[1]
    if dil == 1:
        return _matmul(h, w, BF16, 1024, c // 2, "proj_dilated_qkv_g0").reshape(batch, 1, seq, c)
    per_seq = seq // tm
    return pl.pallas_call(
        _mm_stream_kernel,
        grid=(n // tm,),
        in_specs=[
            pl.BlockSpec((tm, d), lambda i: (i, 0)),
            pl.BlockSpec((d, c), lambda i: (0, 0)),
            pl.BlockSpec((tm, tm), lambda i: (0, 0)),
        ],
        out_specs=pl.BlockSpec((1, dil, tm // dil, c), lambda i: (i // per_seq, 0, i % per_seq, 0)),
        out_shape=jax.ShapeDtypeStruct((batch, dil, seq // dil, c), BF16),
        compiler_params=_params("arbitrary"),
        name=f"proj_dilated_qkv_d{dil}",
    )(h, w, _deinterleave_matrix(tm, dil))


LSE_LANES = LANES // B_HEADS_PER_GROUP


def _dil_kernel(q_ref, kc_ref, kp_ref, vc_ref, vp_ref, bias_ref, o_ref, *, nsub, steps):
    first = pl.program_id(2) == 0
    P = BAND_BLOCK
    dh = B_HEAD_DIM
    row = lax.broadcasted_iota(jnp.int32, (P, 2 * P), 0)
    u = lax.broadcasted_iota(jnp.int32, (P, 2 * P), 1)
    back = row + P - u
    band = (back >= 0) & (back <= steps)
    band_first = band & ((u >= P) | jnp.logical_not(first))
    lane_head = lax.broadcasted_iota(jnp.int32, (P, LANES), 1) // LSE_LANES
    ones_cols = jnp.ones((2 * P, dh), BF16)
    for a in range(nsub):
        valid = band_first if a == 0 else band
        rs = slice(a * P, (a + 1) * P)
        lse_all = jnp.zeros((P, LANES), F32)
        for j in range(B_HEADS_PER_GROUP):
            cs = slice(j * dh, (j + 1) * dh)
            if a == 0:
                k_prev, v_prev = kp_ref[0, 0, :, cs], vp_ref[0, 0, :, cs]
            else:
                ps = slice((a - 1) * P, a * P)
                k_prev, v_prev = kc_ref[0, 0, ps, cs], vc_ref[0, 0, ps, cs]
            k_cat = jnp.concatenate([k_prev, kc_ref[0, 0, rs, cs]], axis=0)
            v_cat = jnp.concatenate([v_prev, vc_ref[0, 0, rs, cs]], axis=0)
            s = _dot_nt(q_ref[0, 0, rs, cs], k_cat) * (dh ** -0.5) + bias_ref[0, j]
            s = jnp.where(valid, s, NEG_INF)
            mx = jnp.max(s, axis=-1, keepdims=True)
            p = jnp.exp(s - mx).astype(BF16)
            pv = _dot(p, jnp.concatenate([v_cat, ones_cols], axis=1))
            den = pv[:, dh:]
            o_ref[0, 0, rs, cs] = pv[:, :dh] / den
            lse_all = jnp.where(lane_head == j, mx + jnp.log(den), lse_all)
        o_ref[0, 0, rs, B_OUT:] = lse_all


def _dilated_group(qkv, bias_b, g, batch, seq):
    window, dil = B_GROUPS[g]
    steps = window // dil
    assert steps <= BAND_BLOCK
    m = seq // dil
    rows = min(4 * BAND_BLOCK, m)
    assert m % rows == 0 and rows % BAND_BLOCK == 0
    nsub = rows // BAND_BLOCK
    width = B_OUT

    def cur(which):
        return pl.BlockSpec((1, 1, rows, width), lambda b, r, i: (b, r, i, which))

    def prev(which):
        return pl.BlockSpec((1, 1, BAND_BLOCK, width),
                            lambda b, r, i: (b, r, jnp.maximum(i * nsub - 1, 0), which))

    return pl.pallas_call(
        functools.partial(_dil_kernel, nsub=nsub, steps=steps),
        grid=(batch, dil, m // rows),
        in_specs=[cur(0), cur(1), prev(1), cur(2), prev(2),
                  pl.BlockSpec((1, B_HEADS_PER_GROUP, BAND_BLOCK, 2 * BAND_BLOCK),
                               lambda b, r, i: (g, 0, 0, 0))],
        out_specs=pl.BlockSpec((1, 1, rows, width + LANES), lambda b, r, i: (b, r, i, 0)),
        out_shape=jax.ShapeDtypeStruct((batch, dil, m, width + LANES), F32),
        compiler_params=_params("arbitrary", "arbitrary", "arbitrary"),
        name=f"dilated_attention_g{g}",
    )(qkv, qkv, qkv, qkv, qkv, bias_b)


def _to_token_order(blk, pinv):
    x = blk.reshape(pinv.shape[0], blk.shape[-1])
    hi = x.astype(BF16)
    r1 = x - hi.astype(F32)
    mid = r1.astype(BF16)
    lo = (r1 - mid.astype(F32)).astype(BF16)
    return (_dot(pinv, hi) + _dot(pinv, mid)) + _dot(pinv, lo)


def _dil_merge_kernel(t0_ref, t1_ref, t2_ref, p1_ref, p2_ref, y_ref):
    groups = [t0_ref[...], _to_token_order(t1_ref[0], p1_ref[...]), _to_token_order(t2_ref[0], p2_ref[...])]
    tm = y_ref.shape[0]
    dh = B_HEAD_DIM
    for j in range(B_HEADS_PER_GROUP):
        lse = [jnp.broadcast_to(t[:, B_OUT + j * LSE_LANES:B_OUT + j * LSE_LANES + 1], (tm, dh))
               for t in groups]
        mx = jnp.maximum(jnp.maximum(lse[0], lse[1]), lse[2])
        e = [jnp.exp(a - mx) for a in lse]
        den = e[0] + e[1] + e[2]
        y = sum((e[g] / den) * groups[g][:, j * dh:(j + 1) * dh] for g in range(3))
        y_ref[:, j * dh:(j + 1) * dh] = y.astype(y_ref.dtype)


def _dilated_merge(packed, batch, seq, tm=256):
    n = batch * seq
    w = packed[0].shape[-1]
    per_seq = seq // tm
    dils = [d for _, d in B_GROUPS]
    assert dils[0] == 1 and len(dils) == 3

    def stream(dil):
        return pl.BlockSpec((1, dil, tm // dil, w), lambda i: (i // per_seq, 0, i % per_seq, 0))

    pspec = pl.BlockSpec((tm, tm), lambda i: (0, 0))
    return pl.pallas_call(
        _dil_merge_kernel,
        grid=(n // tm,),
        in_specs=[pl.BlockSpec((tm, w), lambda i: (i, 0)), stream(dils[1]), stream(dils[2]), pspec, pspec],
        out_specs=pl.BlockSpec((tm, B_OUT), lambda i: (i, 0)),
        out_shape=jax.ShapeDtypeStruct((n, B_OUT), BF16),
        compiler_params=_params("arbitrary"),
        name="dilated_merge",
    )(packed[0].reshape(n, w), packed[1], packed[2],
      _deinterleave_matrix(tm, dils[1]).T, _deinterleave_matrix(tm, dils[2]).T)


def _merge_kernel(h_ref, ya_ref, yb_ref, wga_ref, wgb_ref, wa_ref, wb_ref, o_ref):
    h = h_ref[...]
    ga = jax.nn.sigmoid(_dot(h, wga_ref[...]))
    gb = jax.nn.sigmoid(_dot(h, wgb_ref[...]))
    merged = ga * _dot(ya_ref[...], wa_ref[...]) + gb * _dot(yb_ref[...], wb_ref[...])
    o_ref[...] = merged.astype(o_ref.dtype)


def _gated_merge(h, ya, yb, w_tail, gate_col, wa, wb, tm=1024, tn=512):
    n, d = h.shape
    assert gate_col % tn == 0 and d % tn == 0
    ga0, gb0 = gate_col // tn, (gate_col + d) // tn

    def rows(k):
        return pl.BlockSpec((tm, k), lambda i, j: (i, 0))

    def cols(k):
        return pl.BlockSpec((k, tn), lambda i, j: (0, j))

    return pl.pallas_call(
        _merge_kernel,
        grid=(n // tm, d // tn),
        in_specs=[rows(d), rows(A_OUT), rows(B_OUT),
                  pl.BlockSpec((d, tn), lambda i, j: (0, ga0 + j)),
                  pl.BlockSpec((d, tn), lambda i, j: (0, gb0 + j)),
                  cols(A_OUT), cols(B_OUT)],
        out_specs=pl.BlockSpec((tm, tn), lambda i, j: (i, j)),
        out_shape=jax.ShapeDtypeStruct((n, d), BF16),
        compiler_params=_params("arbitrary", "arbitrary"),
        name="gated_merge",
    )(h, ya, yb, w_tail, w_tail, wa, wb)


def _res_norm_kernel(a_ref, w_ref, x_ref, gate_ref, g_ref, sc_ref, sh_ref, xo_ref, h_ref):
    xn = x_ref[...] + gate_ref[0] * _dot(a_ref[...], w_ref[...])
    xo_ref[...] = xn
    h_ref[...] = (_rms(xn) * g_ref[...] * (1.0 + sc_ref[0]) + sh_ref[0]).astype(h_ref.dtype)


def _res_final_kernel(a_ref, w_ref, x_ref, gate_ref, g_ref, o_ref):
    xn = x_ref[...] + gate_ref[0] * _dot(a_ref[...], w_ref[...])
    o_ref[...] = _rms(xn) * g_ref[...]


def _matmul_residual_norm(a, w, x, gate, g, scale, shift, seq, tm, name):
    n, k = a.shape
    d = w.shape[1]
    per = seq // tm
    row = pl.BlockSpec((tm, d), lambda i: (i, 0))
    vec = pl.BlockSpec((1, 1, d), lambda i: (i // per, 0, 0))
    in_specs = [
        pl.BlockSpec((tm, k), lambda i: (i, 0)),
        pl.BlockSpec((k, d), lambda i: (0, 0), pipeline_mode=pl.Buffered(1)),
        row, vec,
        pl.BlockSpec((1, d), lambda i: (0, 0)),
    ]
    args = [a, w, x, gate, g.reshape(1, d)]
    if scale is None:
        return pl.pallas_call(
            _res_final_kernel, grid=(n // tm,), in_specs=in_specs, out_specs=row,
            out_shape=jax.ShapeDtypeStruct((n, d), F32),
            compiler_params=_params("arbitrary"), name=name,
        )(*args)
    return pl.pallas_call(
        _res_norm_kernel, grid=(n // tm,), in_specs=in_specs + [vec, vec], out_specs=[row, row],
        out_shape=[jax.ShapeDtypeStruct((n, d), F32), jax.ShapeDtypeStruct((n, d), BF16)],
        compiler_params=_params("arbitrary"), name=name,
    )(*args, scale, shift)


def _ffn_up_kernel(h_ref, wg_ref, wu_ref, o_ref, wg_scr, wu_scr):
    @pl.when(pl.program_id(1) == 0)
    def _():
        wg_scr[...] = wg_ref[...].astype(BF16)
        wu_scr[...] = wu_ref[...].astype(BF16)

    h = h_ref[...]
    a = _dot(h, wg_scr[...])
    o_ref[...] = (a * jax.nn.sigmoid(a) * _dot(h, wu_scr[...])).astype(o_ref.dtype)


def _ffn_up(h, wg, wu, layer, tm=1024, tn=512):
    n, d = h.shape
    f = wg.shape[2]
    wspec = pl.BlockSpec((None, d, tn), lambda j, i: (layer, 0, j))
    return pl.pallas_call(
        _ffn_up_kernel,
        grid=(f // tn, n // tm),
        in_specs=[pl.BlockSpec((tm, d), lambda j, i: (i, 0)), wspec, wspec],
        out_specs=pl.BlockSpec((tm, tn), lambda j, i: (i, j)),
        out_shape=jax.ShapeDtypeStruct((n, f), BF16),
        scratch_shapes=[pltpu.VMEM((d, tn), BF16), pltpu.VMEM((d, tn), BF16)],
        compiler_params=_params("arbitrary", "arbitrary"),
        name="ffn_up",
    )(h, wg, wu)


def _rope_tables(seq):
    half = IDX_ROPE_DIM // 2
    freqs = ROPE_THETA ** (-jnp.arange(half, dtype=F32) / half)
    ang = jnp.arange(seq).astype(F32)[:, None] * freqs[None, :]
    cos, sin = jnp.cos(ang), jnp.sin(ang)
    rest = IDX_HEAD_DIM - IDX_ROPE_DIM
    one = jnp.ones((seq, rest), F32)
    zr = jnp.zeros((seq, rest), F32)
    zh = jnp.zeros((seq, half), F32)
    cos_t = jnp.concatenate([cos, cos, one], axis=1)
    sin_lo = jnp.concatenate([-sin, zh, zr], axis=1)
    sin_hi = jnp.concatenate([zh, sin, zr], axis=1)
    rep = LANES // IDX_HEAD_DIM
    return tuple(jnp.tile(t, (1, rep)) for t in (cos_t, sin_lo, sin_hi))


def _layer(layer, x, h, h_mod, next_norm, tables, bias_a, bias_b, batch, seq, w_in, kv_norm_g, idx_ln_g,
           idx_ln_b, w_uk, w_uv, w_a_up, w_b_up, w_out, norm2_g, w_ff_gate, w_ff_up, w_ff_down):
    _, _, gate1, shift2, scale2, gate2 = h_mod
    offs = [0]
    for s in IN_SIZES:
        offs.append(offs[-1] + s)
    w_qa, w_kv, w_qi, w_ki, w_wi = [w_in[:, offs[k]:offs[k + 1]] for k in range(5)]
    w_tail = w_in[:, offs[5]:].astype(BF16)
    w_qkvb = w_tail[:, :B_QKV_COLS]
    d = w_in.shape[0]
    rep = LANES // IDX_HEAD_DIM
    w_kw = jnp.concatenate(
        [w_ki] * rep + [w_wi, jnp.zeros((d, LANES - IDX_HEADS), w_in.dtype)], axis=1).astype(BF16)
    ln_g2 = jnp.tile(idx_ln_g, rep).reshape(1, LANES)
    ln_b2 = jnp.tile(idx_ln_b, rep).reshape(1, LANES)

    ql = _proj_qlat(h, w_qa.astype(BF16), w_uk.astype(BF16))
    ckv, ckv_t = _proj_ckv(h, w_kv.astype(BF16), kv_norm_g, DSA_TILE)
    qi = _proj_qi(h, w_qi.astype(BF16), tables, seq)
    k2, wi_t = _proj_kw(h, w_kw, ln_g2, ln_b2, tables, seq)
    wuv_t = jnp.swapaxes(w_uv, 1, 2).astype(BF16)
    ya = _dsa_attention(qi, wi_t, k2, ql, ckv, ckv_t, bias_a, wuv_t, batch, seq, DSA_TILE)

    ng = len(B_GROUPS)
    packed = []
    for g, (_, dil) in enumerate(B_GROUPS):
        w_g = jnp.concatenate([w_qkvb[:, (s * ng + g) * B_OUT:(s * ng + g + 1) * B_OUT] for s in range(3)],
                              axis=1)
        packed.append(_dilated_group(_proj_dilated(h, w_g, batch, seq, dil), bias_b, g, batch, seq))
    yb = _dilated_merge(packed, batch, seq)

    merged = _gated_merge(h, ya, yb, w_tail, B_QKV_COLS, w_a_up.astype(BF16), w_b_up.astype(BF16))
    x, h2 = _matmul_residual_norm(merged, w_out.astype(BF16), x, gate1, norm2_g, scale2, shift2, seq,
                                  512, "out_proj_residual")
    act = _ffn_up(h2, w_ff_gate, w_ff_up, layer)
    return _matmul_residual_norm(act, w_ff_down.astype(BF16), x, gate2, *next_norm, seq, 256,
                                 "ffn_down_residual")


def kernel(x, c, rel_bias, w_ada, b_ada, norm1_g, w_in, kv_norm_g, idx_ln_g, idx_ln_b, w_uk, w_uv,
           w_a_up, w_b_up, w_out, norm2_g, w_ff_gate, w_ff_up, w_ff_down, final_g):
    batch, seq, d = x.shape
    depth = w_ada.shape[0]
    assert d == D_MODEL and seq % (B_GROUPS[-1][1] * BAND_BLOCK) == 0
    n = batch * seq
    rows = -(-batch // SUBLANES) * SUBLANES
    mod = _modulation(jnp.pad(c, ((0, rows - batch), (0, 0))), w_ada, b_ada)
    mod = mod[:, :batch].reshape(depth, batch, 6, 1, d)
    bias_a = _bias_a_table(rel_bias, min(seq // BAND_BLOCK, FAR_DELTA + 1))
    bias_b = _bias_b_table(rel_bias)
    tables = _rope_tables(seq)
    xf = x.reshape(n, d)
    h_mods = [[mod[l, :, k] for k in range(6)] for l in range(depth)]
    h = _norm_mod(xf, norm1_g[0], h_mods[0][1], h_mods[0][0], seq)
    for l in range(depth):
        last = l + 1 == depth
        next_norm = (final_g, None, None) if last else (norm1_g[l + 1], h_mods[l + 1][1], h_mods[l + 1][0])
        out = _layer(l, xf, h, h_mods[l], next_norm, tables, bias_a, bias_b, batch, seq, w_in[l],
                     kv_norm_g[l], idx_ln_g[l], idx_ln_b[l], w_uk[l], w_uv[l], w_a_up[l], w_b_up[l], w_out[l],
                     norm2_g[l], w_ff_gate, w_ff_up, w_ff_down[l])
        if last:
            return out.reshape(batch, seq, d)
        xf, h = out
```

```python
import functools
import math

import jax
import jax.numpy as jnp
from jax import lax
from jax.experimental import pallas as pl
from jax.experimental.pallas import tpu as pltpu

D_MODEL = 2048
A_HEADS = 8
A_HEAD_DIM = 128
A_KV_RANK = 256
IDX_HEADS = 16
IDX_HEAD_DIM = 64
IDX_ROPE_DIM = 32
ROPE_THETA = 10000.0
DSA_TOPK = 256
B_GROUPS = ((128, 1), (512, 4), (2048, 16))
B_HEADS_PER_GROUP = 4
B_HEAD_DIM = 128
B_HEADS = B_HEADS_PER_GROUP * len(B_GROUPS)
BAND_BLOCK = 128
REL_BUCKETS = 32
REL_MAX_DISTANCE = 2048
N_BIAS_HEADS = A_HEADS + B_HEADS
D_FF = -(-8 * D_MODEL // (3 * 256)) * 256
NORM_EPS = 1e-6
NEG_INF = -1e30

A_Q_COLS = A_HEADS * A_HEAD_DIM
IDX_Q_COLS = IDX_HEADS * IDX_HEAD_DIM
B_QKV_COLS = 3 * B_HEADS * B_HEAD_DIM
GATE_COLS = 2 * D_MODEL
IN_SIZES = (A_Q_COLS, A_KV_RANK, IDX_Q_COLS, IDX_HEAD_DIM, IDX_HEADS, B_QKV_COLS, GATE_COLS)
A_OUT = A_HEADS * A_HEAD_DIM
B_OUT = B_HEADS_PER_GROUP * B_HEAD_DIM

LANES = 128
SUBLANES = 8
VMEM_LIMIT_BYTES = 56 * 1024 * 1024

BF16 = jnp.bfloat16
F32 = jnp.float32
LOG2E = math.log2(math.e)
HALF_BITS = 16
HALF_MASK = 2 ** HALF_BITS - 1
HALF_BIAS = 2 ** (HALF_BITS - 1)
PACKED_ROWS = 2 * SUBLANES
KV_ROWS = A_KV_RANK + PACKED_ROWS

FAR_DELTA = -(-(REL_MAX_DISTANCE + BAND_BLOCK - 1) // BAND_BLOCK)
DSA_TILE = 512
DSA_STRIP = 256


def _dot(a, b):
    return jnp.dot(a, b, preferred_element_type=F32)


def _dot_nt(a, b):
    return lax.dot_general(a, b, (((1,), (1,)), ((), ())), preferred_element_type=F32)


def _params(*sem):
    return pltpu.CompilerParams(dimension_semantics=sem, vmem_limit_bytes=VMEM_LIMIT_BYTES)


def _mod_kernel(c_ref, w_ref, b_ref, o_ref):
    c = c_ref[...]
    a = (c * jax.nn.sigmoid(c)).astype(BF16)
    o_ref[0] = _dot(a, w_ref[0].astype(BF16)) + b_ref[0]


def _modulation(c_pad, w_ada, b_ada, tn=2048):
    depth, d, n6 = w_ada.shape
    rows = c_pad.shape[0]
    return pl.pallas_call(
        _mod_kernel,
        grid=(depth, n6 // tn),
        in_specs=[
            pl.BlockSpec((rows, d), lambda l, j: (0, 0)),
            pl.BlockSpec((1, d, tn), lambda l, j: (l, 0, j)),
            pl.BlockSpec((1, 1, tn), lambda l, j: (l, 0, j)),
        ],
        out_specs=pl.BlockSpec((1, rows, tn), lambda l, j: (l, 0, j)),
        out_shape=jax.ShapeDtypeStruct((depth, rows, n6), F32),
        compiler_params=_params("arbitrary", "arbitrary"),
        name="adaln_modulation",
    )(c_pad, w_ada, b_ada.reshape(depth, 1, n6))


def _bucket(dist):
    n = jnp.maximum(dist, 0)
    exact = REL_BUCKETS // 2
    nf = jnp.maximum(n, 1).astype(F32)
    large = exact + (jnp.log(nf / exact) / math.log(REL_MAX_DISTANCE / exact)
                     * (REL_BUCKETS - exact)).astype(jnp.int32)
    return jnp.where(n < exact, n, jnp.minimum(large, REL_BUCKETS - 1))


def _lookup(rb_ref, bucket, head):
    t = jnp.zeros(bucket.shape, F32)
    for k in range(REL_BUCKETS):
        t = jnp.where(bucket == k, rb_ref[k, head], t)
    return t


def _bias_a_kernel(rb_ref, o_ref):
    delta = pl.program_id(0)
    row = lax.broadcasted_iota(jnp.int32, (BAND_BLOCK, BAND_BLOCK), 0)
    col = lax.broadcasted_iota(jnp.int32, (BAND_BLOCK, BAND_BLOCK), 1)
    bucket = _bucket(delta * BAND_BLOCK + col - row)
    for h in range(A_HEADS):
        o_ref[h, 0] = _lookup(rb_ref, bucket, h) * LOG2E


def _bias_a_table(rel_bias, n_delta):
    return pl.pallas_call(
        _bias_a_kernel,
        grid=(n_delta,),
        in_specs=[pl.BlockSpec(memory_space=pltpu.SMEM)],
        out_specs=pl.BlockSpec((A_HEADS, 1, BAND_BLOCK, BAND_BLOCK), lambda d: (0, d, 0, 0)),
        out_shape=jax.ShapeDtypeStruct((A_HEADS, n_delta, BAND_BLOCK, BAND_BLOCK), F32),
        compiler_params=_params("arbitrary"),
        name="rel_bias_table_a",
    )(rel_bias)


def _bias_b_kernel(rb_ref, o_ref):
    g = pl.program_id(0)
    dil = jnp.where(g == 0, B_GROUPS[0][1], jnp.where(g == 1, B_GROUPS[1][1], B_GROUPS[2][1]))
    row = lax.broadcasted_iota(jnp.int32, (BAND_BLOCK, 2 * BAND_BLOCK), 0)
    u = lax.broadcasted_iota(jnp.int32, (BAND_BLOCK, 2 * BAND_BLOCK), 1)
    bucket = _bucket((row + BAND_BLOCK - u) * dil)
    for j in range(B_HEADS_PER_GROUP):
        o_ref[0, j] = _lookup(rb_ref, bucket, A_HEADS + g * B_HEADS_PER_GROUP + j)


def _bias_b_table(rel_bias):
    ng = len(B_GROUPS)
    return pl.pallas_call(
        _bias_b_kernel,
        grid=(ng,),
        in_specs=[pl.BlockSpec(memory_space=pltpu.SMEM)],
        out_specs=pl.BlockSpec((1, B_HEADS_PER_GROUP, BAND_BLOCK, 2 * BAND_BLOCK),
                               lambda g: (g, 0, 0, 0)),
        out_shape=jax.ShapeDtypeStruct((ng, B_HEADS_PER_GROUP, BAND_BLOCK, 2 * BAND_BLOCK), F32),
        compiler_params=_params("arbitrary"),
        name="rel_bias_table_b",
    )(rel_bias)


def _rms(x):
    return x * lax.rsqrt(jnp.mean(x * x, axis=-1, keepdims=True) + NORM_EPS)


def _norm_mod_kernel(x_ref, g_ref, sc_ref, sh_ref, o_ref):
    y = _rms(x_ref[...]) * g_ref[...]
    o_ref[...] = (y * (1.0 + sc_ref[0]) + sh_ref[0]).astype(o_ref.dtype)


def _norm_mod(x, g, scale, shift, seq, tm=512):
    n, d = x.shape
    per = seq // tm
    return pl.pallas_call(
        _norm_mod_kernel,
        grid=(n // tm,),
        in_specs=[
            pl.BlockSpec((tm, d), lambda i: (i, 0)),
            pl.BlockSpec((1, d), lambda i: (0, 0)),
            pl.BlockSpec((1, 1, d), lambda i: (i // per, 0, 0)),
            pl.BlockSpec((1, 1, d), lambda i: (i // per, 0, 0)),
        ],
        out_specs=pl.BlockSpec((tm, d), lambda i: (i, 0)),
        out_shape=jax.ShapeDtypeStruct((n, d), BF16),
        compiler_params=_params("arbitrary"),
        name="norm_modulate",
    )(x, g.reshape(1, d), scale, shift)


def _qlat_kernel(h_ref, w_ref, wuk_ref, o_ref):
    z = _dot(h_ref[...], w_ref[...])
    for hd in range(A_HEADS):
        zh = z[:, hd * A_HEAD_DIM:(hd + 1) * A_HEAD_DIM].astype(BF16)
        o_ref[hd] = (_dot(zh, wuk_ref[hd]) * (A_HEAD_DIM ** -0.5 * LOG2E)).astype(o_ref.dtype)


def _proj_qlat(h, w, wuk, tm=512):
    n, d = h.shape
    return pl.pallas_call(
        _qlat_kernel,
        grid=(n // tm,),
        in_specs=[
            pl.BlockSpec((tm, d), lambda i: (i, 0)),
            pl.BlockSpec((d, A_Q_COLS), lambda i: (0, 0)),
            pl.BlockSpec((A_HEADS, A_HEAD_DIM, A_KV_RANK), lambda i: (0, 0, 0)),
        ],
        out_specs=pl.BlockSpec((A_HEADS, tm, A_KV_RANK), lambda i: (0, i, 0)),
        out_shape=jax.ShapeDtypeStruct((A_HEADS, n, A_KV_RANK), BF16),
        compiler_params=_params("arbitrary"),
        name="proj_q_latent",
    )(h, w, wuk)


def _ckv_kernel(h_ref, w_ref, g_ref, o_ref, ot_ref, *, tk):
    z = _dot(h_ref[...], w_ref[...])
    ckv = _rms(z) * g_ref[...]
    o_ref[...] = ckv.astype(o_ref.dtype)
    ones = jnp.ones((KV_ROWS - A_KV_RANK, tk), F32)
    for s in range(ot_ref.shape[0]):
        ot_ref[s] = jnp.concatenate([ckv[s * tk:(s + 1) * tk, :].T, ones], axis=0).astype(ot_ref.dtype)


def _proj_ckv(h, w, g, tk, tm=1024):
    n, d = h.shape
    return pl.pallas_call(
        functools.partial(_ckv_kernel, tk=tk),
        grid=(n // tm,),
        in_specs=[
            pl.BlockSpec((tm, d), lambda i: (i, 0)),
            pl.BlockSpec((d, A_KV_RANK), lambda i: (0, 0)),
            pl.BlockSpec((1, A_KV_RANK), lambda i: (0, 0)),
        ],
        out_specs=[pl.BlockSpec((tm, A_KV_RANK), lambda i: (i, 0)),
                   pl.BlockSpec((tm // tk, KV_ROWS, tk), lambda i: (i, 0, 0))],
        out_shape=[jax.ShapeDtypeStruct((n, A_KV_RANK), BF16),
                   jax.ShapeDtypeStruct((n // tk, KV_ROWS, tk), BF16)],
        compiler_params=_params("arbitrary"),
        name="proj_latent_kv",
    )(h, w, g.reshape(1, A_KV_RANK))


def _rope(z, cos_t, sin_lo, sin_hi):
    half = IDX_ROPE_DIM // 2
    return (z * cos_t + pltpu.roll(z, half, 1) * sin_hi
            + pltpu.roll(z, LANES - half, 1) * sin_lo)


def _qi_kernel(h_ref, w_ref, cos_ref, slo_ref, shi_ref, o_ref):
    z = _dot(h_ref[...], w_ref[...])
    cos_t, slo, shi = cos_ref[...], slo_ref[...], shi_ref[...]
    for s in range(IDX_Q_COLS // LANES):
        zs = z[:, s * LANES:(s + 1) * LANES]
        o_ref[:, s * LANES:(s + 1) * LANES] = _rope(zs, cos_t, slo, shi).astype(o_ref.dtype)


def _proj_qi(h, w, tables, seq, tm=512):
    n, d = h.shape
    per = seq // tm
    tspec = pl.BlockSpec((tm, LANES), lambda i: (i % per, 0))
    return pl.pallas_call(
        _qi_kernel,
        grid=(n // tm,),
        in_specs=[
            pl.BlockSpec((tm, d), lambda i: (i, 0)),
            pl.BlockSpec((d, IDX_Q_COLS), lambda i: (0, 0)),
            tspec, tspec, tspec,
        ],
        out_specs=pl.BlockSpec((tm, IDX_Q_COLS), lambda i: (i, 0)),
        out_shape=jax.ShapeDtypeStruct((n, IDX_Q_COLS), BF16),
        compiler_params=_params("arbitrary"),
        name="proj_index_q",
    )(h, w, *tables)


def _kw_kernel(h_ref, w_ref, g_ref, b_ref, cos_ref, slo_ref, shi_ref, k_ref, wi_ref):
    z = _dot(h_ref[...], w_ref[...])
    zk = z[:, :LANES]
    mu = jnp.mean(zk, axis=-1, keepdims=True)
    var = jnp.mean(jnp.square(zk - mu), axis=-1, keepdims=True)
    kn = (zk - mu) * lax.rsqrt(var + NORM_EPS) * g_ref[...] + b_ref[...]
    k_ref[...] = _rope(kn, cos_ref[...], slo_ref[...], shi_ref[...]).astype(k_ref.dtype)
    wi_ref[...] = (z[:, LANES:] * (IDX_HEADS ** -0.5 * IDX_HEAD_DIM ** -0.5)).T


def _proj_kw(h, w, g2, b2, tables, seq, tm=1024):
    n, d = h.shape
    per = seq // tm
    tspec = pl.BlockSpec((tm, LANES), lambda i: (i % per, 0))
    vspec = pl.BlockSpec((1, LANES), lambda i: (0, 0))
    ospec = pl.BlockSpec((tm, LANES), lambda i: (i, 0))
    return pl.pallas_call(
        _kw_kernel,
        grid=(n // tm,),
        in_specs=[
            pl.BlockSpec((tm, d), lambda i: (i, 0)),
            pl.BlockSpec((d, 2 * LANES), lambda i: (0, 0)),
            vspec, vspec, tspec, tspec, tspec,
        ],
        out_specs=[ospec, pl.BlockSpec((LANES, tm), lambda i: (0, i))],
        out_shape=[jax.ShapeDtypeStruct((n, LANES), BF16), jax.ShapeDtypeStruct((LANES, n), F32)],
        compiler_params=_params("arbitrary"),
        name="proj_index_kw",
    )(h, w, g2, b2, *tables)


def _mm_kernel(a_ref, w_ref, o_ref):
    o_ref[...] = _dot(a_ref[...], w_ref[...]).astype(o_ref.dtype)


def _matmul(a, w, out_dtype, tm, tn, name):
    n, k = a.shape
    cols = w.shape[1]
    return pl.pallas_call(
        _mm_kernel,
        grid=(n // tm, cols // tn),
        in_specs=[pl.BlockSpec((tm, k), lambda i, j: (i, 0)), pl.BlockSpec((k, tn), lambda i, j: (0, j))],
        out_specs=pl.BlockSpec((tm, tn), lambda i, j: (i, j)),
        out_shape=jax.ShapeDtypeStruct((n, cols), out_dtype),
        compiler_params=_params("arbitrary", "arbitrary"),
        name=name,
    )(a, w)


def _dsa_kernel(qi_ref, wi_ref, k2_ref, ql_ref, ckv_ref, ckvt_ref, ba_ref, wuvt_ref, o_ref,
                key_scr, hi_scr, lo_scr, qm_scr, acc_scr, m_scr, madd_scr, tie_scr,
                *, tq, tk, topk, n_delta):
    i = pl.program_id(1)
    q0 = i * tq
    nck = (q0 + tq + tk - 1) // tk
    nsub_k = tk // BAND_BLOCK
    qw = min(tq, DSA_STRIP)
    key_row = lax.broadcasted_iota(jnp.int32, (tk, tq), 0)
    q_pos = q0 + lax.broadcasted_iota(jnp.int32, (tk, tq), 1)
    lane = lax.broadcasted_iota(jnp.int32, (tq, LANES), 1)

    for h in range(IDX_HEADS):
        qs = qi_ref[:, (h // 2) * LANES:(h // 2 + 1) * LANES].astype(F32)
        keep = (lane >= IDX_HEAD_DIM) if h % 2 else (lane < IDX_HEAD_DIM)
        qm_scr[h] = jnp.where(keep, qs, 0.0).astype(BF16)

    def idx_body(c, carry):
        k2 = k2_ref[pl.ds(pl.multiple_of(c * tk, tk), tk), :]
        score = jnp.zeros((tk, tq), F32)
        for h in range(IDX_HEADS):
            logits = _dot_nt(k2, qm_scr[h])
            score = score + wi_ref[h:h + 1, :] * jnp.maximum(logits, 0.0)
        score = jnp.where(c * tk + key_row <= q_pos, score, -jnp.inf)
        bits = jnp.where(score == 0.0, 0, pltpu.bitcast(score, jnp.int32))
        key = bits ^ ((bits >> 31) & 0x7FFFFFFF)
        key_scr[c] = key
        hi_scr[c] = (key >> HALF_BITS).astype(jnp.int16)
        lo_scr[c] = ((key & HALF_MASK) - HALF_BIAS).astype(jnp.int16)
        return carry

    lax.fori_loop(0, nck, idx_body, 0)

    def count16(mask_fn):
        def cnt_body(c, part):
            hit = mask_fn(c).astype(jnp.int16)
            for g in range(tk // PACKED_ROWS):
                part = part + hit[g * PACKED_ROWS:(g + 1) * PACKED_ROWS]
            return part

        part = lax.fori_loop(0, nck, cnt_body, jnp.zeros((PACKED_ROWS, tq), jnp.int16))
        return jnp.sum(part.astype(F32), axis=0, keepdims=True)

    def select16(src_scr, need, n_all):
        def bit_body(b, carry):
            tx, n_ge = carry
            cand_x = tx | jnp.left_shift(jnp.int32(1), HALF_BITS - 1 - b)
            cand = (cand_x - HALF_BIAS).astype(jnp.int16)
            cnt = count16(lambda c: src_scr[c] >= cand)
            take = cnt >= need
            return jnp.where(take, cand_x, tx), jnp.where(take, cnt, n_ge)

        tx, n_ge = lax.fori_loop(0, HALF_BITS, bit_body, (jnp.zeros((1, tq), jnp.int32), n_all))
        return tx - HALF_BIAS, n_ge

    n_keys = jnp.full((1, tq), 1.0, F32) * (nck * tk).astype(F32)
    t_hi, n_hi_ge = select16(hi_scr, topk, n_keys)
    t_hi16 = t_hi.astype(jnp.int16)

    def tie_body(c, carry):
        lo_scr[c] = jnp.where(hi_scr[c] == t_hi16, lo_scr[c], jnp.int16(-HALF_BIAS))
        return carry

    lax.fori_loop(0, nck, tie_body, 0)
    above = count16(lambda c: hi_scr[c] > t_hi16)
    t_lo, n_lo_ge = select16(lo_scr, topk - above, n_hi_ge - above)
    thr = (t_hi << HALF_BITS) | (t_lo + HALF_BIAS)
    n_ge = above + n_lo_ge

    def count32(mask_fn):
        def cnt_body(c, part):
            hit = mask_fn(key_scr[c]).astype(jnp.int32)
            return part + jnp.sum(hit.reshape(tk // SUBLANES, SUBLANES, tq), axis=0)

        part = lax.fori_loop(0, nck, cnt_body, jnp.zeros((SUBLANES, tq), jnp.int32))
        return jnp.sum(part.astype(F32), axis=0, keepdims=True)

    has_tie = jnp.max(n_ge) > topk
    tie_scr[0:1, :] = jnp.zeros((1, tq), F32)

    @pl.when(has_tie)
    def _():
        tie_scr[1:2, :] = topk - count32(lambda key: key > thr)

    m_scr[...] = jnp.full(m_scr.shape, NEG_INF, F32)
    acc_scr[...] = jnp.zeros(acc_scr.shape, F32)

    def att_body(c, carry):
        kv = ckv_ref[pl.ds(pl.multiple_of(c * tk, tk), tk), :]
        kvt = ckvt_ref[c]
        causal = c * tk + key_row <= q_pos

        @pl.when(jnp.logical_not(has_tie))
        def _():
            madd_scr[...] = jnp.where((key_scr[c] >= thr) & causal, 0.0, NEG_INF)

        @pl.when(has_tie)
        def _():
            key = key_scr[c]
            tied = key == thr
            below = (lax.broadcasted_iota(jnp.int32, (tk, tk), 0)
                     >= lax.broadcasted_iota(jnp.int32, (tk, tk), 1))
            tied_f = tied.astype(F32)
            rank = _dot(below.astype(F32).astype(BF16), tied_f.astype(BF16)) + tie_scr[0:1, :]
            keep = (key > thr) | (tied & (rank <= tie_scr[1:2, :]))
            madd_scr[...] = jnp.where(keep & causal, 0.0, NEG_INF)
            tie_scr[0:1, :] = tie_scr[0:1, :] + jnp.sum(tied_f, axis=0, keepdims=True)

        madd = madd_scr[...]
        base_delta = (q0 - c * tk) // BAND_BLOCK
        items = [(h, w) for h in range(A_HEADS) for w in range(tq // qw)]

        def scores(item):
            h, w = item
            return _dot_nt(kv, ql_ref[h, w * qw:(w + 1) * qw, :])

        pending = [scores(items[0]), scores(items[1])]
        for n_item, (h, w) in enumerate(items):
            qs = slice(w * qw, (w + 1) * qw)
            s_raw = pending.pop(0)
            if n_item + 2 < len(items):
                pending.append(scores(items[n_item + 2]))
            s_next = pending[0] if pending else None
            rows = []
            for j in range(nsub_k):
                tiles = []
                for a in range(w * qw // BAND_BLOCK, (w + 1) * qw // BAND_BLOCK):
                    delta = jnp.clip(base_delta + (a - j), 0, n_delta - 1)
                    tiles.append(ba_ref[h, delta])
                rows.append(tiles[0] if len(tiles) == 1 else jnp.concatenate(tiles, axis=1))
            bias = rows[0] if nsub_k == 1 else jnp.concatenate(rows, axis=0)
            s = s_raw + bias + madd[:, qs]
            m_prev = m_scr[h:h + 1, qs]
            m_new = jnp.maximum(m_prev, jnp.max(s, axis=0, keepdims=True))
            alpha = jnp.exp2(m_prev - m_new)
            p = jnp.exp2(s - m_new)
            m_scr[h:h + 1, qs] = m_new
            if s_next is not None:
                nxt = pltpu.bitcast(s_next[:SUBLANES], jnp.uint32)
                zero = lax.shift_right_logical(lax.shift_right_logical(nxt, jnp.uint32(16)), jnp.uint32(16))
                p = jnp.concatenate([p[:SUBLANES] + zero.astype(F32), p[SUBLANES:]], axis=0)
            acc_scr[h, :, qs] = acc_scr[h, :, qs] * alpha + _dot(kvt, p.astype(BF16))
        return carry

    lax.fori_loop(0, nck, att_body, 0)

    for h in range(A_HEADS):
        o_lat_t = (acc_scr[h, :A_KV_RANK, :] / acc_scr[h, A_KV_RANK:A_KV_RANK + 1, :]).astype(BF16)
        y_t = _dot(wuvt_ref[h], o_lat_t)
        o_ref[:, h * A_HEAD_DIM:(h + 1) * A_HEAD_DIM] = y_t.T.astype(o_ref.dtype)


def _dsa_attention(qi, wi_t, k2, ql, ckv, ckv_t, bias_a, wuv_t, batch, seq, tq):
    tk = tq
    n = batch * seq
    nq = seq // tq
    nck = seq // tk
    topk = min(DSA_TOPK, seq // 4)
    assert topk <= tk and seq % tq == 0 and tq % BAND_BLOCK == 0
    n_delta = bias_a.shape[1]
    kern = functools.partial(_dsa_kernel, tq=tq, tk=tk, topk=topk, n_delta=n_delta)
    once = pl.Buffered(1)
    return pl.pallas_call(
        kern,
        grid=(batch, nq),
        in_specs=[
            pl.BlockSpec((tq, IDX_Q_COLS), lambda b, i: (b * nq + i, 0)),
            pl.BlockSpec((LANES, tq), lambda b, i: (0, b * nq + i)),
            pl.BlockSpec((seq, LANES), lambda b, i: (b, 0), pipeline_mode=once),
            pl.BlockSpec((A_HEADS, tq, A_KV_RANK), lambda b, i: (0, b * nq + i, 0)),
            pl.BlockSpec((seq, A_KV_RANK), lambda b, i: (b, 0), pipeline_mode=once),
            pl.BlockSpec((nck, KV_ROWS, tk), lambda b, i: (b, 0, 0), pipeline_mode=once),
            pl.BlockSpec(bias_a.shape, lambda b, i: (0, 0, 0, 0), pipeline_mode=once),
            pl.BlockSpec(wuv_t.shape, lambda b, i: (0, 0, 0), pipeline_mode=once),
        ],
        out_specs=pl.BlockSpec((tq, A_OUT), lambda b, i: (b * nq + i, 0)),
        out_shape=jax.ShapeDtypeStruct((n, A_OUT), BF16),
        scratch_shapes=[
            pltpu.VMEM((nck, tk, tq), jnp.int32),
            pltpu.VMEM((nck, tk, tq), jnp.int16),
            pltpu.VMEM((nck, tk, tq), jnp.int16),
            pltpu.VMEM((IDX_HEADS, tq, LANES), BF16),
            pltpu.VMEM((A_HEADS, KV_ROWS, tq), F32),
            pltpu.VMEM((A_HEADS, tq), F32),
            pltpu.VMEM((tk, tq), F32),
            pltpu.VMEM((SUBLANES, tq), F32),
        ],
        compiler_params=_params("arbitrary", "arbitrary"),
        name="dsa_attention",
    )(qi, wi_t, k2, ql, ckv, ckv_t, bias_a, wuv_t)


def _deinterleave_matrix(tm, dil):
    per = tm // dil
    i = jnp.arange(tm)
    src = (i % per) * dil + i // per
    return (src[:, None] == i[None, :]).astype(BF16)


def _mm_stream_kernel(a_ref, w_ref, p_ref, o_ref):
    z = _dot(a_ref[...], w_ref[...]).astype(BF16)
    zs = _dot(p_ref[...], z).astype(BF16)
    o_ref[0] = zs.reshape(o_ref.shape[1:])


def _proj_dilated(h, w, batch, seq, dil, tm=256):
    n, d = h.shape
    c = w.shape[1]
    if dil == 1:
        return _matmul(h, w, BF16, 1024, c // 2, "proj_dilated_qkv_g0").reshape(batch, 1, seq, c)
    per_seq = seq // tm
    return pl.pallas_call(
        _mm_stream_kernel,
        grid=(n // tm,),
        in_specs=[
            pl.BlockSpec((tm, d), lambda i: (i, 0)),
            pl.BlockSpec((d, c), lambda i: (0, 0)),
            pl.BlockSpec((tm, tm), lambda i: (0, 0)),
        ],
        out_specs=pl.BlockSpec((1, dil, tm // dil, c), lambda i: (i // per_seq, 0, i % per_seq, 0)),
        out_shape=jax.ShapeDtypeStruct((batch, dil, seq // dil, c), BF16),
        compiler_params=_params("arbitrary"),
        name=f"proj_dilated_qkv_d{dil}",
    )(h, w, _deinterleave_matrix(tm, dil))


LSE_LANES = LANES // B_HEADS_PER_GROUP


def _dil_kernel(q_ref, kc_ref, kp_ref, vc_ref, vp_ref, bias_ref, o_ref, *, nsub, steps):
    first = pl.program_id(2) == 0
    P = BAND_BLOCK
    dh = B_HEAD_DIM
    row = lax.broadcasted_iota(jnp.int32, (P, 2 * P), 0)
    u = lax.broadcasted_iota(jnp.int32, (P, 2 * P), 1)
    back = row + P - u
    band = (back >= 0) & (back <= steps)
    band_first = band & ((u >= P) | jnp.logical_not(first))
    lane_head = lax.broadcasted_iota(jnp.int32, (P, LANES), 1) // LSE_LANES
    ones_cols = jnp.ones((2 * P, dh), BF16)
    for a in range(nsub):
        valid = band_first if a == 0 else band
        rs = slice(a * P, (a + 1) * P)
        lse_all = jnp.zeros((P, LANES), F32)
        for j in range(B_HEADS_PER_GROUP):
            cs = slice(j * dh, (j + 1) * dh)
            if a == 0:
                k_prev, v_prev = kp_ref[0, 0, :, cs], vp_ref[0, 0, :, cs]
            else:
                ps = slice((a - 1) * P, a * P)
                k_prev, v_prev = kc_ref[0, 0, ps, cs], vc_ref[0, 0, ps, cs]
            k_cat = jnp.concatenate([k_prev, kc_ref[0, 0, rs, cs]], axis=0)
            v_cat = jnp.concatenate([v_prev, vc_ref[0, 0, rs, cs]], axis=0)
            s = _dot_nt(q_ref[0, 0, rs, cs], k_cat) * (dh ** -0.5) + bias_ref[0, j]
            s = jnp.where(valid, s, NEG_INF)
            mx = jnp.max(s, axis=-1, keepdims=True)
            p = jnp.exp(s - mx).astype(BF16)
            pv = _dot(p, jnp.concatenate([v_cat, ones_cols], axis=1))
            den = pv[:, dh:]
            o_ref[0, 0, rs, cs] = pv[:, :dh] / den
            lse_all = jnp.where(lane_head == j, mx + jnp.log(den), lse_all)
        o_ref[0, 0, rs, B_OUT:] = lse_all


def _dilated_group(qkv, bias_b, g, batch, seq):
    window, dil = B_GROUPS[g]
    steps = window // dil
    assert steps <= BAND_BLOCK
    m = seq // dil
    rows = min(4 * BAND_BLOCK, m)
    assert m % rows == 0 and rows % BAND_BLOCK == 0
    nsub = rows // BAND_BLOCK
    width = B_OUT

    def cur(which):
        return pl.BlockSpec((1, 1, rows, width), lambda b, r, i: (b, r, i, which))

    def prev(which):
        return pl.BlockSpec((1, 1, BAND_BLOCK, width),
                            lambda b, r, i: (b, r, jnp.maximum(i * nsub - 1, 0), which))

    return pl.pallas_call(
        functools.partial(_dil_kernel, nsub=nsub, steps=steps),
        grid=(batch, dil, m // rows),
        in_specs=[cur(0), cur(1), prev(1), cur(2), prev(2),
                  pl.BlockSpec((1, B_HEADS_PER_GROUP, BAND_BLOCK, 2 * BAND_BLOCK),
                               lambda b, r, i: (g, 0, 0, 0))],
        out_specs=pl.BlockSpec((1, 1, rows, width + LANES), lambda b, r, i: (b, r, i, 0)),
        out_shape=jax.ShapeDtypeStruct((batch, dil, m, width + LANES), F32),
        compiler_params=_params("arbitrary", "arbitrary", "arbitrary"),
        name=f"dilated_attention_g{g}",
    )(qkv, qkv, qkv, qkv, qkv, bias_b)


def _to_token_order(blk, pinv):
    x = blk.reshape(pinv.shape[0], blk.shape[-1])
    hi = x.astype(BF16)
    r1 = x - hi.astype(F32)
    mid = r1.astype(BF16)
    lo = (r1 - mid.astype(F32)).astype(BF16)
    return (_dot(pinv, hi) + _dot(pinv, mid)) + _dot(pinv, lo)


def _dil_merge_kernel(t0_ref, t1_ref, t2_ref, p1_ref, p2_ref, y_ref):
    groups = [t0_ref[...], _to_token_order(t1_ref[0], p1_ref[...]), _to_token_order(t2_ref[0], p2_ref[...])]
    tm = y_ref.shape[0]
    dh = B_HEAD_DIM
    for j in range(B_HEADS_PER_GROUP):
        lse = [jnp.broadcast_to(t[:, B_OUT + j * LSE_LANES:B_OUT + j * LSE_LANES + 1], (tm, dh))
               for t in groups]
        mx = jnp.maximum(jnp.maximum(lse[0], lse[1]), lse[2])
        e = [jnp.exp(a - mx) for a in lse]
        den = e[0] + e[1] + e[2]
        y = sum((e[g] / den) * groups[g][:, j * dh:(j + 1) * dh] for g in range(3))
        y_ref[:, j * dh:(j + 1) * dh] = y.astype(y_ref.dtype)


def _dilated_merge(packed, batch, seq, tm=256):
    n = batch * seq
    w = packed[0].shape[-1]
    per_seq = seq // tm
    dils = [d for _, d in B_GROUPS]
    assert dils[0] == 1 and len(dils) == 3

    def stream(dil):
        return pl.BlockSpec((1, dil, tm // dil, w), lambda i: (i // per_seq, 0, i % per_seq, 0))

    pspec = pl.BlockSpec((tm, tm), lambda i: (0, 0))
    return pl.pallas_call(
        _dil_merge_kernel,
        grid=(n // tm,),
        in_specs=[pl.BlockSpec((tm, w), lambda i: (i, 0)), stream(dils[1]), stream(dils[2]), pspec, pspec],
        out_specs=pl.BlockSpec((tm, B_OUT), lambda i: (i, 0)),
        out_shape=jax.ShapeDtypeStruct((n, B_OUT), BF16),
        compiler_params=_params("arbitrary"),
        name="dilated_merge",
    )(packed[0].reshape(n, w), packed[1], packed[2],
      _deinterleave_matrix(tm, dils[1]).T, _deinterleave_matrix(tm, dils[2]).T)


def _merge_kernel(h_ref, ya_ref, yb_ref, wga_ref, wgb_ref, wa_ref, wb_ref, o_ref):
    h = h_ref[...]
    ga = jax.nn.sigmoid(_dot(h, wga_ref[...]))
    gb = jax.nn.sigmoid(_dot(h, wgb_ref[...]))
    merged = ga * _dot(ya_ref[...], wa_ref[...]) + gb * _dot(yb_ref[...], wb_ref[...])
    o_ref[...] = merged.astype(o_ref.dtype)


def _gated_merge(h, ya, yb, w_tail, gate_col, wa, wb, tm=1024, tn=512):
    n, d = h.shape
    assert gate_col % tn == 0 and d % tn == 0
    ga0, gb0 = gate_col // tn, (gate_col + d) // tn

    def rows(k):
        return pl.BlockSpec((tm, k), lambda i, j: (i, 0))

    def cols(k):
        return pl.BlockSpec((k, tn), lambda i, j: (0, j))

    return pl.pallas_call(
        _merge_kernel,
        grid=(n // tm, d // tn),
        in_specs=[rows(d), rows(A_OUT), rows(B_OUT),
                  pl.BlockSpec((d, tn), lambda i, j: (0, ga0 + j)),
                  pl.BlockSpec((d, tn), lambda i, j: (0, gb0 + j)),
                  cols(A_OUT), cols(B_OUT)],
        out_specs=pl.BlockSpec((tm, tn), lambda i, j: (i, j)),
        out_shape=jax.ShapeDtypeStruct((n, d), BF16),
        compiler_params=_params("arbitrary", "arbitrary"),
        name="gated_merge",
    )(h, ya, yb, w_tail, w_tail, wa, wb)


def _res_norm_kernel(a_ref, w_ref, x_ref, gate_ref, g_ref, sc_ref, sh_ref, xo_ref, h_ref):
    xn = x_ref[...] + gate_ref[0] * _dot(a_ref[...], w_ref[...])
    xo_ref[...] = xn
    h_ref[...] = (_rms(xn) * g_ref[...] * (1.0 + sc_ref[0]) + sh_ref[0]).astype(h_ref.dtype)


def _res_final_kernel(a_ref, w_ref, x_ref, gate_ref, g_ref, o_ref):
    xn = x_ref[...] + gate_ref[0] * _dot(a_ref[...], w_ref[...])
    o_ref[...] = _rms(xn) * g_ref[...]


def _matmul_residual_norm(a, w, x, gate, g, scale, shift, seq, tm, name):
    n, k = a.shape
    d = w.shape[1]
    per = seq // tm
    row = pl.BlockSpec((tm, d), lambda i: (i, 0))
    vec = pl.BlockSpec((1, 1, d), lambda i: (i // per, 0, 0))
    in_specs = [
        pl.BlockSpec((tm, k), lambda i: (i, 0)),
        pl.BlockSpec((k, d), lambda i: (0, 0), pipeline_mode=pl.Buffered(1)),
        row, vec,
        pl.BlockSpec((1, d), lambda i: (0, 0)),
    ]
    args = [a, w, x, gate, g.reshape(1, d)]
    if scale is None:
        return pl.pallas_call(
            _res_final_kernel, grid=(n // tm,), in_specs=in_specs, out_specs=row,
            out_shape=jax.ShapeDtypeStruct((n, d), F32),
            compiler_params=_params("arbitrary"), name=name,
        )(*args)
    return pl.pallas_call(
        _res_norm_kernel, grid=(n // tm,), in_specs=in_specs + [vec, vec], out_specs=[row, row],
        out_shape=[jax.ShapeDtypeStruct((n, d), F32), jax.ShapeDtypeStruct((n, d), BF16)],
        compiler_params=_params("arbitrary"), name=name,
    )(*args, scale, shift)


def _ffn_up_kernel(h_ref, wg_ref, wu_ref, o_ref, wg_scr, wu_scr):
    @pl.when(pl.program_id(1) == 0)
    def _():
        wg_scr[...] = wg_ref[...].astype(BF16)
        wu_scr[...] = wu_ref[...].astype(BF16)

    h = h_ref[...]
    a = _dot(h, wg_scr[...])
    o_ref[...] = (a * jax.nn.sigmoid(a) * _dot(h, wu_scr[...])).astype(o_ref.dtype)


def _ffn_up(h, wg, wu, layer, tm=1024, tn=512):
    n, d = h.shape
    f = wg.shape[2]
    wspec = pl.BlockSpec((None, d, tn), lambda j, i: (layer, 0, j))
    return pl.pallas_call(
        _ffn_up_kernel,
        grid=(f // tn, n // tm),
        in_specs=[pl.BlockSpec((tm, d), lambda j, i: (i, 0)), wspec, wspec],
        out_specs=pl.BlockSpec((tm, tn), lambda j, i: (i, j)),
        out_shape=jax.ShapeDtypeStruct((n, f), BF16),
        scratch_shapes=[pltpu.VMEM((d, tn), BF16), pltpu.VMEM((d, tn), BF16)],
        compiler_params=_params("arbitrary", "arbitrary"),
        name="ffn_up",
    )(h, wg, wu)


def _rope_tables(seq):
    half = IDX_ROPE_DIM // 2
    freqs = ROPE_THETA ** (-jnp.arange(half, dtype=F32) / half)
    ang = jnp.arange(seq).astype(F32)[:, None] * freqs[None, :]
    cos, sin = jnp.cos(ang), jnp.sin(ang)
    rest = IDX_HEAD_DIM - IDX_ROPE_DIM
    one = jnp.ones((seq, rest), F32)
    zr = jnp.zeros((seq, rest), F32)
    zh = jnp.zeros((seq, half), F32)
    cos_t = jnp.concatenate([cos, cos, one], axis=1)
    sin_lo = jnp.concatenate([-sin, zh, zr], axis=1)
    sin_hi = jnp.concatenate([zh, sin, zr], axis=1)
    rep = LANES // IDX_HEAD_DIM
    return tuple(jnp.tile(t, (1, rep)) for t in (cos_t, sin_lo, sin_hi))


def _layer(layer, x, h, h_mod, next_norm, tables, bias_a, bias_b, batch, seq, w_in, kv_norm_g, idx_ln_g,
           idx_ln_b, w_uk, w_uv, w_a_up, w_b_up, w_out, norm2_g, w_ff_gate, w_ff_up, w_ff_down):
    _, _, gate1, shift2, scale2, gate2 = h_mod
    offs = [0]
    for s in IN_SIZES:
        offs.append(offs[-1] + s)
    w_qa, w_kv, w_qi, w_ki, w_wi = [w_in[:, offs[k]:offs[k + 1]] for k in range(5)]
    w_tail = w_in[:, offs[5]:].astype(BF16)
    w_qkvb = w_tail[:, :B_QKV_COLS]
    d = w_in.shape[0]
    rep = LANES // IDX_HEAD_DIM
    w_kw = jnp.concatenate(
        [w_ki] * rep + [w_wi, jnp.zeros((d, LANES - IDX_HEADS), w_in.dtype)], axis=1).astype(BF16)
    ln_g2 = jnp.tile(idx_ln_g, rep).reshape(1, LANES)
    ln_b2 = jnp.tile(idx_ln_b, rep).reshape(1, LANES)

    ql = _proj_qlat(h, w_qa.astype(BF16), w_uk.astype(BF16))
    ckv, ckv_t = _proj_ckv(h, w_kv.astype(BF16), kv_norm_g, DSA_TILE)
    qi = _proj_qi(h, w_qi.astype(BF16), tables, seq)
    k2, wi_t = _proj_kw(h, w_kw, ln_g2, ln_b2, tables, seq)
    wuv_t = jnp.swapaxes(w_uv, 1, 2).astype(BF16)
    ya = _dsa_attention(qi, wi_t, k2, ql, ckv, ckv_t, bias_a, wuv_t, batch, seq, DSA_TILE)

    ng = len(B_GROUPS)
    packed = []
    for g, (_, dil) in enumerate(B_GROUPS):
        w_g = jnp.concatenate([w_qkvb[:, (s * ng + g) * B_OUT:(s * ng + g + 1) * B_OUT] for s in range(3)],
                              axis=1)
        packed.append(_dilated_group(_proj_dilated(h, w_g, batch, seq, dil), bias_b, g, batch, seq))
    yb = _dilated_merge(packed, batch, seq)

    merged = _gated_merge(h, ya, yb, w_tail, B_QKV_COLS, w_a_up.astype(BF16), w_b_up.astype(BF16))
    x, h2 = _matmul_residual_norm(merged, w_out.astype(BF16), x, gate1, norm2_g, scale2, shift2, seq,
                                  512, "out_proj_residual")
    act = _ffn_up(h2, w_ff_gate, w_ff_up, layer)
    return _matmul_residual_norm(act, w_ff_down.astype(BF16), x, gate2, *next_norm, seq, 256,
                                 "ffn_down_residual")


def kernel(x, c, rel_bias, w_ada, b_ada, norm1_g, w_in, kv_norm_g, idx_ln_g, idx_ln_b, w_uk, w_uv,
           w_a_up, w_b_up, w_out, norm2_g, w_ff_gate, w_ff_up, w_ff_down, final_g):
    batch, seq, d = x.shape
    depth = w_ada.shape[0]
    assert d == D_MODEL and seq % (B_GROUPS[-1][1] * BAND_BLOCK) == 0
    n = batch * seq
    rows = -(-batch // SUBLANES) * SUBLANES
    mod = _modulation(jnp.pad(c, ((0, rows - batch), (0, 0))), w_ada, b_ada)
    mod = mod[:, :batch].reshape(depth, batch, 6, 1, d)
    bias_a = _bias_a_table(rel_bias, min(seq // BAND_BLOCK, FAR_DELTA + 1))
    bias_b = _bias_b_table(rel_bias)
    tables = _rope_tables(seq)
    xf = x.reshape(n, d)
    h_mods = [[mod[l, :, k] for k in range(6)] for l in range(depth)]
    h = _norm_mod(xf, norm1_g[0], h_mods[0][1], h_mods[0][0], seq)
    for l in range(depth):
        last = l + 1 == depth
        next_norm = (final_g, None, None) if last else (norm1_g[l + 1], h_mods[l + 1][1], h_mods[l + 1][0])
        out = _layer(l, xf, h, h_mods[l], next_norm, tables, bias_a, bias_b, batch, seq, w_in[l],
                     kv_norm_g[l], idx_ln_g[l], idx_ln_b[l], w_uk[l], w_uv[l], w_a_up[l], w_b_up[l], w_out[l],
                     norm2_g[l], w_ff_gate, w_ff_up, w_ff_down[l])
        if last:
            return out.reshape(batch, seq, d)
        xf, h = out
```

```python
import functools
import math

import jax
import jax.numpy as jnp
from jax import lax
from jax.experimental import pallas as pl
from jax.experimental.pallas import tpu as pltpu

D_MODEL = 2048
A_HEADS = 8
A_HEAD_DIM = 128
A_KV_RANK = 256
IDX_HEADS = 16
IDX_HEAD_DIM = 64
IDX_ROPE_DIM = 32
ROPE_THETA = 10000.0
DSA_TOPK = 256
B_GROUPS = ((128, 1), (512, 4), (2048, 16))
B_HEADS_PER_GROUP = 4
B_HEAD_DIM = 128
B_HEADS = B_HEADS_PER_GROUP * len(B_GROUPS)
BAND_BLOCK = 128
REL_BUCKETS = 32
REL_MAX_DISTANCE = 2048
N_BIAS_HEADS = A_HEADS + B_HEADS
D_FF = -(-8 * D_MODEL // (3 * 256)) * 256
NORM_EPS = 1e-6
NEG_INF = -1e30

A_Q_COLS = A_HEADS * A_HEAD_DIM
IDX_Q_COLS = IDX_HEADS * IDX_HEAD_DIM
B_QKV_COLS = 3 * B_HEADS * B_HEAD_DIM
GATE_COLS = 2 * D_MODEL
IN_SIZES = (A_Q_COLS, A_KV_RANK, IDX_Q_COLS, IDX_HEAD_DIM, IDX_HEADS, B_QKV_COLS, GATE_COLS)
A_OUT = A_HEADS * A_HEAD_DIM
B_OUT = B_HEADS_PER_GROUP * B_HEAD_DIM

LANES = 128
SUBLANES = 8
VMEM_LIMIT_BYTES = 56 * 1024 * 1024

BF16 = jnp.bfloat16
F32 = jnp.float32
LOG2E = math.log2(math.e)
HALF_BITS = 16
HALF_MASK = 2 ** HALF_BITS - 1
HALF_BIAS = 2 ** (HALF_BITS - 1)
PACKED_ROWS = 2 * SUBLANES
KV_ROWS = A_KV_RANK + PACKED_ROWS

FAR_DELTA = -(-(REL_MAX_DISTANCE + BAND_BLOCK - 1) // BAND_BLOCK)
DSA_TILE = 512
DSA_STRIP = 256
DIL_ROWS = 4 * BAND_BLOCK
ROWS_STREAM = 1024
ROWS_OUT_PROJ = 512
ROWS_FFN_DOWN = 256


def _dot(a, b):
    return jnp.dot(a, b, preferred_element_type=F32)


def _dot_nt(a, b):
    return lax.dot_general(a, b, (((1,), (1,)), ((), ())), preferred_element_type=F32)


def _params(*sem):
    return pltpu.CompilerParams(dimension_semantics=sem, vmem_limit_bytes=VMEM_LIMIT_BYTES)


def _mod_kernel(c_ref, w_ref, b_ref, o_ref):
    c = c_ref[...]
    a = (c * jax.nn.sigmoid(c)).astype(BF16)
    o_ref[0] = _dot(a, w_ref[0].astype(BF16)) + b_ref[0]


def _modulation(c_pad, w_ada, b_ada, tn=2048):
    depth, d, n6 = w_ada.shape
    rows = c_pad.shape[0]
    return pl.pallas_call(
        _mod_kernel,
        grid=(depth, n6 // tn),
        in_specs=[
            pl.BlockSpec((rows, d), lambda l, j: (0, 0)),
            pl.BlockSpec((1, d, tn), lambda l, j: (l, 0, j)),
            pl.BlockSpec((1, 1, tn), lambda l, j: (l, 0, j)),
        ],
        out_specs=pl.BlockSpec((1, rows, tn), lambda l, j: (l, 0, j)),
        out_shape=jax.ShapeDtypeStruct((depth, rows, n6), F32),
        compiler_params=_params("arbitrary", "arbitrary"),
        name="adaln_modulation",
    )(c_pad, w_ada, b_ada.reshape(depth, 1, n6))


def _bucket(dist):
    n = jnp.maximum(dist, 0)
    exact = REL_BUCKETS // 2
    nf = jnp.maximum(n, 1).astype(F32)
    large = exact + (jnp.log(nf / exact) / math.log(REL_MAX_DISTANCE / exact)
                     * (REL_BUCKETS - exact)).astype(jnp.int32)
    return jnp.where(n < exact, n, jnp.minimum(large, REL_BUCKETS - 1))


def _lookup(rb_ref, bucket, head):
    t = jnp.zeros(bucket.shape, F32)
    for k in range(REL_BUCKETS):
        t = jnp.where(bucket == k, rb_ref[k, head], t)
    return t


def _bias_a_kernel(rb_ref, o_ref):
    delta = pl.program_id(0)
    row = lax.broadcasted_iota(jnp.int32, (BAND_BLOCK, BAND_BLOCK), 0)
    col = lax.broadcasted_iota(jnp.int32, (BAND_BLOCK, BAND_BLOCK), 1)
    bucket = _bucket(delta * BAND_BLOCK + col - row)
    for h in range(A_HEADS):
        o_ref[h, 0] = _lookup(rb_ref, bucket, h) * LOG2E


def _bias_a_table(rel_bias, n_delta):
    return pl.pallas_call(
        _bias_a_kernel,
        grid=(n_delta,),
        in_specs=[pl.BlockSpec(memory_space=pltpu.SMEM)],
        out_specs=pl.BlockSpec((A_HEADS, 1, BAND_BLOCK, BAND_BLOCK), lambda d: (0, d, 0, 0)),
        out_shape=jax.ShapeDtypeStruct((A_HEADS, n_delta, BAND_BLOCK, BAND_BLOCK), F32),
        compiler_params=_params("arbitrary"),
        name="rel_bias_table_a",
    )(rel_bias)


def _bias_b_kernel(rb_ref, o_ref):
    g = pl.program_id(0)
    dil = jnp.where(g == 0, B_GROUPS[0][1], jnp.where(g == 1, B_GROUPS[1][1], B_GROUPS[2][1]))
    row = lax.broadcasted_iota(jnp.int32, (BAND_BLOCK, 2 * BAND_BLOCK), 0)
    u = lax.broadcasted_iota(jnp.int32, (BAND_BLOCK, 2 * BAND_BLOCK), 1)
    bucket = _bucket((row + BAND_BLOCK - u) * dil)
    for j in range(B_HEADS_PER_GROUP):
        o_ref[0, j] = _lookup(rb_ref, bucket, A_HEADS + g * B_HEADS_PER_GROUP + j)


def _bias_b_table(rel_bias):
    ng = len(B_GROUPS)
    return pl.pallas_call(
        _bias_b_kernel,
        grid=(ng,),
        in_specs=[pl.BlockSpec(memory_space=pltpu.SMEM)],
        out_specs=pl.BlockSpec((1, B_HEADS_PER_GROUP, BAND_BLOCK, 2 * BAND_BLOCK),
                               lambda g: (g, 0, 0, 0)),
        out_shape=jax.ShapeDtypeStruct((ng, B_HEADS_PER_GROUP, BAND_BLOCK, 2 * BAND_BLOCK), F32),
        compiler_params=_params("arbitrary"),
        name="rel_bias_table_b",
    )(rel_bias)


def _rms(x):
    return x * lax.rsqrt(jnp.mean(x * x, axis=-1, keepdims=True) + NORM_EPS)


def _norm_mod_kernel(x_ref, g_ref, sc_ref, sh_ref, o_ref):
    y = _rms(x_ref[...]) * g_ref[...]
    o_ref[...] = (y * (1.0 + sc_ref[0]) + sh_ref[0]).astype(o_ref.dtype)


def _norm_mod(x, g, scale, shift, seq, tm=512):
    n, d = x.shape
    per = seq // tm
    return pl.pallas_call(
        _norm_mod_kernel,
        grid=(n // tm,),
        in_specs=[
            pl.BlockSpec((tm, d), lambda i: (i, 0)),
            pl.BlockSpec((1, d), lambda i: (0, 0)),
            pl.BlockSpec((1, 1, d), lambda i: (i // per, 0, 0)),
            pl.BlockSpec((1, 1, d), lambda i: (i // per, 0, 0)),
        ],
        out_specs=pl.BlockSpec((tm, d), lambda i: (i, 0)),
        out_shape=jax.ShapeDtypeStruct((n, d), BF16),
        compiler_params=_params("arbitrary"),
        name="norm_modulate",
    )(x, g.reshape(1, d), scale, shift)


def _qlat_kernel(h_ref, w_ref, wuk_ref, o_ref):
    z = _dot(h_ref[...], w_ref[...])
    for hd in range(A_HEADS):
        zh = z[:, hd * A_HEAD_DIM:(hd + 1) * A_HEAD_DIM].astype(BF16)
        o_ref[hd] = (_dot(zh, wuk_ref[hd]) * (A_HEAD_DIM ** -0.5 * LOG2E)).astype(o_ref.dtype)


def _proj_qlat(h, w, wuk, tm=512):
    n, d = h.shape
    return pl.pallas_call(
        _qlat_kernel,
        grid=(n // tm,),
        in_specs=[
            pl.BlockSpec((tm, d), lambda i: (i, 0)),
            pl.BlockSpec((d, A_Q_COLS), lambda i: (0, 0)),
            pl.BlockSpec((A_HEADS, A_HEAD_DIM, A_KV_RANK), lambda i: (0, 0, 0)),
        ],
        out_specs=pl.BlockSpec((A_HEADS, tm, A_KV_RANK), lambda i: (0, i, 0)),
        out_shape=jax.ShapeDtypeStruct((A_HEADS, n, A_KV_RANK), BF16),
        compiler_params=_params("arbitrary"),
        name="proj_q_latent",
    )(h, w, wuk)


def _ckv_kernel(h_ref, w_ref, g_ref, o_ref, ot_ref, *, tk):
    z = _dot(h_ref[...], w_ref[...])
    ckv = _rms(z) * g_ref[...]
    o_ref[...] = ckv.astype(o_ref.dtype)
    ones = jnp.ones((KV_ROWS - A_KV_RANK, tk), F32)
    for s in range(ot_ref.shape[0]):
        ot_ref[s] = jnp.concatenate([ckv[s * tk:(s + 1) * tk, :].T, ones], axis=0).astype(ot_ref.dtype)


def _proj_ckv(h, w, g, tk, tm=1024):
    n, d = h.shape
    return pl.pallas_call(
        functools.partial(_ckv_kernel, tk=tk),
        grid=(n // tm,),
        in_specs=[
            pl.BlockSpec((tm, d), lambda i: (i, 0)),
            pl.BlockSpec((d, A_KV_RANK), lambda i: (0, 0)),
            pl.BlockSpec((1, A_KV_RANK), lambda i: (0, 0)),
        ],
        out_specs=[pl.BlockSpec((tm, A_KV_RANK), lambda i: (i, 0)),
                   pl.BlockSpec((tm // tk, KV_ROWS, tk), lambda i: (i, 0, 0))],
        out_shape=[jax.ShapeDtypeStruct((n, A_KV_RANK), BF16),
                   jax.ShapeDtypeStruct((n // tk, KV_ROWS, tk), BF16)],
        compiler_params=_params("arbitrary"),
        name="proj_latent_kv",
    )(h, w, g.reshape(1, A_KV_RANK))


def _rope(z, cos_t, sin_lo, sin_hi):
    half = IDX_ROPE_DIM // 2
    return (z * cos_t + pltpu.roll(z, half, 1) * sin_hi
            + pltpu.roll(z, LANES - half, 1) * sin_lo)


def _qi_kernel(h_ref, w_ref, cos_ref, slo_ref, shi_ref, o_ref):
    z = _dot(h_ref[...], w_ref[...])
    cos_t, slo, shi = cos_ref[...], slo_ref[...], shi_ref[...]
    for s in range(IDX_Q_COLS // LANES):
        zs = z[:, s * LANES:(s + 1) * LANES]
        o_ref[:, s * LANES:(s + 1) * LANES] = _rope(zs, cos_t, slo, shi).astype(o_ref.dtype)


def _proj_qi(h, w, tables, seq, tm=512):
    n, d = h.shape
    per = seq // tm
    tspec = pl.BlockSpec((tm, LANES), lambda i: (i % per, 0))
    return pl.pallas_call(
        _qi_kernel,
        grid=(n // tm,),
        in_specs=[
            pl.BlockSpec((tm, d), lambda i: (i, 0)),
            pl.BlockSpec((d, IDX_Q_COLS), lambda i: (0, 0)),
            tspec, tspec, tspec,
        ],
        out_specs=pl.BlockSpec((tm, IDX_Q_COLS), lambda i: (i, 0)),
        out_shape=jax.ShapeDtypeStruct((n, IDX_Q_COLS), BF16),
        compiler_params=_params("arbitrary"),
        name="proj_index_q",
    )(h, w, *tables)


def _kw_kernel(h_ref, w_ref, g_ref, b_ref, cos_ref, slo_ref, shi_ref, k_ref, wi_ref):
    z = _dot(h_ref[...], w_ref[...])
    zk = z[:, :LANES]
    mu = jnp.mean(zk, axis=-1, keepdims=True)
    var = jnp.mean(jnp.square(zk - mu), axis=-1, keepdims=True)
    kn = (zk - mu) * lax.rsqrt(var + NORM_EPS) * g_ref[...] + b_ref[...]
    k_ref[...] = _rope(kn, cos_ref[...], slo_ref[...], shi_ref[...]).astype(k_ref.dtype)
    wi_ref[...] = (z[:, LANES:] * (IDX_HEADS ** -0.5 * IDX_HEAD_DIM ** -0.5)).T


def _proj_kw(h, w, g2, b2, tables, seq, tm=1024):
    n, d = h.shape
    per = seq // tm
    tspec = pl.BlockSpec((tm, LANES), lambda i: (i % per, 0))
    vspec = pl.BlockSpec((1, LANES), lambda i: (0, 0))
    ospec = pl.BlockSpec((tm, LANES), lambda i: (i, 0))
    return pl.pallas_call(
        _kw_kernel,
        grid=(n // tm,),
        in_specs=[
            pl.BlockSpec((tm, d), lambda i: (i, 0)),
            pl.BlockSpec((d, 2 * LANES), lambda i: (0, 0)),
            vspec, vspec, tspec, tspec, tspec,
        ],
        out_specs=[ospec, pl.BlockSpec((LANES, tm), lambda i: (0, i))],
        out_shape=[jax.ShapeDtypeStruct((n, LANES), BF16), jax.ShapeDtypeStruct((LANES, n), F32)],
        compiler_params=_params("arbitrary"),
        name="proj_index_kw",
    )(h, w, g2, b2, *tables)


def _mm_kernel(a_ref, w_ref, o_ref):
    o_ref[...] = _dot(a_ref[...], w_ref[...]).astype(o_ref.dtype)


def _matmul(a, w, out_dtype, tm, tn, name):
    n, k = a.shape
    cols = w.shape[1]
    return pl.pallas_call(
        _mm_kernel,
        grid=(n // tm, cols // tn),
        in_specs=[pl.BlockSpec((tm, k), lambda i, j: (i, 0)), pl.BlockSpec((k, tn), lambda i, j: (0, j))],
        out_specs=pl.BlockSpec((tm, tn), lambda i, j: (i, j)),
        out_shape=jax.ShapeDtypeStruct((n, cols), out_dtype),
        compiler_params=_params("arbitrary", "arbitrary"),
        name=name,
    )(a, w)


def _dsa_kernel(qi_ref, wi_ref, k2_ref, ql_ref, ckv_ref, ckvt_ref, ba_ref, wuvt_ref, o_ref,
                key_scr, hi_scr, lo_scr, qm_scr, acc_scr, m_scr, madd_scr, tie_scr,
                *, tq, tk, topk, n_delta):
    i = pl.program_id(1)
    q0 = i * tq
    nck = (q0 + tq + tk - 1) // tk
    nsub_k = tk // BAND_BLOCK
    qw = min(tq, DSA_STRIP)
    key_row = lax.broadcasted_iota(jnp.int32, (tk, tq), 0)
    q_pos = q0 + lax.broadcasted_iota(jnp.int32, (tk, tq), 1)
    lane = lax.broadcasted_iota(jnp.int32, (tq, LANES), 1)

    for h in range(IDX_HEADS):
        qs = qi_ref[:, (h // 2) * LANES:(h // 2 + 1) * LANES].astype(F32)
        keep = (lane >= IDX_HEAD_DIM) if h % 2 else (lane < IDX_HEAD_DIM)
        qm_scr[h] = jnp.where(keep, qs, 0.0).astype(BF16)

    def idx_body(c, carry):
        k2 = k2_ref[pl.ds(pl.multiple_of(c * tk, tk), tk), :]
        score = jnp.zeros((tk, tq), F32)
        for h in range(IDX_HEADS):
            logits = _dot_nt(k2, qm_scr[h])
            score = score + wi_ref[h:h + 1, :] * jnp.maximum(logits, 0.0)
        score = jnp.where(c * tk + key_row <= q_pos, score, -jnp.inf)
        bits = jnp.where(score == 0.0, 0, pltpu.bitcast(score, jnp.int32))
        key = bits ^ ((bits >> 31) & 0x7FFFFFFF)
        key_scr[c] = key
        hi_scr[c] = (key >> HALF_BITS).astype(jnp.int16)
        lo_scr[c] = ((key & HALF_MASK) - HALF_BIAS).astype(jnp.int16)
        return carry

    lax.fori_loop(0, nck, idx_body, 0)

    def count16(mask_fn):
        def cnt_body(c, part):
            hit = mask_fn(c).astype(jnp.int16)
            for g in range(tk // PACKED_ROWS):
                part = part + hit[g * PACKED_ROWS:(g + 1) * PACKED_ROWS]
            return part

        part = lax.fori_loop(0, nck, cnt_body, jnp.zeros((PACKED_ROWS, tq), jnp.int16))
        return jnp.sum(part.astype(F32), axis=0, keepdims=True)

    def select16(src_scr, need, n_all):
        def bit_body(b, carry):
            tx, n_ge = carry
            cand_x = tx | jnp.left_shift(jnp.int32(1), HALF_BITS - 1 - b)
            cand = (cand_x - HALF_BIAS).astype(jnp.int16)
            cnt = count16(lambda c: src_scr[c] >= cand)
            take = cnt >= need
            return jnp.where(take, cand_x, tx), jnp.where(take, cnt, n_ge)

        tx, n_ge = lax.fori_loop(0, HALF_BITS, bit_body, (jnp.zeros((1, tq), jnp.int32), n_all))
        return tx - HALF_BIAS, n_ge

    n_keys = jnp.full((1, tq), 1.0, F32) * (nck * tk).astype(F32)
    t_hi, n_hi_ge = select16(hi_scr, topk, n_keys)
    t_hi16 = t_hi.astype(jnp.int16)

    def tie_body(c, carry):
        lo_scr[c] = jnp.where(hi_scr[c] == t_hi16, lo_scr[c], jnp.int16(-HALF_BIAS))
        return carry

    lax.fori_loop(0, nck, tie_body, 0)
    above = count16(lambda c: hi_scr[c] > t_hi16)
    t_lo, n_lo_ge = select16(lo_scr, topk - above, n_hi_ge - above)
    thr = (t_hi << HALF_BITS) | (t_lo + HALF_BIAS)
    n_ge = above + n_lo_ge

    def count32(mask_fn):
        def cnt_body(c, part):
            hit = mask_fn(key_scr[c]).astype(jnp.int32)
            return part + jnp.sum(hit.reshape(tk // SUBLANES, SUBLANES, tq), axis=0)

        part = lax.fori_loop(0, nck, cnt_body, jnp.zeros((SUBLANES, tq), jnp.int32))
        return jnp.sum(part.astype(F32), axis=0, keepdims=True)

    has_tie = jnp.max(n_ge) > topk
    tie_scr[0:1, :] = jnp.zeros((1, tq), F32)

    @pl.when(has_tie)
    def _():
        tie_scr[1:2, :] = topk - count32(lambda key: key > thr)

    m_scr[...] = jnp.full(m_scr.shape, NEG_INF, F32)
    acc_scr[...] = jnp.zeros(acc_scr.shape, F32)

    def att_body(c, carry):
        kv = ckv_ref[pl.ds(pl.multiple_of(c * tk, tk), tk), :]
        kvt = ckvt_ref[c]
        causal = c * tk + key_row <= q_pos

        @pl.when(jnp.logical_not(has_tie))
        def _():
            madd_scr[...] = jnp.where((key_scr[c] >= thr) & causal, 0.0, NEG_INF)

        @pl.when(has_tie)
        def _():
            key = key_scr[c]
            tied = key == thr
            below = (lax.broadcasted_iota(jnp.int32, (tk, tk), 0)
                     >= lax.broadcasted_iota(jnp.int32, (tk, tk), 1))
            tied_f = tied.astype(F32)
            rank = _dot(below.astype(F32).astype(BF16), tied_f.astype(BF16)) + tie_scr[0:1, :]
            keep = (key > thr) | (tied & (rank <= tie_scr[1:2, :]))
            madd_scr[...] = jnp.where(keep & causal, 0.0, NEG_INF)
            tie_scr[0:1, :] = tie_scr[0:1, :] + jnp.sum(tied_f, axis=0, keepdims=True)

        madd = madd_scr[...]
        base_delta = (q0 - c * tk) // BAND_BLOCK
        items = [(h, w) for h in range(A_HEADS) for w in range(tq // qw)]

        def scores(item):
            h, w = item
            return _dot_nt(kv, ql_ref[h, w * qw:(w + 1) * qw, :])

        pending = [scores(items[0]), scores(items[1])]
        for n_item, (h, w) in enumerate(items):
            qs = slice(w * qw, (w + 1) * qw)
            s_raw = pending.pop(0)
            if n_item + 2 < len(items):
                pending.append(scores(items[n_item + 2]))
            s_next = pending[0] if pending else None
            rows = []
            for j in range(nsub_k):
                tiles = []
                for a in range(w * qw // BAND_BLOCK, (w + 1) * qw // BAND_BLOCK):
                    delta = jnp.clip(base_delta + (a - j), 0, n_delta - 1)
                    tiles.append(ba_ref[h, delta])
                rows.append(tiles[0] if len(tiles) == 1 else jnp.concatenate(tiles, axis=1))
            bias = rows[0] if nsub_k == 1 else jnp.concatenate(rows, axis=0)
            s = s_raw + bias + madd[:, qs]
            m_prev = m_scr[h:h + 1, qs]
            m_new = jnp.maximum(m_prev, jnp.max(s, axis=0, keepdims=True))
            alpha = jnp.exp2(m_prev - m_new)
            p = jnp.exp2(s - m_new)
            m_scr[h:h + 1, qs] = m_new
            if s_next is not None:
                nxt = pltpu.bitcast(s_next[:SUBLANES], jnp.uint32)
                zero = lax.shift_right_logical(lax.shift_right_logical(nxt, jnp.uint32(16)), jnp.uint32(16))
                p = jnp.concatenate([p[:SUBLANES] + zero.astype(F32), p[SUBLANES:]], axis=0)
            acc_scr[h, :, qs] = acc_scr[h, :, qs] * alpha + _dot(kvt, p.astype(BF16))
        return carry

    lax.fori_loop(0, nck, att_body, 0)

    for h in range(A_HEADS):
        o_lat_t = (acc_scr[h, :A_KV_RANK, :] / acc_scr[h, A_KV_RANK:A_KV_RANK + 1, :]).astype(BF16)
        y_t = _dot(wuvt_ref[h], o_lat_t)
        o_ref[:, h * A_HEAD_DIM:(h + 1) * A_HEAD_DIM] = y_t.T.astype(o_ref.dtype)


def _dsa_attention(qi, wi_t, k2, ql, ckv, ckv_t, bias_a, wuv_t, batch, seq, tq):
    tk = tq
    n = batch * seq
    nq = seq // tq
    nck = seq // tk
    topk = min(DSA_TOPK, seq // 4)
    assert topk <= tk and seq % tq == 0 and tq % BAND_BLOCK == 0
    n_delta = bias_a.shape[1]
    kern = functools.partial(_dsa_kernel, tq=tq, tk=tk, topk=topk, n_delta=n_delta)
    once = pl.Buffered(1)
    return pl.pallas_call(
        kern,
        grid=(batch, nq),
        in_specs=[
            pl.BlockSpec((tq, IDX_Q_COLS), lambda b, i: (b * nq + i, 0)),
            pl.BlockSpec((LANES, tq), lambda b, i: (0, b * nq + i)),
            pl.BlockSpec((seq, LANES), lambda b, i: (b, 0), pipeline_mode=once),
            pl.BlockSpec((A_HEADS, tq, A_KV_RANK), lambda b, i: (0, b * nq + i, 0)),
            pl.BlockSpec((seq, A_KV_RANK), lambda b, i: (b, 0), pipeline_mode=once),
            pl.BlockSpec((nck, KV_ROWS, tk), lambda b, i: (b, 0, 0), pipeline_mode=once),
            pl.BlockSpec(bias_a.shape, lambda b, i: (0, 0, 0, 0), pipeline_mode=once),
            pl.BlockSpec(wuv_t.shape, lambda b, i: (0, 0, 0), pipeline_mode=once),
        ],
        out_specs=pl.BlockSpec((tq, A_OUT), lambda b, i: (b * nq + i, 0)),
        out_shape=jax.ShapeDtypeStruct((n, A_OUT), BF16),
        scratch_shapes=[
            pltpu.VMEM((nck, tk, tq), jnp.int32),
            pltpu.VMEM((nck, tk, tq), jnp.int16),
            pltpu.VMEM((nck, tk, tq), jnp.int16),
            pltpu.VMEM((IDX_HEADS, tq, LANES), BF16),
            pltpu.VMEM((A_HEADS, KV_ROWS, tq), F32),
            pltpu.VMEM((A_HEADS, tq), F32),
            pltpu.VMEM((tk, tq), F32),
            pltpu.VMEM((SUBLANES, tq), F32),
        ],
        compiler_params=_params("arbitrary", "arbitrary"),
        name="dsa_attention",
    )(qi, wi_t, k2, ql, ckv, ckv_t, bias_a, wuv_t)


def _deinterleave_matrix(tm, dil):
    per = tm // dil
    i = jnp.arange(tm)
    src = (i % per) * dil + i // per
    return (src[:, None] == i[None, :]).astype(BF16)


def _mm_stream_kernel(a_ref, w_ref, p_ref, o_ref):
    z = _dot(a_ref[...], w_ref[...]).astype(BF16)
    zs = _dot(p_ref[...], z).astype(BF16)
    o_ref[0] = zs.reshape(o_ref.shape[1:])


def _proj_dilated(h, w, batch, seq, dil, tm=256):
    n, d = h.shape
    c = w.shape[1]
    if dil == 1:
        return _matmul(h, w, BF16, ROWS_STREAM, c // 2, "proj_dilated_qkv_g0").reshape(batch, 1, seq, c)
    per_seq = seq // tm
    return pl.pallas_call(
        _mm_stream_kernel,
        grid=(n // tm,),
        in_specs=[
            pl.BlockSpec((tm, d), lambda i: (i, 0)),
            pl.BlockSpec((d, c), lambda i: (0, 0)),
            pl.BlockSpec((tm, tm), lambda i: (0, 0)),
        ],
        out_specs=pl.BlockSpec((1, dil, tm // dil, c), lambda i: (i // per_seq, 0, i % per_seq, 0)),
        out_shape=jax.ShapeDtypeStruct((batch, dil, seq // dil, c), BF16),
        compiler_params=_params("arbitrary"),
        name=f"proj_dilated_qkv_d{dil}",
    )(h, w, _deinterleave_matrix(tm, dil))


LSE_LANES = LANES // B_HEADS_PER_GROUP


def _dil_kernel(q_ref, kc_ref, kp_ref, vc_ref, vp_ref, bias_ref, o_ref, *, nres, nsub, steps):
    first = pl.program_id(2) == 0
    P = BAND_BLOCK
    dh = B_HEAD_DIM
    row = lax.broadcasted_iota(jnp.int32, (P, 2 * P), 0)
    u = lax.broadcasted_iota(jnp.int32, (P, 2 * P), 1)
    back = row + P - u
    band = (back >= 0) & (back <= steps)
    band_first = band & ((u >= P) | jnp.logical_not(first))
    lane_head = lax.broadcasted_iota(jnp.int32, (P, LANES), 1) // LSE_LANES
    ones_cols = jnp.ones((2 * P, dh), BF16)
    for rr, a in [(rr, a) for rr in range(nres) for a in range(nsub)]:
        valid = band_first if a == 0 else band
        rs = slice(a * P, (a + 1) * P)
        lse_all = jnp.zeros((P, LANES), F32)
        for j in range(B_HEADS_PER_GROUP):
            cs = slice(j * dh, (j + 1) * dh)
            if a == 0:
                k_prev, v_prev = kp_ref[0, rr, :, cs], vp_ref[0, rr, :, cs]
            else:
                ps = slice((a - 1) * P, a * P)
                k_prev, v_prev = kc_ref[0, rr, ps, cs], vc_ref[0, rr, ps, cs]
            k_cat = jnp.concatenate([k_prev, kc_ref[0, rr, rs, cs]], axis=0)
            v_cat = jnp.concatenate([v_prev, vc_ref[0, rr, rs, cs]], axis=0)
            s = _dot_nt(q_ref[0, rr, rs, cs], k_cat) * (dh ** -0.5) + bias_ref[0, j]
            s = jnp.where(valid, s, NEG_INF)
            mx = jnp.max(s, axis=-1, keepdims=True)
            p = jnp.exp(s - mx).astype(BF16)
            pv = _dot(p, jnp.concatenate([v_cat, ones_cols], axis=1))
            den = pv[:, dh:]
            o_ref[0, rr, rs, cs] = pv[:, :dh] / den
            lse_all = jnp.where(lane_head == j, mx + jnp.log(den), lse_all)
        o_ref[0, rr, rs, B_OUT:] = lse_all


def _dilated_group(qkv, bias_b, g, batch, seq):
    window, dil = B_GROUPS[g]
    steps = window // dil
    assert steps <= BAND_BLOCK
    m = seq // dil
    rows = min(DIL_ROWS, m)
    assert m % rows == 0 and rows % BAND_BLOCK == 0
    nsub = rows // BAND_BLOCK
    width = B_OUT

    nres = max(1, min(dil, DIL_ROWS // rows))
    assert dil % nres == 0

    def cur(which):
        return pl.BlockSpec((1, nres, rows, width), lambda b, r, i: (b, r, i, which))

    def prev(which):
        return pl.BlockSpec((1, nres, BAND_BLOCK, width),
                            lambda b, r, i: (b, r, jnp.maximum(i * nsub - 1, 0), which))

    return pl.pallas_call(
        functools.partial(_dil_kernel, nres=nres, nsub=nsub, steps=steps),
        grid=(batch, dil // nres, m // rows),
        in_specs=[cur(0), cur(1), prev(1), cur(2), prev(2),
                  pl.BlockSpec((1, B_HEADS_PER_GROUP, BAND_BLOCK, 2 * BAND_BLOCK),
                               lambda b, r, i: (g, 0, 0, 0))],
        out_specs=pl.BlockSpec((1, nres, rows, width + LANES), lambda b, r, i: (b, r, i, 0)),
        out_shape=jax.ShapeDtypeStruct((batch, dil, m, width + LANES), F32),
        compiler_params=_params("arbitrary", "arbitrary", "arbitrary"),
        name=f"dilated_attention_g{g}",
    )(qkv, qkv, qkv, qkv, qkv, bias_b)


def _to_token_order(blk, pinv):
    x = blk.reshape(pinv.shape[0], blk.shape[-1])
    hi = x.astype(BF16)
    r1 = x - hi.astype(F32)
    mid = r1.astype(BF16)
    lo = (r1 - mid.astype(F32)).astype(BF16)
    return (_dot(pinv, hi) + _dot(pinv, mid)) + _dot(pinv, lo)


def _dil_merge_kernel(t0_ref, t1_ref, t2_ref, p1_ref, p2_ref, y_ref):
    groups = [t0_ref[...], _to_token_order(t1_ref[0], p1_ref[...]), _to_token_order(t2_ref[0], p2_ref[...])]
    tm = y_ref.shape[0]
    dh = B_HEAD_DIM
    for j in range(B_HEADS_PER_GROUP):
        lse = [jnp.broadcast_to(t[:, B_OUT + j * LSE_LANES:B_OUT + j * LSE_LANES + 1], (tm, dh))
               for t in groups]
        mx = jnp.maximum(jnp.maximum(lse[0], lse[1]), lse[2])
        e = [jnp.exp(a - mx) for a in lse]
        den = e[0] + e[1] + e[2]
        y = sum((e[g] / den) * groups[g][:, j * dh:(j + 1) * dh] for g in range(3))
        y_ref[:, j * dh:(j + 1) * dh] = y.astype(y_ref.dtype)


def _dilated_merge(packed, batch, seq, tm=256):
    n = batch * seq
    w = packed[0].shape[-1]
    per_seq = seq // tm
    dils = [d for _, d in B_GROUPS]
    assert dils[0] == 1 and len(dils) == 3

    def stream(dil):
        return pl.BlockSpec((1, dil, tm // dil, w), lambda i: (i // per_seq, 0, i % per_seq, 0))

    pspec = pl.BlockSpec((tm, tm), lambda i: (0, 0))
    return pl.pallas_call(
        _dil_merge_kernel,
        grid=(n // tm,),
        in_specs=[pl.BlockSpec((tm, w), lambda i: (i, 0)), stream(dils[1]), stream(dils[2]), pspec, pspec],
        out_specs=pl.BlockSpec((tm, B_OUT), lambda i: (i, 0)),
        out_shape=jax.ShapeDtypeStruct((n, B_OUT), BF16),
        compiler_params=_params("arbitrary"),
        name="dilated_merge",
    )(packed[0].reshape(n, w), packed[1], packed[2],
      _deinterleave_matrix(tm, dils[1]).T, _deinterleave_matrix(tm, dils[2]).T)


def _merge_kernel(h_ref, ya_ref, yb_ref, wga_ref, wgb_ref, wa_ref, wb_ref, o_ref):
    h = h_ref[...]
    ga = jax.nn.sigmoid(_dot(h, wga_ref[...]))
    gb = jax.nn.sigmoid(_dot(h, wgb_ref[...]))
    merged = ga * _dot(ya_ref[...], wa_ref[...]) + gb * _dot(yb_ref[...], wb_ref[...])
    o_ref[...] = merged.astype(o_ref.dtype)


def _gated_merge(h, ya, yb, w_tail, gate_col, wa, wb, tm=1024, tn=512):
    n, d = h.shape
    assert gate_col % tn == 0 and d % tn == 0
    ga0, gb0 = gate_col // tn, (gate_col + d) // tn

    def rows(k):
        return pl.BlockSpec((tm, k), lambda i, j: (i, 0))

    def cols(k):
        return pl.BlockSpec((k, tn), lambda i, j: (0, j))

    return pl.pallas_call(
        _merge_kernel,
        grid=(n // tm, d // tn),
        in_specs=[rows(d), rows(A_OUT), rows(B_OUT),
                  pl.BlockSpec((d, tn), lambda i, j: (0, ga0 + j)),
                  pl.BlockSpec((d, tn), lambda i, j: (0, gb0 + j)),
                  cols(A_OUT), cols(B_OUT)],
        out_specs=pl.BlockSpec((tm, tn), lambda i, j: (i, j)),
        out_shape=jax.ShapeDtypeStruct((n, d), BF16),
        compiler_params=_params("arbitrary", "arbitrary"),
        name="gated_merge",
    )(h, ya, yb, w_tail, w_tail, wa, wb)


def _res_norm_kernel(a_ref, w_ref, x_ref, gate_ref, g_ref, sc_ref, sh_ref, xo_ref, h_ref):
    xn = x_ref[...] + gate_ref[0] * _dot(a_ref[...], w_ref[...])
    xo_ref[...] = xn
    h_ref[...] = (_rms(xn) * g_ref[...] * (1.0 + sc_ref[0]) + sh_ref[0]).astype(h_ref.dtype)


def _res_final_kernel(a_ref, w_ref, x_ref, gate_ref, g_ref, o_ref):
    xn = x_ref[...] + gate_ref[0] * _dot(a_ref[...], w_ref[...])
    o_ref[...] = _rms(xn) * g_ref[...]


def _matmul_residual_norm(a, w, x, gate, g, scale, shift, seq, tm, name):
    n, k = a.shape
    d = w.shape[1]
    per = seq // tm
    row = pl.BlockSpec((tm, d), lambda i: (i, 0))
    vec = pl.BlockSpec((1, 1, d), lambda i: (i // per, 0, 0))
    in_specs = [
        pl.BlockSpec((tm, k), lambda i: (i, 0)),
        pl.BlockSpec((k, d), lambda i: (0, 0), pipeline_mode=pl.Buffered(1)),
        row, vec,
        pl.BlockSpec((1, d), lambda i: (0, 0)),
    ]
    args = [a, w, x, gate, g.reshape(1, d)]
    if scale is None:
        return pl.pallas_call(
            _res_final_kernel, grid=(n // tm,), in_specs=in_specs, out_specs=row,
            out_shape=jax.ShapeDtypeStruct((n, d), F32),
            compiler_params=_params("arbitrary"), name=name,
        )(*args)
    return pl.pallas_call(
        _res_norm_kernel, grid=(n // tm,), in_specs=in_specs + [vec, vec], out_specs=[row, row],
        out_shape=[jax.ShapeDtypeStruct((n, d), F32), jax.ShapeDtypeStruct((n, d), BF16)],
        compiler_params=_params("arbitrary"), name=name,
    )(*args, scale, shift)


def _ffn_up_kernel(h_ref, wg_ref, wu_ref, o_ref, wg_scr, wu_scr):
    @pl.when(pl.program_id(1) == 0)
    def _():
        wg_scr[...] = wg_ref[...].astype(BF16)
        wu_scr[...] = wu_ref[...].astype(BF16)

    h = h_ref[...]
    a = _dot(h, wg_scr[...])
    o_ref[...] = (a * jax.nn.sigmoid(a) * _dot(h, wu_scr[...])).astype(o_ref.dtype)


def _ffn_up(h, wg, wu, layer, tm=1024, tn=512):
    n, d = h.shape
    f = wg.shape[2]
    wspec = pl.BlockSpec((None, d, tn), lambda j, i: (layer, 0, j))
    return pl.pallas_call(
        _ffn_up_kernel,
        grid=(f // tn, n // tm),
        in_specs=[pl.BlockSpec((tm, d), lambda j, i: (i, 0)), wspec, wspec],
        out_specs=pl.BlockSpec((tm, tn), lambda j, i: (i, j)),
        out_shape=jax.ShapeDtypeStruct((n, f), BF16),
        scratch_shapes=[pltpu.VMEM((d, tn), BF16), pltpu.VMEM((d, tn), BF16)],
        compiler_params=_params("arbitrary", "arbitrary"),
        name="ffn_up",
    )(h, wg, wu)


def _rope_tables(seq):
    half = IDX_ROPE_DIM // 2
    freqs = ROPE_THETA ** (-jnp.arange(half, dtype=F32) / half)
    ang = jnp.arange(seq).astype(F32)[:, None] * freqs[None, :]
    cos, sin = jnp.cos(ang), jnp.sin(ang)
    rest = IDX_HEAD_DIM - IDX_ROPE_DIM
    one = jnp.ones((seq, rest), F32)
    zr = jnp.zeros((seq, rest), F32)
    zh = jnp.zeros((seq, half), F32)
    cos_t = jnp.concatenate([cos, cos, one], axis=1)
    sin_lo = jnp.concatenate([-sin, zh, zr], axis=1)
    sin_hi = jnp.concatenate([zh, sin, zr], axis=1)
    rep = LANES // IDX_HEAD_DIM
    return tuple(jnp.tile(t, (1, rep)) for t in (cos_t, sin_lo, sin_hi))


def _layer(layer, x, h, h_mod, next_norm, tables, bias_a, bias_b, batch, seq, w_in, kv_norm_g, idx_ln_g,
           idx_ln_b, w_uk, w_uv, w_a_up, w_b_up, w_out, norm2_g, w_ff_gate, w_ff_up, w_ff_down):
    _, _, gate1, shift2, scale2, gate2 = h_mod
    offs = [0]
    for s in IN_SIZES:
        offs.append(offs[-1] + s)
    w_qa, w_kv, w_qi, w_ki, w_wi = [w_in[:, offs[k]:offs[k + 1]] for k in range(5)]
    w_tail = w_in[:, offs[5]:].astype(BF16)
    w_qkvb = w_tail[:, :B_QKV_COLS]
    d = w_in.shape[0]
    rep = LANES // IDX_HEAD_DIM
    w_kw = jnp.concatenate(
        [w_ki] * rep + [w_wi, jnp.zeros((d, LANES - IDX_HEADS), w_in.dtype)], axis=1).astype(BF16)
    ln_g2 = jnp.tile(idx_ln_g, rep).reshape(1, LANES)
    ln_b2 = jnp.tile(idx_ln_b, rep).reshape(1, LANES)

    ql = _proj_qlat(h, w_qa.astype(BF16), w_uk.astype(BF16))
    ckv, ckv_t = _proj_ckv(h, w_kv.astype(BF16), kv_norm_g, DSA_TILE)
    qi = _proj_qi(h, w_qi.astype(BF16), tables, seq)
    k2, wi_t = _proj_kw(h, w_kw, ln_g2, ln_b2, tables, seq)
    wuv_t = jnp.swapaxes(w_uv, 1, 2).astype(BF16)
    ya = _dsa_attention(qi, wi_t, k2, ql, ckv, ckv_t, bias_a, wuv_t, batch, seq, DSA_TILE)

    ng = len(B_GROUPS)
    packed = []
    for g, (_, dil) in enumerate(B_GROUPS):
        w_g = jnp.concatenate([w_qkvb[:, (s * ng + g) * B_OUT:(s * ng + g + 1) * B_OUT] for s in range(3)],
                              axis=1)
        packed.append(_dilated_group(_proj_dilated(h, w_g, batch, seq, dil), bias_b, g, batch, seq))
    yb = _dilated_merge(packed, batch, seq)

    merged = _gated_merge(h, ya, yb, w_tail, B_QKV_COLS, w_a_up.astype(BF16), w_b_up.astype(BF16))
    x, h2 = _matmul_residual_norm(merged, w_out.astype(BF16), x, gate1, norm2_g, scale2, shift2, seq,
                                  ROWS_OUT_PROJ, "out_proj_residual")
    act = _ffn_up(h2, w_ff_gate, w_ff_up, layer)
    return _matmul_residual_norm(act, w_ff_down.astype(BF16), x, gate2, *next_norm, seq, ROWS_FFN_DOWN,
                                 "ffn_down_residual")


def kernel(x, c, rel_bias, w_ada, b_ada, norm1_g, w_in, kv_norm_g, idx_ln_g, idx_ln_b, w_uk, w_uv,
           w_a_up, w_b_up, w_out, norm2_g, w_ff_gate, w_ff_up, w_ff_down, final_g):
    batch, seq, d = x.shape
    depth = w_ada.shape[0]
    assert d == D_MODEL and seq % (B_GROUPS[-1][1] * BAND_BLOCK) == 0
    n = batch * seq
    rows = -(-batch // SUBLANES) * SUBLANES
    mod = _modulation(jnp.pad(c, ((0, rows - batch), (0, 0))), w_ada, b_ada)
    mod = mod[:, :batch].reshape(depth, batch, 6, 1, d)
    bias_a = _bias_a_table(rel_bias, min(seq // BAND_BLOCK, FAR_DELTA + 1))
    bias_b = _bias_b_table(rel_bias)
    tables = _rope_tables(seq)
    xf = x.reshape(n, d)
    h_mods = [[mod[l, :, k] for k in range(6)] for l in range(depth)]
    h = _norm_mod(xf, norm1_g[0], h_mods[0][1], h_mods[0][0], seq)
    for l in range(depth):
        last = l + 1 == depth
        next_norm = (final_g, None, None) if last else (norm1_g[l + 1], h_mods[l + 1][1], h_mods[l + 1][0])
        out = _layer(l, xf, h, h_mods[l], next_norm, tables, bias_a, bias_b, batch, seq, w_in[l],
                     kv_norm_g[l], idx_ln_g[l], idx_ln_b[l], w_uk[l], w_uv[l], w_a_up[l], w_b_up[l], w_out[l],
                     norm2_g[l], w_ff_gate, w_ff_up, w_ff_down[l])
        if last:
            return out.reshape(batch, seq, d)
        xf, h = out
```

```python
import functools
import math

import jax
import jax.numpy as jnp
from jax import lax
from jax.experimental import pallas as pl
from jax.experimental.pallas import tpu as pltpu

D_MODEL = 2048
A_HEADS = 8
A_HEAD_DIM = 128
A_KV_RANK = 256
IDX_HEADS = 16
IDX_HEAD_DIM = 64
IDX_ROPE_DIM = 32
ROPE_THETA = 10000.0
DSA_TOPK = 256
B_GROUPS = ((128, 1), (512, 4), (2048, 16))
B_HEADS_PER_GROUP = 4
B_HEAD_DIM = 128
B_HEADS = B_HEADS_PER_GROUP * len(B_GROUPS)
BAND_BLOCK = 128
REL_BUCKETS = 32
REL_MAX_DISTANCE = 2048
N_BIAS_HEADS = A_HEADS + B_HEADS
D_FF = -(-8 * D_MODEL // (3 * 256)) * 256
NORM_EPS = 1e-6
NEG_INF = -1e30

A_Q_COLS = A_HEADS * A_HEAD_DIM
IDX_Q_COLS = IDX_HEADS * IDX_HEAD_DIM
B_QKV_COLS = 3 * B_HEADS * B_HEAD_DIM
GATE_COLS = 2 * D_MODEL
IN_SIZES = (A_Q_COLS, A_KV_RANK, IDX_Q_COLS, IDX_HEAD_DIM, IDX_HEADS, B_QKV_COLS, GATE_COLS)
A_OUT = A_HEADS * A_HEAD_DIM
B_OUT = B_HEADS_PER_GROUP * B_HEAD_DIM

LANES = 128
SUBLANES = 8
VMEM_LIMIT_BYTES = 56 * 1024 * 1024

BF16 = jnp.bfloat16
F32 = jnp.float32
LOG2E = math.log2(math.e)
HALF_BITS = 16
HALF_MASK = 2 ** HALF_BITS - 1
HALF_BIAS = 2 ** (HALF_BITS - 1)
PACKED_ROWS = 2 * SUBLANES
KV_ROWS = A_KV_RANK + PACKED_ROWS

FAR_DELTA = -(-(REL_MAX_DISTANCE + BAND_BLOCK - 1) // BAND_BLOCK)
DSA_TILE = 512
DSA_STRIP = 256
DIL_ROWS = 8 * BAND_BLOCK
ROWS_STREAM = 1024
ROWS_OUT_PROJ = 512
ROWS_FFN_DOWN = 256


def _dot(a, b):
    return jnp.dot(a, b, preferred_element_type=F32)


def _dot_nt(a, b):
    return lax.dot_general(a, b, (((1,), (1,)), ((), ())), preferred_element_type=F32)


def _params(*sem):
    return pltpu.CompilerParams(dimension_semantics=sem, vmem_limit_bytes=VMEM_LIMIT_BYTES)


def _mod_kernel(c_ref, w_ref, b_ref, o_ref):
    c = c_ref[...]
    a = (c * jax.nn.sigmoid(c)).astype(BF16)
    o_ref[0] = _dot(a, w_ref[0].astype(BF16)) + b_ref[0]


def _modulation(c_pad, w_ada, b_ada, tn=2048):
    depth, d, n6 = w_ada.shape
    rows = c_pad.shape[0]
    return pl.pallas_call(
        _mod_kernel,
        grid=(depth, n6 // tn),
        in_specs=[
            pl.BlockSpec((rows, d), lambda l, j: (0, 0)),
            pl.BlockSpec((1, d, tn), lambda l, j: (l, 0, j)),
            pl.BlockSpec((1, 1, tn), lambda l, j: (l, 0, j)),
        ],
        out_specs=pl.BlockSpec((1, rows, tn), lambda l, j: (l, 0, j)),
        out_shape=jax.ShapeDtypeStruct((depth, rows, n6), F32),
        compiler_params=_params("arbitrary", "arbitrary"),
        name="adaln_modulation",
    )(c_pad, w_ada, b_ada.reshape(depth, 1, n6))


def _bucket(dist):
    n = jnp.maximum(dist, 0)
    exact = REL_BUCKETS // 2
    nf = jnp.maximum(n, 1).astype(F32)
    large = exact + (jnp.log(nf / exact) / math.log(REL_MAX_DISTANCE / exact)
                     * (REL_BUCKETS - exact)).astype(jnp.int32)
    return jnp.where(n < exact, n, jnp.minimum(large, REL_BUCKETS - 1))


def _lookup(rb_ref, bucket, head):
    t = jnp.zeros(bucket.shape, F32)
    for k in range(REL_BUCKETS):
        t = jnp.where(bucket == k, rb_ref[k, head], t)
    return t


def _bias_a_kernel(rb_ref, o_ref):
    delta = pl.program_id(0)
    row = lax.broadcasted_iota(jnp.int32, (BAND_BLOCK, BAND_BLOCK), 0)
    col = lax.broadcasted_iota(jnp.int32, (BAND_BLOCK, BAND_BLOCK), 1)
    bucket = _bucket(delta * BAND_BLOCK + col - row)
    for h in range(A_HEADS):
        o_ref[h, 0] = _lookup(rb_ref, bucket, h) * LOG2E


def _bias_a_table(rel_bias, n_delta):
    return pl.pallas_call(
        _bias_a_kernel,
        grid=(n_delta,),
        in_specs=[pl.BlockSpec(memory_space=pltpu.SMEM)],
        out_specs=pl.BlockSpec((A_HEADS, 1, BAND_BLOCK, BAND_BLOCK), lambda d: (0, d, 0, 0)),
        out_shape=jax.ShapeDtypeStruct((A_HEADS, n_delta, BAND_BLOCK, BAND_BLOCK), F32),
        compiler_params=_params("arbitrary"),
        name="rel_bias_table_a",
    )(rel_bias)


def _bias_b_kernel(rb_ref, o_ref):
    g = pl.program_id(0)
    dil = jnp.where(g == 0, B_GROUPS[0][1], jnp.where(g == 1, B_GROUPS[1][1], B_GROUPS[2][1]))
    row = lax.broadcasted_iota(jnp.int32, (BAND_BLOCK, 2 * BAND_BLOCK), 0)
    u = lax.broadcasted_iota(jnp.int32, (BAND_BLOCK, 2 * BAND_BLOCK), 1)
    bucket = _bucket((row + BAND_BLOCK - u) * dil)
    for j in range(B_HEADS_PER_GROUP):
        o_ref[0, j] = _lookup(rb_ref, bucket, A_HEADS + g * B_HEADS_PER_GROUP + j)


def _bias_b_table(rel_bias):
    ng = len(B_GROUPS)
    return pl.pallas_call(
        _bias_b_kernel,
        grid=(ng,),
        in_specs=[pl.BlockSpec(memory_space=pltpu.SMEM)],
        out_specs=pl.BlockSpec((1, B_HEADS_PER_GROUP, BAND_BLOCK, 2 * BAND_BLOCK),
                               lambda g: (g, 0, 0, 0)),
        out_shape=jax.ShapeDtypeStruct((ng, B_HEADS_PER_GROUP, BAND_BLOCK, 2 * BAND_BLOCK), F32),
        compiler_params=_params("arbitrary"),
        name="rel_bias_table_b",
    )(rel_bias)


def _rms(x):
    return x * lax.rsqrt(jnp.mean(x * x, axis=-1, keepdims=True) + NORM_EPS)


def _norm_mod_kernel(x_ref, g_ref, sc_ref, sh_ref, o_ref):
    y = _rms(x_ref[...]) * g_ref[...]
    o_ref[...] = (y * (1.0 + sc_ref[0]) + sh_ref[0]).astype(o_ref.dtype)


def _norm_mod(x, g, scale, shift, seq, tm=512):
    n, d = x.shape
    per = seq // tm
    return pl.pallas_call(
        _norm_mod_kernel,
        grid=(n // tm,),
        in_specs=[
            pl.BlockSpec((tm, d), lambda i: (i, 0)),
            pl.BlockSpec((1, d), lambda i: (0, 0)),
            pl.BlockSpec((1, 1, d), lambda i: (i // per, 0, 0)),
            pl.BlockSpec((1, 1, d), lambda i: (i // per, 0, 0)),
        ],
        out_specs=pl.BlockSpec((tm, d), lambda i: (i, 0)),
        out_shape=jax.ShapeDtypeStruct((n, d), BF16),
        compiler_params=_params("arbitrary"),
        name="norm_modulate",
    )(x, g.reshape(1, d), scale, shift)


def _qlat_kernel(h_ref, w_ref, wuk_ref, o_ref):
    z = _dot(h_ref[...], w_ref[...])
    for hd in range(A_HEADS):
        zh = z[:, hd * A_HEAD_DIM:(hd + 1) * A_HEAD_DIM].astype(BF16)
        o_ref[hd] = (_dot(zh, wuk_ref[hd]) * (A_HEAD_DIM ** -0.5 * LOG2E)).astype(o_ref.dtype)


def _proj_qlat(h, w, wuk, tm=512):
    n, d = h.shape
    return pl.pallas_call(
        _qlat_kernel,
        grid=(n // tm,),
        in_specs=[
            pl.BlockSpec((tm, d), lambda i: (i, 0)),
            pl.BlockSpec((d, A_Q_COLS), lambda i: (0, 0)),
            pl.BlockSpec((A_HEADS, A_HEAD_DIM, A_KV_RANK), lambda i: (0, 0, 0)),
        ],
        out_specs=pl.BlockSpec((A_HEADS, tm, A_KV_RANK), lambda i: (0, i, 0)),
        out_shape=jax.ShapeDtypeStruct((A_HEADS, n, A_KV_RANK), BF16),
        compiler_params=_params("arbitrary"),
        name="proj_q_latent",
    )(h, w, wuk)


def _ckv_kernel(h_ref, w_ref, g_ref, o_ref, ot_ref, *, tk):
    z = _dot(h_ref[...], w_ref[...])
    ckv = _rms(z) * g_ref[...]
    o_ref[...] = ckv.astype(o_ref.dtype)
    ones = jnp.ones((KV_ROWS - A_KV_RANK, tk), F32)
    for s in range(ot_ref.shape[0]):
        ot_ref[s] = jnp.concatenate([ckv[s * tk:(s + 1) * tk, :].T, ones], axis=0).astype(ot_ref.dtype)


def _proj_ckv(h, w, g, tk, tm=1024):
    n, d = h.shape
    return pl.pallas_call(
        functools.partial(_ckv_kernel, tk=tk),
        grid=(n // tm,),
        in_specs=[
            pl.BlockSpec((tm, d), lambda i: (i, 0)),
            pl.BlockSpec((d, A_KV_RANK), lambda i: (0, 0)),
            pl.BlockSpec((1, A_KV_RANK), lambda i: (0, 0)),
        ],
        out_specs=[pl.BlockSpec((tm, A_KV_RANK), lambda i: (i, 0)),
                   pl.BlockSpec((tm // tk, KV_ROWS, tk), lambda i: (i, 0, 0))],
        out_shape=[jax.ShapeDtypeStruct((n, A_KV_RANK), BF16),
                   jax.ShapeDtypeStruct((n // tk, KV_ROWS, tk), BF16)],
        compiler_params=_params("arbitrary"),
        name="proj_latent_kv",
    )(h, w, g.reshape(1, A_KV_RANK))


def _rope(z, cos_t, sin_lo, sin_hi):
    half = IDX_ROPE_DIM // 2
    return (z * cos_t + pltpu.roll(z, half, 1) * sin_hi
            + pltpu.roll(z, LANES - half, 1) * sin_lo)


def _qi_kernel(h_ref, w_ref, cos_ref, slo_ref, shi_ref, o_ref):
    z = _dot(h_ref[...], w_ref[...])
    cos_t, slo, shi = cos_ref[...], slo_ref[...], shi_ref[...]
    for s in range(IDX_Q_COLS // LANES):
        zs = z[:, s * LANES:(s + 1) * LANES]
        o_ref[:, s * LANES:(s + 1) * LANES] = _rope(zs, cos_t, slo, shi).astype(o_ref.dtype)


def _proj_qi(h, w, tables, seq, tm=512):
    n, d = h.shape
    per = seq // tm
    tspec = pl.BlockSpec((tm, LANES), lambda i: (i % per, 0))
    return pl.pallas_call(
        _qi_kernel,
        grid=(n // tm,),
        in_specs=[
            pl.BlockSpec((tm, d), lambda i: (i, 0)),
            pl.BlockSpec((d, IDX_Q_COLS), lambda i: (0, 0)),
            tspec, tspec, tspec,
        ],
        out_specs=pl.BlockSpec((tm, IDX_Q_COLS), lambda i: (i, 0)),
        out_shape=jax.ShapeDtypeStruct((n, IDX_Q_COLS), BF16),
        compiler_params=_params("arbitrary"),
        name="proj_index_q",
    )(h, w, *tables)


def _kw_kernel(h_ref, w_ref, g_ref, b_ref, cos_ref, slo_ref, shi_ref, k_ref, wi_ref):
    z = _dot(h_ref[...], w_ref[...])
    zk = z[:, :LANES]
    mu = jnp.mean(zk, axis=-1, keepdims=True)
    var = jnp.mean(jnp.square(zk - mu), axis=-1, keepdims=True)
    kn = (zk - mu) * lax.rsqrt(var + NORM_EPS) * g_ref[...] + b_ref[...]
    k_ref[...] = _rope(kn, cos_ref[...], slo_ref[...], shi_ref[...]).astype(k_ref.dtype)
    wi_ref[...] = (z[:, LANES:] * (IDX_HEADS ** -0.5 * IDX_HEAD_DIM ** -0.5)).T


def _proj_kw(h, w, g2, b2, tables, seq, tm=1024):
    n, d = h.shape
    per = seq // tm
    tspec = pl.BlockSpec((tm, LANES), lambda i: (i % per, 0))
    vspec = pl.BlockSpec((1, LANES), lambda i: (0, 0))
    ospec = pl.BlockSpec((tm, LANES), lambda i: (i, 0))
    return pl.pallas_call(
        _kw_kernel,
        grid=(n // tm,),
        in_specs=[
            pl.BlockSpec((tm, d), lambda i: (i, 0)),
            pl.BlockSpec((d, 2 * LANES), lambda i: (0, 0)),
            vspec, vspec, tspec, tspec, tspec,
        ],
        out_specs=[ospec, pl.BlockSpec((LANES, tm), lambda i: (0, i))],
        out_shape=[jax.ShapeDtypeStruct((n, LANES), BF16), jax.ShapeDtypeStruct((LANES, n), F32)],
        compiler_params=_params("arbitrary"),
        name="proj_index_kw",
    )(h, w, g2, b2, *tables)


def _mm_kernel(a_ref, w_ref, o_ref):
    o_ref[...] = _dot(a_ref[...], w_ref[...]).astype(o_ref.dtype)


def _matmul(a, w, out_dtype, tm, tn, name):
    n, k = a.shape
    cols = w.shape[1]
    return pl.pallas_call(
        _mm_kernel,
        grid=(n // tm, cols // tn),
        in_specs=[pl.BlockSpec((tm, k), lambda i, j: (i, 0)), pl.BlockSpec((k, tn), lambda i, j: (0, j))],
        out_specs=pl.BlockSpec((tm, tn), lambda i, j: (i, j)),
        out_shape=jax.ShapeDtypeStruct((n, cols), out_dtype),
        compiler_params=_params("arbitrary", "arbitrary"),
        name=name,
    )(a, w)


def _dsa_kernel(qi_ref, wi_ref, k2_ref, ql_ref, ckv_ref, ckvt_ref, ba_ref, wuvt_ref, o_ref,
                key_scr, hi_scr, lo_scr, qm_scr, acc_scr, m_scr, madd_scr, tie_scr,
                *, tq, tk, topk, n_delta):
    i = pl.program_id(1)
    q0 = i * tq
    nck = (q0 + tq + tk - 1) // tk
    nsub_k = tk // BAND_BLOCK
    qw = min(tq, DSA_STRIP)
    key_row = lax.broadcasted_iota(jnp.int32, (tk, tq), 0)
    q_pos = q0 + lax.broadcasted_iota(jnp.int32, (tk, tq), 1)
    lane = lax.broadcasted_iota(jnp.int32, (tq, LANES), 1)

    for h in range(IDX_HEADS):
        qs = qi_ref[:, (h // 2) * LANES:(h // 2 + 1) * LANES].astype(F32)
        keep = (lane >= IDX_HEAD_DIM) if h % 2 else (lane < IDX_HEAD_DIM)
        qm_scr[h] = jnp.where(keep, qs, 0.0).astype(BF16)

    def idx_body(c, carry):
        k2 = k2_ref[pl.ds(pl.multiple_of(c * tk, tk), tk), :]
        score = jnp.zeros((tk, tq), F32)
        for h in range(IDX_HEADS):
            logits = _dot_nt(k2, qm_scr[h])
            score = score + wi_ref[h:h + 1, :] * jnp.maximum(logits, 0.0)
        score = jnp.where(c * tk + key_row <= q_pos, score, -jnp.inf)
        bits = jnp.where(score == 0.0, 0, pltpu.bitcast(score, jnp.int32))
        key = bits ^ ((bits >> 31) & 0x7FFFFFFF)
        key_scr[c] = key
        hi_scr[c] = (key >> HALF_BITS).astype(jnp.int16)
        lo_scr[c] = ((key & HALF_MASK) - HALF_BIAS).astype(jnp.int16)
        return carry

    lax.fori_loop(0, nck, idx_body, 0)

    def count16(mask_fn):
        def cnt_body(c, part):
            hit = mask_fn(c).astype(jnp.int16)
            for g in range(tk // PACKED_ROWS):
                part = part + hit[g * PACKED_ROWS:(g + 1) * PACKED_ROWS]
            return part

        part = lax.fori_loop(0, nck, cnt_body, jnp.zeros((PACKED_ROWS, tq), jnp.int16))
        return jnp.sum(part.astype(F32), axis=0, keepdims=True)

    def select16(src_scr, need, n_all):
        def bit_body(b, carry):
            tx, n_ge = carry
            cand_x = tx | jnp.left_shift(jnp.int32(1), HALF_BITS - 1 - b)
            cand = (cand_x - HALF_BIAS).astype(jnp.int16)
            cnt = count16(lambda c: src_scr[c] >= cand)
            take = cnt >= need
            return jnp.where(take, cand_x, tx), jnp.where(take, cnt, n_ge)

        tx, n_ge = lax.fori_loop(0, HALF_BITS, bit_body, (jnp.zeros((1, tq), jnp.int32), n_all))
        return tx - HALF_BIAS, n_ge

    n_keys = jnp.full((1, tq), 1.0, F32) * (nck * tk).astype(F32)
    t_hi, n_hi_ge = select16(hi_scr, topk, n_keys)
    t_hi16 = t_hi.astype(jnp.int16)

    def tie_body(c, carry):
        lo_scr[c] = jnp.where(hi_scr[c] == t_hi16, lo_scr[c], jnp.int16(-HALF_BIAS))
        return carry

    lax.fori_loop(0, nck, tie_body, 0)
    above = count16(lambda c: hi_scr[c] > t_hi16)
    t_lo, n_lo_ge = select16(lo_scr, topk - above, n_hi_ge - above)
    thr = (t_hi << HALF_BITS) | (t_lo + HALF_BIAS)
    n_ge = above + n_lo_ge

    def count32(mask_fn):
        def cnt_body(c, part):
            hit = mask_fn(key_scr[c]).astype(jnp.int32)
            return part + jnp.sum(hit.reshape(tk // SUBLANES, SUBLANES, tq), axis=0)

        part = lax.fori_loop(0, nck, cnt_body, jnp.zeros((SUBLANES, tq), jnp.int32))
        return jnp.sum(part.astype(F32), axis=0, keepdims=True)

    has_tie = jnp.max(n_ge) > topk
    tie_scr[0:1, :] = jnp.zeros((1, tq), F32)

    @pl.when(has_tie)
    def _():
        tie_scr[1:2, :] = topk - count32(lambda key: key > thr)

    m_scr[...] = jnp.full(m_scr.shape, NEG_INF, F32)
    acc_scr[...] = jnp.zeros(acc_scr.shape, F32)

    def att_body(c, carry):
        kv = ckv_ref[pl.ds(pl.multiple_of(c * tk, tk), tk), :]
        kvt = ckvt_ref[c]
        causal = c * tk + key_row <= q_pos

        @pl.when(jnp.logical_not(has_tie))
        def _():
            madd_scr[...] = jnp.where((key_scr[c] >= thr) & causal, 0.0, NEG_INF)

        @pl.when(has_tie)
        def _():
            key = key_scr[c]
            tied = key == thr
            below = (lax.broadcasted_iota(jnp.int32, (tk, tk), 0)
                     >= lax.broadcasted_iota(jnp.int32, (tk, tk), 1))
            tied_f = tied.astype(F32)
            rank = _dot(below.astype(F32).astype(BF16), tied_f.astype(BF16)) + tie_scr[0:1, :]
            keep = (key > thr) | (tied & (rank <= tie_scr[1:2, :]))
            madd_scr[...] = jnp.where(keep & causal, 0.0, NEG_INF)
            tie_scr[0:1, :] = tie_scr[0:1, :] + jnp.sum(tied_f, axis=0, keepdims=True)

        madd = madd_scr[...]
        base_delta = (q0 - c * tk) // BAND_BLOCK
        items = [(h, w) for h in range(A_HEADS) for w in range(tq // qw)]

        def scores(item):
            h, w = item
            return _dot_nt(kv, ql_ref[h, w * qw:(w + 1) * qw, :])

        pending = [scores(items[0]), scores(items[1])]
        for n_item, (h, w) in enumerate(items):
            qs = slice(w * qw, (w + 1) * qw)
            s_raw = pending.pop(0)
            if n_item + 2 < len(items):
                pending.append(scores(items[n_item + 2]))
            s_next = pending[0] if pending else None
            rows = []
            for j in range(nsub_k):
                tiles = []
                for a in range(w * qw // BAND_BLOCK, (w + 1) * qw // BAND_BLOCK):
                    delta = jnp.clip(base_delta + (a - j), 0, n_delta - 1)
                    tiles.append(ba_ref[h, delta])
                rows.append(tiles[0] if len(tiles) == 1 else jnp.concatenate(tiles, axis=1))
            bias = rows[0] if nsub_k == 1 else jnp.concatenate(rows, axis=0)
            s = s_raw + bias + madd[:, qs]
            m_prev = m_scr[h:h + 1, qs]
            m_new = jnp.maximum(m_prev, jnp.max(s, axis=0, keepdims=True))
            alpha = jnp.exp2(m_prev - m_new)
            p = jnp.exp2(s - m_new)
            m_scr[h:h + 1, qs] = m_new
            if s_next is not None:
                nxt = pltpu.bitcast(s_next[:SUBLANES], jnp.uint32)
                zero = lax.shift_right_logical(lax.shift_right_logical(nxt, jnp.uint32(16)), jnp.uint32(16))
                p = jnp.concatenate([p[:SUBLANES] + zero.astype(F32), p[SUBLANES:]], axis=0)
            acc_scr[h, :, qs] = acc_scr[h, :, qs] * alpha + _dot(kvt, p.astype(BF16))
        return carry

    lax.fori_loop(0, nck, att_body, 0)

    for h in range(A_HEADS):
        o_lat_t = (acc_scr[h, :A_KV_RANK, :] / acc_scr[h, A_KV_RANK:A_KV_RANK + 1, :]).astype(BF16)
        y_t = _dot(wuvt_ref[h], o_lat_t)
        o_ref[:, h * A_HEAD_DIM:(h + 1) * A_HEAD_DIM] = y_t.T.astype(o_ref.dtype)


def _dsa_attention(qi, wi_t, k2, ql, ckv, ckv_t, bias_a, wuv_t, batch, seq, tq):
    tk = tq
    n = batch * seq
    nq = seq // tq
    nck = seq // tk
    topk = min(DSA_TOPK, seq // 4)
    assert topk <= tk and seq % tq == 0 and tq % BAND_BLOCK == 0
    n_delta = bias_a.shape[1]
    kern = functools.partial(_dsa_kernel, tq=tq, tk=tk, topk=topk, n_delta=n_delta)
    once = pl.Buffered(1)
    return pl.pallas_call(
        kern,
        grid=(batch, nq),
        in_specs=[
            pl.BlockSpec((tq, IDX_Q_COLS), lambda b, i: (b * nq + i, 0)),
            pl.BlockSpec((LANES, tq), lambda b, i: (0, b * nq + i)),
            pl.BlockSpec((seq, LANES), lambda b, i: (b, 0), pipeline_mode=once),
            pl.BlockSpec((A_HEADS, tq, A_KV_RANK), lambda b, i: (0, b * nq + i, 0)),
            pl.BlockSpec((seq, A_KV_RANK), lambda b, i: (b, 0), pipeline_mode=once),
            pl.BlockSpec((nck, KV_ROWS, tk), lambda b, i: (b, 0, 0), pipeline_mode=once),
            pl.BlockSpec(bias_a.shape, lambda b, i: (0, 0, 0, 0), pipeline_mode=once),
            pl.BlockSpec(wuv_t.shape, lambda b, i: (0, 0, 0), pipeline_mode=once),
        ],
        out_specs=pl.BlockSpec((tq, A_OUT), lambda b, i: (b * nq + i, 0)),
        out_shape=jax.ShapeDtypeStruct((n, A_OUT), BF16),
        scratch_shapes=[
            pltpu.VMEM((nck, tk, tq), jnp.int32),
            pltpu.VMEM((nck, tk, tq), jnp.int16),
            pltpu.VMEM((nck, tk, tq), jnp.int16),
            pltpu.VMEM((IDX_HEADS, tq, LANES), BF16),
            pltpu.VMEM((A_HEADS, KV_ROWS, tq), F32),
            pltpu.VMEM((A_HEADS, tq), F32),
            pltpu.VMEM((tk, tq), F32),
            pltpu.VMEM((SUBLANES, tq), F32),
        ],
        compiler_params=_params("arbitrary", "arbitrary"),
        name="dsa_attention",
    )(qi, wi_t, k2, ql, ckv, ckv_t, bias_a, wuv_t)


def _deinterleave_matrix(tm, dil):
    per = tm // dil
    i = jnp.arange(tm)
    src = (i % per) * dil + i // per
    return (src[:, None] == i[None, :]).astype(BF16)


def _mm_stream_kernel(a_ref, w_ref, p_ref, o_ref):
    z = _dot(a_ref[...], w_ref[...]).astype(BF16)
    zs = _dot(p_ref[...], z).astype(BF16)
    o_ref[0] = zs.reshape(o_ref.shape[1:])


def _proj_dilated(h, w, batch, seq, dil, tm=256):
    n, d = h.shape
    c = w.shape[1]
    if dil == 1:
        return _matmul(h, w, BF16, ROWS_STREAM, c // 2, "proj_dilated_qkv_g0").reshape(batch, 1, seq, c)
    per_seq = seq // tm
    return pl.pallas_call(
        _mm_stream_kernel,
        grid=(n // tm,),
        in_specs=[
            pl.BlockSpec((tm, d), lambda i: (i, 0)),
            pl.BlockSpec((d, c), lambda i: (0, 0)),
            pl.BlockSpec((tm, tm), lambda i: (0, 0)),
        ],
        out_specs=pl.BlockSpec((1, dil, tm // dil, c), lambda i: (i // per_seq, 0, i % per_seq, 0)),
        out_shape=jax.ShapeDtypeStruct((batch, dil, seq // dil, c), BF16),
        compiler_params=_params("arbitrary"),
        name=f"proj_dilated_qkv_d{dil}",
    )(h, w, _deinterleave_matrix(tm, dil))


LSE_LANES = LANES // B_HEADS_PER_GROUP


def _dil_kernel(q_ref, kc_ref, kp_ref, vc_ref, vp_ref, bias_ref, o_ref, *, nres, nsub, steps):
    first = pl.program_id(2) == 0
    P = BAND_BLOCK
    dh = B_HEAD_DIM
    row = lax.broadcasted_iota(jnp.int32, (P, 2 * P), 0)
    u = lax.broadcasted_iota(jnp.int32, (P, 2 * P), 1)
    back = row + P - u
    band = (back >= 0) & (back <= steps)
    band_first = band & ((u >= P) | jnp.logical_not(first))
    lane_head = lax.broadcasted_iota(jnp.int32, (P, LANES), 1) // LSE_LANES
    ones_cols = jnp.ones((2 * P, dh), BF16)
    for rr, a in [(rr, a) for rr in range(nres) for a in range(nsub)]:
        valid = band_first if a == 0 else band
        rs = slice(a * P, (a + 1) * P)
        lse_all = jnp.zeros((P, LANES), F32)
        for j in range(B_HEADS_PER_GROUP):
            cs = slice(j * dh, (j + 1) * dh)
            if a == 0:
                k_prev, v_prev = kp_ref[0, rr, :, cs], vp_ref[0, rr, :, cs]
            else:
                ps = slice((a - 1) * P, a * P)
                k_prev, v_prev = kc_ref[0, rr, ps, cs], vc_ref[0, rr, ps, cs]
            k_cat = jnp.concatenate([k_prev, kc_ref[0, rr, rs, cs]], axis=0)
            v_cat = jnp.concatenate([v_prev, vc_ref[0, rr, rs, cs]], axis=0)
            s = _dot_nt(q_ref[0, rr, rs, cs], k_cat) * (dh ** -0.5) + bias_ref[0, j]
            s = jnp.where(valid, s, NEG_INF)
            mx = jnp.max(s, axis=-1, keepdims=True)
            p = jnp.exp(s - mx).astype(BF16)
            pv = _dot(p, jnp.concatenate([v_cat, ones_cols], axis=1))
            den = pv[:, dh:]
            o_ref[0, rr, rs, cs] = pv[:, :dh] / den
            lse_all = jnp.where(lane_head == j, mx + jnp.log(den), lse_all)
        o_ref[0, rr, rs, B_OUT:] = lse_all


def _dilated_group(qkv, bias_b, g, batch, seq):
    window, dil = B_GROUPS[g]
    steps = window // dil
    assert steps <= BAND_BLOCK
    m = seq // dil
    rows = min(DIL_ROWS, m)
    assert m % rows == 0 and rows % BAND_BLOCK == 0
    nsub = rows // BAND_BLOCK
    width = B_OUT

    nres = max(1, min(dil, DIL_ROWS // rows))
    assert dil % nres == 0

    def cur(which):
        return pl.BlockSpec((1, nres, rows, width), lambda b, r, i: (b, r, i, which))

    def prev(which):
        return pl.BlockSpec((1, nres, BAND_BLOCK, width),
                            lambda b, r, i: (b, r, jnp.maximum(i * nsub - 1, 0), which))

    return pl.pallas_call(
        functools.partial(_dil_kernel, nres=nres, nsub=nsub, steps=steps),
        grid=(batch, dil // nres, m // rows),
        in_specs=[cur(0), cur(1), prev(1), cur(2), prev(2),
                  pl.BlockSpec((1, B_HEADS_PER_GROUP, BAND_BLOCK, 2 * BAND_BLOCK),
                               lambda b, r, i: (g, 0, 0, 0))],
        out_specs=pl.BlockSpec((1, nres, rows, width + LANES), lambda b, r, i: (b, r, i, 0)),
        out_shape=jax.ShapeDtypeStruct((batch, dil, m, width + LANES), F32),
        compiler_params=_params("arbitrary", "arbitrary", "arbitrary"),
        name=f"dilated_attention_g{g}",
    )(qkv, qkv, qkv, qkv, qkv, bias_b)


def _to_token_order(blk, pinv):
    x = blk.reshape(pinv.shape[0], blk.shape[-1])
    hi = x.astype(BF16)
    r1 = x - hi.astype(F32)
    mid = r1.astype(BF16)
    lo = (r1 - mid.astype(F32)).astype(BF16)
    return (_dot(pinv, hi) + _dot(pinv, mid)) + _dot(pinv, lo)


def _dil_merge_kernel(t0_ref, t1_ref, t2_ref, p1_ref, p2_ref, y_ref):
    groups = [t0_ref[...], _to_token_order(t1_ref[0], p1_ref[...]), _to_token_order(t2_ref[0], p2_ref[...])]
    tm = y_ref.shape[0]
    dh = B_HEAD_DIM
    for j in range(B_HEADS_PER_GROUP):
        lse = [jnp.broadcast_to(t[:, B_OUT + j * LSE_LANES:B_OUT + j * LSE_LANES + 1], (tm, dh))
               for t in groups]
        mx = jnp.maximum(jnp.maximum(lse[0], lse[1]), lse[2])
        e = [jnp.exp(a - mx) for a in lse]
        den = e[0] + e[1] + e[2]
        y = sum((e[g] / den) * groups[g][:, j * dh:(j + 1) * dh] for g in range(3))
        y_ref[:, j * dh:(j + 1) * dh] = y.astype(y_ref.dtype)


def _dilated_merge(packed, batch, seq, tm=256):
    n = batch * seq
    w = packed[0].shape[-1]
    per_seq = seq // tm
    dils = [d for _, d in B_GROUPS]
    assert dils[0] == 1 and len(dils) == 3

    def stream(dil):
        return pl.BlockSpec((1, dil, tm // dil, w), lambda i: (i // per_seq, 0, i % per_seq, 0))

    pspec = pl.BlockSpec((tm, tm), lambda i: (0, 0))
    return pl.pallas_call(
        _dil_merge_kernel,
        grid=(n // tm,),
        in_specs=[pl.BlockSpec((tm, w), lambda i: (i, 0)), stream(dils[1]), stream(dils[2]), pspec, pspec],
        out_specs=pl.BlockSpec((tm, B_OUT), lambda i: (i, 0)),
        out_shape=jax.ShapeDtypeStruct((n, B_OUT), BF16),
        compiler_params=_params("arbitrary"),
        name="dilated_merge",
    )(packed[0].reshape(n, w), packed[1], packed[2],
      _deinterleave_matrix(tm, dils[1]).T, _deinterleave_matrix(tm, dils[2]).T)


def _merge_kernel(h_ref, ya_ref, yb_ref, wga_ref, wgb_ref, wa_ref, wb_ref, o_ref):
    h = h_ref[...]
    ga = jax.nn.sigmoid(_dot(h, wga_ref[...]))
    gb = jax.nn.sigmoid(_dot(h, wgb_ref[...]))
    merged = ga * _dot(ya_ref[...], wa_ref[...]) + gb * _dot(yb_ref[...], wb_ref[...])
    o_ref[...] = merged.astype(o_ref.dtype)


def _gated_merge(h, ya, yb, w_tail, gate_col, wa, wb, tm=1024, tn=512):
    n, d = h.shape
    assert gate_col % tn == 0 and d % tn == 0
    ga0, gb0 = gate_col // tn, (gate_col + d) // tn

    def rows(k):
        return pl.BlockSpec((tm, k), lambda i, j: (i, 0))

    def cols(k):
        return pl.BlockSpec((k, tn), lambda i, j: (0, j))

    return pl.pallas_call(
        _merge_kernel,
        grid=(n // tm, d // tn),
        in_specs=[rows(d), rows(A_OUT), rows(B_OUT),
                  pl.BlockSpec((d, tn), lambda i, j: (0, ga0 + j)),
                  pl.BlockSpec((d, tn), lambda i, j: (0, gb0 + j)),
                  cols(A_OUT), cols(B_OUT)],
        out_specs=pl.BlockSpec((tm, tn), lambda i, j: (i, j)),
        out_shape=jax.ShapeDtypeStruct((n, d), BF16),
        compiler_params=_params("arbitrary", "arbitrary"),
        name="gated_merge",
    )(h, ya, yb, w_tail, w_tail, wa, wb)


def _res_norm_kernel(a_ref, w_ref, x_ref, gate_ref, g_ref, sc_ref, sh_ref, xo_ref, h_ref):
    xn = x_ref[...] + gate_ref[0] * _dot(a_ref[...], w_ref[...])
    xo_ref[...] = xn
    h_ref[...] = (_rms(xn) * g_ref[...] * (1.0 + sc_ref[0]) + sh_ref[0]).astype(h_ref.dtype)


def _res_final_kernel(a_ref, w_ref, x_ref, gate_ref, g_ref, o_ref):
    xn = x_ref[...] + gate_ref[0] * _dot(a_ref[...], w_ref[...])
    o_ref[...] = _rms(xn) * g_ref[...]


def _matmul_residual_norm(a, w, x, gate, g, scale, shift, seq, tm, name):
    n, k = a.shape
    d = w.shape[1]
    per = seq // tm
    row = pl.BlockSpec((tm, d), lambda i: (i, 0))
    vec = pl.BlockSpec((1, 1, d), lambda i: (i // per, 0, 0))
    in_specs = [
        pl.BlockSpec((tm, k), lambda i: (i, 0)),
        pl.BlockSpec((k, d), lambda i: (0, 0), pipeline_mode=pl.Buffered(1)),
        row, vec,
        pl.BlockSpec((1, d), lambda i: (0, 0)),
    ]
    args = [a, w, x, gate, g.reshape(1, d)]
    if scale is None:
        return pl.pallas_call(
            _res_final_kernel, grid=(n // tm,), in_specs=in_specs, out_specs=row,
            out_shape=jax.ShapeDtypeStruct((n, d), F32),
            compiler_params=_params("arbitrary"), name=name,
        )(*args)
    return pl.pallas_call(
        _res_norm_kernel, grid=(n // tm,), in_specs=in_specs + [vec, vec], out_specs=[row, row],
        out_shape=[jax.ShapeDtypeStruct((n, d), F32), jax.ShapeDtypeStruct((n, d), BF16)],
        compiler_params=_params("arbitrary"), name=name,
    )(*args, scale, shift)


def _ffn_up_kernel(h_ref, wg_ref, wu_ref, o_ref, wg_scr, wu_scr):
    @pl.when(pl.program_id(1) == 0)
    def _():
        wg_scr[...] = wg_ref[...].astype(BF16)
        wu_scr[...] = wu_ref[...].astype(BF16)

    h = h_ref[...]
    a = _dot(h, wg_scr[...])
    o_ref[...] = (a * jax.nn.sigmoid(a) * _dot(h, wu_scr[...])).astype(o_ref.dtype)


def _ffn_up(h, wg, wu, layer, tm=1024, tn=512):
    n, d = h.shape
    f = wg.shape[2]
    wspec = pl.BlockSpec((None, d, tn), lambda j, i: (layer, 0, j))
    return pl.pallas_call(
        _ffn_up_kernel,
        grid=(f // tn, n // tm),
        in_specs=[pl.BlockSpec((tm, d), lambda j, i: (i, 0)), wspec, wspec],
        out_specs=pl.BlockSpec((tm, tn), lambda j, i: (i, j)),
        out_shape=jax.ShapeDtypeStruct((n, f), BF16),
        scratch_shapes=[pltpu.VMEM((d, tn), BF16), pltpu.VMEM((d, tn), BF16)],
        compiler_params=_params("arbitrary", "arbitrary"),
        name="ffn_up",
    )(h, wg, wu)


def _rope_tables(seq):
    half = IDX_ROPE_DIM // 2
    freqs = ROPE_THETA ** (-jnp.arange(half, dtype=F32) / half)
    ang = jnp.arange(seq).astype(F32)[:, None] * freqs[None, :]
    cos, sin = jnp.cos(ang), jnp.sin(ang)
    rest = IDX_HEAD_DIM - IDX_ROPE_DIM
    one = jnp.ones((seq, rest), F32)
    zr = jnp.zeros((seq, rest), F32)
    zh = jnp.zeros((seq, half), F32)
    cos_t = jnp.concatenate([cos, cos, one], axis=1)
    sin_lo = jnp.concatenate([-sin, zh, zr], axis=1)
    sin_hi = jnp.concatenate([zh, sin, zr], axis=1)
    rep = LANES // IDX_HEAD_DIM
    return tuple(jnp.tile(t, (1, rep)) for t in (cos_t, sin_lo, sin_hi))


def _layer(layer, x, h, h_mod, next_norm, tables, bias_a, bias_b, batch, seq, w_in, kv_norm_g, idx_ln_g,
           idx_ln_b, w_uk, w_uv, w_a_up, w_b_up, w_out, norm2_g, w_ff_gate, w_ff_up, w_ff_down):
    _, _, gate1, shift2, scale2, gate2 = h_mod
    offs = [0]
    for s in IN_SIZES:
        offs.append(offs[-1] + s)
    w_qa, w_kv, w_qi, w_ki, w_wi = [w_in[:, offs[k]:offs[k + 1]] for k in range(5)]
    w_tail = w_in[:, offs[5]:].astype(BF16)
    w_qkvb = w_tail[:, :B_QKV_COLS]
    d = w_in.shape[0]
    rep = LANES // IDX_HEAD_DIM
    w_kw = jnp.concatenate(
        [w_ki] * rep + [w_wi, jnp.zeros((d, LANES - IDX_HEADS), w_in.dtype)], axis=1).astype(BF16)
    ln_g2 = jnp.tile(idx_ln_g, rep).reshape(1, LANES)
    ln_b2 = jnp.tile(idx_ln_b, rep).reshape(1, LANES)

    ql = _proj_qlat(h, w_qa.astype(BF16), w_uk.astype(BF16))
    ckv, ckv_t = _proj_ckv(h, w_kv.astype(BF16), kv_norm_g, DSA_TILE)
    qi = _proj_qi(h, w_qi.astype(BF16), tables, seq)
    k2, wi_t = _proj_kw(h, w_kw, ln_g2, ln_b2, tables, seq)
    wuv_t = jnp.swapaxes(w_uv, 1, 2).astype(BF16)
    ya = _dsa_attention(qi, wi_t, k2, ql, ckv, ckv_t, bias_a, wuv_t, batch, seq, DSA_TILE)

    ng = len(B_GROUPS)
    packed = []
    for g, (_, dil) in enumerate(B_GROUPS):
        w_g = jnp.concatenate([w_qkvb[:, (s * ng + g) * B_OUT:(s * ng + g + 1) * B_OUT] for s in range(3)],
                              axis=1)
        packed.append(_dilated_group(_proj_dilated(h, w_g, batch, seq, dil), bias_b, g, batch, seq))
    yb = _dilated_merge(packed, batch, seq)

    merged = _gated_merge(h, ya, yb, w_tail, B_QKV_COLS, w_a_up.astype(BF16), w_b_up.astype(BF16))
    x, h2 = _matmul_residual_norm(merged, w_out.astype(BF16), x, gate1, norm2_g, scale2, shift2, seq,
                                  ROWS_OUT_PROJ, "out_proj_residual")
    act = _ffn_up(h2, w_ff_gate, w_ff_up, layer)
    return _matmul_residual_norm(act, w_ff_down.astype(BF16), x, gate2, *next_norm, seq, ROWS_FFN_DOWN,
                                 "ffn_down_residual")


def kernel(x, c, rel_bias, w_ada, b_ada, norm1_g, w_in, kv_norm_g, idx_ln_g, idx_ln_b, w_uk, w_uv,
           w_a_up, w_b_up, w_out, norm2_g, w_ff_gate, w_ff_up, w_ff_down, final_g):
    batch, seq, d = x.shape
    depth = w_ada.shape[0]
    assert d == D_MODEL and seq % (B_GROUPS[-1][1] * BAND_BLOCK) == 0
    n = batch * seq
    rows = -(-batch // SUBLANES) * SUBLANES
    mod = _modulation(jnp.pad(c, ((0, rows - batch), (0, 0))), w_ada, b_ada)
    mod = mod[:, :batch].reshape(depth, batch, 6, 1, d)
    bias_a = _bias_a_table(rel_bias, min(seq // BAND_BLOCK, FAR_DELTA + 1))
    bias_b = _bias_b_table(rel_bias)
    tables = _rope_tables(seq)
    xf = x.reshape(n, d)
    h_mods = [[mod[l, :, k] for k in range(6)] for l in range(depth)]
    h = _norm_mod(xf, norm1_g[0], h_mods[0][1], h_mods[0][0], seq)
    for l in range(depth):
        last = l + 1 == depth
        next_norm = (final_g, None, None) if last else (norm1_g[l + 1], h_mods[l + 1][1], h_mods[l + 1][0])
        out = _layer(l, xf, h, h_mods[l], next_norm, tables, bias_a, bias_b, batch, seq, w_in[l],
                     kv_norm_g[l], idx_ln_g[l], idx_ln_b[l], w_uk[l], w_uv[l], w_a_up[l], w_b_up[l], w_out[l],
                     norm2_g[l], w_ff_gate, w_ff_up, w_ff_down[l])
        if last:
            return out.reshape(batch, seq, d)
        xf, h = out
```
